```python
import jax
import jax.numpy as jnp
from jax import lax
import numpy as np

D_MODEL = 1024
BATCH = 16
SEQ = 4096
DEPTH = 4

CTX_LEN = 256
GRID_W = 64

POOL_GROUPS = 4
POOL_GROUP_DIM = D_MODEL // 16
POOL_WIDTH = POOL_GROUPS * POOL_GROUP_DIM
POOL_WINDOWS = (2, 4, 8, 16)

MLSTM_HEADS = 4
MLSTM_HEAD_DIM = D_MODEL // 8
MLSTM_WIDTH = MLSTM_HEADS * MLSTM_HEAD_DIM
MLSTM_CHUNK = 128
QK_CONV_WIDTH = 3

SGU_GROUPS = 4
SGU_GROUP_DIM = D_MODEL // 16
SGU_WIDTH = SGU_GROUPS * SGU_GROUP_DIM
SGU_CHUNK = 128

N_BRANCH = 3
IN_SPLIT_SIZES = (POOL_WIDTH, 2 * MLSTM_WIDTH, MLSTM_WIDTH, MLSTM_WIDTH, 4 * MLSTM_HEADS, 2 * SGU_WIDTH, N_BRANCH * D_MODEL)
IN_COLS = sum(IN_SPLIT_SIZES)
MLSTM_GATE_OFF = POOL_WIDTH + 4 * MLSTM_WIDTH

N_EXPERTS = 32
TOP_K = 4
D_FF = D_MODEL
SWIGLU_LIMIT = 7.0
SWIGLU_ALPHA = 1.702

LN_EPS = 1e-5
HEAD_NORM_EPS = 1e-6
DEEPNORM_ALPHA = (2 * DEPTH) ** 0.25
DEEPNORM_BETA = (8 * DEPTH) ** -0.25

kernel_name = 'pool_mlstm_sgu_moe_flow_block'


def layer_norm_plain(x):
    xf = x.astype(jnp.float32)
    mu = xf.mean(-1, keepdims=True)
    var = jnp.square(xf - mu).mean(-1, keepdims=True)
    return ((xf - mu) * lax.rsqrt(var + LN_EPS)).astype(x.dtype)


def layer_norm(x, g, b):
    return layer_norm_plain(x) * g + b


def modulate(h, shift, scale):
    return h * (1 + scale) + shift


def split_in(z):
    idx = [int(s) for s in np.cumsum(IN_SPLIT_SIZES)[:-1]]
    return jnp.split(z, idx, axis=-1)


def centred_box_mean(x, window):
    length = x.shape[-2]
    xf = x.astype(jnp.float32)
    prefix = jnp.concatenate([jnp.zeros_like(xf[..., :1, :]), jnp.cumsum(xf, axis=-2)], axis=-2)
    t = jnp.arange(length)
    lo = jnp.clip(t - window // 2, 0, length)
    hi = jnp.clip(t + window // 2, 0, length)
    s = jnp.take(prefix, hi, axis=-2) - jnp.take(prefix, lo, axis=-2)
    cnt = (hi - lo).astype(jnp.float32)[:, None]
    return (s / cnt).astype(x.dtype)


def pool_mixer(xp, w_lin, scale):
    grp = xp.reshape(xp.shape[:-1] + (POOL_GROUPS, POOL_GROUP_DIM))
    pooled = jnp.stack([centred_box_mean(grp[..., g, :], w) for g, w in enumerate(POOL_WINDOWS)], axis=-2)
    y = jnp.einsum('...lgc,gcd->...lgd', pooled - grp, w_lin)
    return y.reshape(xp.shape) * scale


def dwconv_centred(x, w, b):
    k = w.shape[0]
    length = x.shape[1]
    left = k // 2
    xp = jnp.pad(x, ((0, 0), (left, k - 1 - left), (0, 0)))
    y = b
    for j in range(k):
        y = y + xp[:, j:j + length, :] * w[j]
    return y


def mlstm_prepare(qk, v, gates, conv_w, conv_b):
    qk = jax.nn.silu(dwconv_centred(qk, conv_w, conv_b))
    q, k = jnp.split(qk, 2, axis=-1)
    b, length, _ = v.shape

    def heads(a):
        return a.reshape(b, length, MLSTM_HEADS, MLSTM_HEAD_DIM).transpose(0, 2, 1, 3).astype(jnp.float32)

    g = gates.astype(jnp.float32).reshape(b, length, 4, MLSTM_HEADS).transpose(2, 0, 3, 1)
    return heads(q), heads(k) * MLSTM_HEAD_DIM ** -0.5, heads(v), g


def mlstm_chunk_scan(q, k, v, log_i, log_f, state):
    b, h, length, dh = q.shape
    nc = length // MLSTM_CHUNK
    lower = jnp.tril(jnp.ones((MLSTM_CHUNK, MLSTM_CHUNK), dtype=bool))

    def chunks(a):
        return jnp.moveaxis(a.reshape((b, h, nc, MLSTM_CHUNK) + a.shape[3:]), 2, 0)

    def step(carry, xs):
        c_mat, n_vec, m = carry
        qc, kc, vc, ic, fc = xs
        bcum = jnp.cumsum(fc, axis=-1)
        dmat = jnp.where(lower, bcum[..., :, None] - bcum[..., None, :] + ic[..., None, :], -jnp.inf)
        inter = bcum + m[..., None]
        m_t = jnp.maximum(inter, dmat.max(-1))
        decay = jnp.exp(inter - m_t)
        scores = jnp.einsum('bhtd,bhsd->bhts', qc, kc) * jnp.exp(dmat - m_t[..., None])
        num = decay[..., None] * jnp.einsum('bhvk,bhtk->bhtv', c_mat, qc) + jnp.einsum('bhts,bhsv->bhtv', scores, vc)
        den = decay * jnp.einsum('bhk,bhtk->bht', n_vec, qc) + scores.sum(-1)
        out = num / jnp.maximum(jnp.abs(den), jnp.exp(-m_t))[..., None]
        total = bcum[..., -1]
        wlog = total[..., None] - bcum + ic
        m_new = jnp.maximum(total + m, wlog.max(-1))
        a = jnp.exp(total + m - m_new)
        wk = jnp.exp(wlog - m_new[..., None])
        c_new = a[..., None, None] * c_mat + jnp.einsum('bhsv,bhsk->bhvk', vc * wk[..., None], kc)
        n_new = a[..., None] * n_vec + jnp.einsum('bhs,bhsk->bhk', wk, kc)
        return (c_new, n_new, m_new), out

    state, hs = lax.scan(step, state, tuple(chunks(a) for a in (q, k, v, log_i, log_f)))
    return jnp.moveaxis(hs, 0, 2).reshape(b, h, length, dh), state


def mlstm_bidirectional(ctx_in, lat_in):
    qc, kc, vc, gc = ctx_in
    ql, kl, vl, gl = lat_in
    b = qc.shape[0]
    zero = (jnp.zeros((b, MLSTM_HEADS, MLSTM_HEAD_DIM, MLSTM_HEAD_DIM), jnp.float32),
            jnp.zeros((b, MLSTM_HEADS, MLSTM_HEAD_DIM), jnp.float32),
            jnp.zeros((b, MLSTM_HEADS), jnp.float32))
    ls = jax.nn.log_sigmoid
    hc_f, st_f = mlstm_chunk_scan(qc, kc, vc, gc[0], ls(gc[1]), zero)
    hl_f, _ = mlstm_chunk_scan(ql, kl, vl, gl[0], ls(gl[1]), st_f)
    fs = lambda a: jnp.flip(a, axis=2)
    fg = lambda a: jnp.flip(a, axis=-1)
    hc_b, st_b = mlstm_chunk_scan(fs(qc), fs(kc), fs(vc), fg(gc[2]), ls(fg(gc[3])), zero)
    hl_b, _ = mlstm_chunk_scan(fs(ql), fs(kl), fs(vl), fg(gl[2]), ls(fg(gl[3])), st_b)
    return hc_f + fs(hc_b), hl_f + fs(hl_b)


def mlstm_output(h, o, norm_g):
    b, _, length, _ = h.shape
    mu = h.mean(-1, keepdims=True)
    var = jnp.square(h - mu).mean(-1, keepdims=True)
    hn = ((h - mu) * lax.rsqrt(var + HEAD_NORM_EPS)).transpose(0, 2, 1, 3).reshape(b, length, MLSTM_WIDTH)
    return (hn * norm_g * jax.nn.sigmoid(o.astype(jnp.float32))).astype(o.dtype)


def sgu_mixer(uv, ln_g, ln_b, w_s, b_s):
    u, v = jnp.split(jax.nn.gelu(uv, approximate=False), 2, axis=-1)
    v = layer_norm(v, ln_g, ln_b)
    b, length, _ = v.shape
    vc = v.reshape(b, length // SGU_CHUNK, SGU_CHUNK, SGU_GROUPS, SGU_GROUP_DIM)
    mixed = jnp.einsum('gts,bnsgc->bntgc', w_s, vc) + b_s.T[:, :, None]
    return u * mixed.reshape(b, length, SGU_WIDTH)


def branch_merge(pool_o, mlstm_o, sgu_o, gate_pre, w_bp, w_bm, w_bs, w_o):
    g = jax.nn.sigmoid(gate_pre.astype(jnp.float32)).astype(pool_o.dtype)
    g_p, g_m, g_s = jnp.split(g, N_BRANCH, axis=-1)
    y = g_p * (pool_o @ w_bp) + g_m * (mlstm_o @ w_bm) + g_s * (sgu_o @ w_bs)
    return y @ w_o


def moe_ffn(h, w_router, b_router, w_gate_up, b_gate_up, w_down, b_down):
    logits = (h @ w_router + b_router).astype(jnp.float32)
    top_val, top_idx = lax.top_k(logits, TOP_K)
    probs = jax.nn.softmax(top_val, axis=-1)
    combine = jnp.einsum('nk,nke->en', probs, jax.nn.one_hot(top_idx, N_EXPERTS, dtype=jnp.float32)).astype(h.dtype)

    def expert(acc, xs):
        wgu, bgu, wd, bd, cw = xs
        gate, up = jnp.split(h @ wgu + bgu, 2, axis=-1)
        gate = jnp.minimum(gate, SWIGLU_LIMIT)
        up = jnp.clip(up, -SWIGLU_LIMIT, SWIGLU_LIMIT)
        glu = gate * jax.nn.sigmoid(SWIGLU_ALPHA * gate)
        y = ((up + 1) * glu) @ wd + bd
        return acc + cw[:, None] * y, None

    out, _ = lax.scan(expert, jnp.zeros_like(h), (w_gate_up, b_gate_up, w_down, b_down, combine))
    return out


def setup_inputs(seed: int = 0) -> dict:
    key = jax.random.key(seed)
    ks = iter(jax.random.split(key, 40))

    def nrm(shape, scale):
        return scale * jax.random.normal(next(ks), shape, jnp.float32)

    L, D, E, F = DEPTH, D_MODEL, N_EXPERTS, D_FF
    f_bias = jnp.linspace(3.0, 6.0, MLSTM_HEADS, dtype=jnp.float32)
    b_in = nrm((L, IN_COLS), 0.02)
    b_in = b_in.at[:, MLSTM_GATE_OFF + MLSTM_HEADS:MLSTM_GATE_OFF + 2 * MLSTM_HEADS].add(f_bias)
    b_in = b_in.at[:, MLSTM_GATE_OFF + 3 * MLSTM_HEADS:MLSTM_GATE_OFF + 4 * MLSTM_HEADS].add(f_bias)
    return {
        'x': nrm((BATCH, SEQ, D), 1.0),
        'c': nrm((BATCH, D), 1.0),
        'ctx': nrm((BATCH, CTX_LEN, D), 1.0),
        'c_ctx': nrm((D,), 1.0),
        'w_ada': nrm((L, D, 6 * D), 0.5 * D ** -0.5),
        'b_ada': nrm((L, 6 * D), 0.02),
        'w_in': nrm((L, D, IN_COLS), D ** -0.5),
        'b_in': b_in,
        'pool_w': nrm((L, POOL_GROUPS, POOL_GROUP_DIM, POOL_GROUP_DIM), POOL_GROUP_DIM ** -0.5),
        'pool_scale': 1.0 + nrm((L, POOL_WIDTH), 0.02),
        'qk_conv_w': nrm((L, QK_CONV_WIDTH, 2 * MLSTM_WIDTH), QK_CONV_WIDTH ** -0.5),
        'qk_conv_b': nrm((L, 2 * MLSTM_WIDTH), 0.02),
        'mlstm_norm_g': 1.0 + nrm((L, MLSTM_WIDTH), 0.02),
        'sgu_ln_g': 1.0 + nrm((L, SGU_WIDTH), 0.02),
        'sgu_ln_b': nrm((L, SGU_WIDTH), 0.02),
        'sgu_w': nrm((L, SGU_GROUPS, SGU_CHUNK, SGU_CHUNK), SGU_CHUNK ** -0.5),
        'sgu_b': 1.0 + nrm((L, SGU_GROUPS, SGU_CHUNK), 0.02),
        'w_br_pool': nrm((L, POOL_WIDTH, D), POOL_WIDTH ** -0.5),
        'w_br_mlstm': nrm((L, MLSTM_WIDTH, D), MLSTM_WIDTH ** -0.5),
        'w_br_sgu': nrm((L, SGU_WIDTH, D), SGU_WIDTH ** -0.5),
        'w_out': nrm((L, D, D), DEEPNORM_BETA * D ** -0.5),
        'ln1_g': 1.0 + nrm((L, D), 0.02),
        'ln1_b': nrm((L, D), 0.02),
        'w_router': nrm((L, D, E), D ** -0.5),
        'b_router': nrm((L, E), 0.01),
        'w_gate_up': nrm((L, E, D, 2 * F), D ** -0.5),
        'b_gate_up': nrm((L, E, 2 * F), 0.02),
        'w_down': nrm((L, E, F, D), DEEPNORM_BETA * F ** -0.5),
        'b_down': nrm((L, E, D), 0.02),
        'ln2_g': 1.0 + nrm((L, D), 0.02),
        'ln2_b': nrm((L, D), 0.02),
    }


def reference(x, c, ctx, c_ctx, w_ada, b_ada, w_in, b_in, pool_w, pool_scale, qk_conv_w, qk_conv_b,
              mlstm_norm_g, sgu_ln_g, sgu_ln_b, sgu_w, sgu_b, w_br_pool, w_br_mlstm, w_br_sgu, w_out,
              ln1_g, ln1_b, w_router, b_router, w_gate_up, b_gate_up, w_down, b_down, ln2_g, ln2_b):
    B, S, D = x.shape
    rows = S // GRID_W
    xl, xc = x, ctx
    for i in range(DEPTH):
        last = i == DEPTH - 1
        sh1_l, sc1_l, g1_l, sh2_l, sc2_l, g2_l = [
            m[:, None, :] for m in jnp.split(jax.nn.silu(c) @ w_ada[i] + b_ada[i], 6, axis=-1)]
        sh1_c, sc1_c, g1_c, sh2_c, sc2_c, g2_c = jnp.split(jax.nn.silu(c_ctx) @ w_ada[i] + b_ada[i], 6, axis=-1)

        z_l = modulate(layer_norm_plain(xl), sh1_l, sc1_l) @ w_in[i] + b_in[i]
        z_c = modulate(layer_norm_plain(xc), sh1_c, sc1_c) @ w_in[i] + b_in[i]
        p_l, qk_l, v_l, o_l, gt_l, uv_l, mg_l = split_in(z_l)
        p_c, qk_c, v_c, o_c, gt_c, uv_c, mg_c = split_in(z_c)

        hm_c, hm_l = mlstm_bidirectional(
            mlstm_prepare(qk_c, v_c, gt_c, qk_conv_w[i], qk_conv_b[i]),
            mlstm_prepare(qk_l, v_l, gt_l, qk_conv_w[i], qk_conv_b[i]))
        merge_w = (w_br_pool[i], w_br_mlstm[i], w_br_sgu[i], w_out[i])
        moe_w = (w_router[i], b_router[i], w_gate_up[i], b_gate_up[i], w_down[i], b_down[i])

        y_l = branch_merge(
            pool_mixer(p_l.reshape(B, rows, GRID_W, POOL_WIDTH), pool_w[i], pool_scale[i]).reshape(B, S, POOL_WIDTH),
            mlstm_output(hm_l, o_l, mlstm_norm_g[i]),
            sgu_mixer(uv_l, sgu_ln_g[i], sgu_ln_b[i], sgu_w[i], sgu_b[i]),
            mg_l, *merge_w)
        xl = layer_norm(DEEPNORM_ALPHA * xl + g1_l * y_l, ln1_g[i], ln1_b[i])
        h_l = modulate(layer_norm_plain(xl), sh2_l, sc2_l).reshape(B * S, D)

        if last:
            f_l = moe_ffn(h_l, *moe_w)
        else:
            y_c = branch_merge(
                pool_mixer(p_c, pool_w[i], pool_scale[i]),
                mlstm_output(hm_c, o_c, mlstm_norm_g[i]),
                sgu_mixer(uv_c, sgu_ln_g[i], sgu_ln_b[i], sgu_w[i], sgu_b[i]),
                mg_c, *merge_w)
            xc = layer_norm(DEEPNORM_ALPHA * xc + g1_c * y_c, ln1_g[i], ln1_b[i])
            h_c = modulate(layer_norm_plain(xc), sh2_c, sc2_c).reshape(-1, D)
            f_all = moe_ffn(jnp.concatenate([h_l, h_c], axis=0), *moe_w)
            f_l, f_c = f_all[:B * S], f_all[B * S:]
            xc = layer_norm(DEEPNORM_ALPHA * xc + g2_c * f_c.reshape(xc.shape), ln2_g[i], ln2_b[i])
        xl = layer_norm(DEEPNORM_ALPHA * xl + g2_l * f_l.reshape(B, S, D), ln2_g[i], ln2_b[i])
    return xl
```

```python
import functools

import numpy as np
import jax
import jax.numpy as jnp
from jax import lax
from jax.experimental import pallas as pl
from jax.experimental.pallas import tpu as pltpu

F32 = jnp.float32
BF16 = jnp.bfloat16

GRID_W = 64
POOL_WINDOWS = (2, 4, 8, 16)
POOL_GROUP_DIM = 64
N_HEADS = 4
HEAD_DIM = 128
CHUNK = 128
TOP_K = 4
SWIGLU_LIMIT = 7.0
SWIGLU_ALPHA = 1.702
LN_EPS = 1e-5
HEAD_NORM_EPS = 1e-6

ROW_TILE = 256
EXPERT_TILE = 512
HALO = 16
LANES = 128
NEG_BIG = -1e30
VMEM_LIMIT = 56 * 1024 * 1024


def _cparams(*sem):
    return pltpu.CompilerParams(dimension_semantics=sem, vmem_limit_bytes=VMEM_LIMIT)


def _ln_plain(x):
    mu = jnp.mean(x, axis=-1, keepdims=True)
    xc = x - mu
    var = jnp.mean(xc * xc, axis=-1, keepdims=True)
    return xc * lax.rsqrt(var + LN_EPS)


def _sigmoid(x):
    return 1.0 / (1.0 + jnp.exp(-x))


def _log_sigmoid(x):
    return jnp.minimum(x, 0.0) - jnp.log1p(jnp.exp(-jnp.abs(x)))


def _ada_kernel(c_ref, w_ref, b_ref, o_ref):
    c = c_ref[...]
    s = c * _sigmoid(c)
    o_ref[...] = jnp.dot(s, w_ref[...], preferred_element_type=F32,
                         precision=lax.Precision.HIGHEST) + b_ref[...]


def _ada_table(cc, w_ada, b_ada):
    depth, d, n = w_ada.shape
    rows = cc.shape[0]
    bn = 1536
    return pl.pallas_call(
        _ada_kernel,
        grid=(depth, n // bn),
        in_specs=[
            pl.BlockSpec((rows, d), lambda l, j: (0, 0)),
            pl.BlockSpec((None, d, bn), lambda l, j: (l, 0, j)),
            pl.BlockSpec((None, 1, bn), lambda l, j: (l, 0, j)),
        ],
        out_specs=pl.BlockSpec((None, rows, bn), lambda l, j: (l, 0, j)),
        out_shape=jax.ShapeDtypeStruct((depth, rows, n), F32),
        compiler_params=_cparams("parallel", "parallel"),
    )(cc, w_ada, b_ada.reshape(depth, 1, n))


_Z_PARTS = (("qk", 1024), ("v", 512), ("o", 512), ("uv", 512), ("pool", 256), ("mg", 3072))
_Z_TOTAL = sum(w for _, w in _Z_PARTS)
_N_GATE_COLS = 4 * N_HEADS


def _inproj_kernel(x_ref, mod_ref, w_ref, b_ref, wg_ref, bg_ref, wgt_ref, bgt_ref,
                   qk_ref, v_ref, o_ref, uv_ref, p_ref, mg_ref, g_ref, gt_ref):
    d = x_ref.shape[1]
    xn = _ln_plain(x_ref[...])
    h = xn * (1.0 + mod_ref[:, d:2 * d]) + mod_ref[:, 0:d]
    hb = h.astype(BF16)
    off = 0
    for ref, (_, width) in zip((qk_ref, v_ref, o_ref, uv_ref, p_ref, mg_ref), _Z_PARTS):
        for c0 in range(0, width, 512):
            cw = min(512, width - c0)
            z = jnp.dot(hb, w_ref[:, off + c0:off + c0 + cw], preferred_element_type=F32)
            ref[:, c0:c0 + cw] = (z + b_ref[:, off + c0:off + c0 + cw]).astype(ref.dtype)
        off += width
    g_ref[...] = jnp.dot(hb, wg_ref[...], preferred_element_type=F32) + bg_ref[...]
    gt_ref[...] = lax.dot_general(wgt_ref[...], hb, (((1,), (1,)), ((), ())),
                                  preferred_element_type=F32) + bgt_ref[...]


def _inproj(x, mods, w, b, wg, bg, wgt, bgt, *, n_lat, seq):
    t, d = x.shape
    tm = ROW_TILE
    n_batch = mods.shape[0] - 1

    def mod_idx(i):
        return (jnp.minimum((i * tm) // seq, n_batch), 0, 0)

    const = lambda i: (0, 0)
    row = lambda i: (i, 0)
    out_shapes = [jax.ShapeDtypeStruct((t, width), BF16) for _, width in _Z_PARTS]
    out_specs = [pl.BlockSpec((tm, width), row) for _, width in _Z_PARTS]
    out_shapes += [jax.ShapeDtypeStruct((t, LANES), F32), jax.ShapeDtypeStruct((_N_GATE_COLS, t), F32)]
    out_specs += [pl.BlockSpec((tm, LANES), row), pl.BlockSpec((_N_GATE_COLS, tm), lambda i: (0, i))]
    return pl.pallas_call(
        _inproj_kernel,
        grid=(t // tm,),
        in_specs=[
            pl.BlockSpec((tm, d), row),
            pl.BlockSpec((None, 1, mods.shape[2]), mod_idx),
            pl.BlockSpec(w.shape, const),
            pl.BlockSpec(b.shape, const),
            pl.BlockSpec(wg.shape, const),
            pl.BlockSpec(bg.shape, const),
            pl.BlockSpec(wgt.shape, const),
            pl.BlockSpec(bgt.shape, const),
        ],
        out_specs=out_specs,
        out_shape=out_shapes,
        compiler_params=_cparams("parallel"),
    )(x, mods, w, b, wg, bg, wgt, bgt)


def _mlstm_chunk_index(j, b, direction, *, n_batch, seq, ctx_len):
    ncc = ctx_len // CHUNK
    ncl = seq // CHUNK
    is_ctx = j < ncc
    if direction == 0:
        c = jnp.where(is_ctx, j, j - ncc)
    else:
        c = jnp.where(is_ctx, ncc - 1 - j, ncl - 1 - (j - ncc))
    base = jnp.where(is_ctx, (n_batch * seq + b * ctx_len) // CHUNK, (b * seq) // CHUNK)
    n = jnp.where(is_ctx, ncc, ncl)
    return base + c, c, n


def _mlstm_kernel(*refs, n_batch, seq, ctx_len):
    (qk_f, pv_f, nx_f, v_f, g_f, gt_f, qk_b, pv_b, nx_b, v_b, g_b, gt_b,
     cw_ref, cb_ref, hf_ref, hb_ref, s_sc, m_sc, pad_sc) = refs
    b = pl.program_id(0)
    j = pl.program_id(1)

    @pl.when(j == 0)
    def _():
        s_sc[...] = jnp.zeros_like(s_sc)
        m_sc[...] = jnp.zeros_like(m_sc)

    ri = lax.broadcasted_iota(jnp.int32, (CHUNK, CHUNK), 0)
    ci = lax.broadcasted_iota(jnp.int32, (CHUNK, CHUNK), 1)
    ones_tile = jnp.ones((CHUNK, HEAD_DIM), BF16)
    w0 = cw_ref[0:1, :]
    w1 = cw_ref[1:2, :]
    w2 = cw_ref[2:3, :]
    dirs = ((qk_f, pv_f, nx_f, v_f, g_f, gt_f, hf_ref), (qk_b, pv_b, nx_b, v_b, g_b, gt_b, hb_ref))
    for d, (qk_ref, pv_ref, nx_ref, v_ref, g_ref, gt_ref, out_ref) in enumerate(dirs):
        _, c, n = _mlstm_chunk_index(j, b, d, n_batch=n_batch, seq=seq, ctx_len=ctx_len)
        has_prev = (c > 0).astype(F32)
        has_next = (c < n - 1).astype(F32)
        pad_sc[d, HALO:HALO + CHUNK, :] = qk_ref[...].astype(F32)
        pad_sc[d, HALO - 1:HALO, :] = pv_ref[HALO - 1:HALO, :].astype(F32) * has_prev
        pad_sc[d, HALO + CHUNK:HALO + CHUNK + 1, :] = nx_ref[0:1, :].astype(F32) * has_next
        y = (cb_ref[...] + pad_sc[d, HALO - 1:HALO - 1 + CHUNK, :] * w0
             + pad_sc[d, HALO:HALO + CHUNK, :] * w1 + pad_sc[d, HALO + 1:HALO + 1 + CHUNK, :] * w2)
        qk_act = y * _sigmoid(y)
        width = N_HEADS * HEAD_DIM
        q_all = qk_act[:, :width].astype(BF16)
        k_all = qk_act[:, width:] * (HEAD_DIM ** -0.5)
        if d == 0:
            tri = (ci <= ri)
        else:
            tri = (ci >= ri)
        tri_f = tri.astype(F32)
        tri_t = (ri <= ci).astype(F32) if d == 0 else (ri >= ci).astype(F32)
        g = g_ref[...]
        gt = gt_ref[...]
        cum = jnp.dot(tri_f, _log_sigmoid(g), preferred_element_type=F32, precision=lax.Precision.HIGHEST)
        cum_t = jnp.dot(_log_sigmoid(gt), tri_t, preferred_element_type=F32, precision=lax.Precision.HIGHEST)
        last = CHUNK - 1 if d == 0 else 0
        for hh in range(N_HEADS):
            col_i = 2 * d * N_HEADS + hh
            col_f = col_i + N_HEADS
            lo, hi = hh * HEAD_DIM, (hh + 1) * HEAD_DIM
            q = q_all[:, lo:hi]
            k = k_all[:, lo:hi]
            v1 = jnp.concatenate([v_ref[:, lo:hi], ones_tile], axis=1)
            bcum_col = cum[:, col_f:col_f + 1]
            bcum_row = cum_t[col_f:col_f + 1, :]
            li_col = g[:, col_i:col_i + 1]
            li_row = gt[col_i:col_i + 1, :]
            state = s_sc[d, hh]
            m_prev = m_sc[d * N_HEADS + hh][0:1, 0:1]
            dmat = jnp.where(tri, bcum_col - bcum_row + li_row, -jnp.inf)
            inter = bcum_col + m_prev
            m_t = jnp.maximum(inter, jnp.max(dmat, axis=-1, keepdims=True))
            decay = jnp.exp(inter - m_t)
            qk_s = lax.dot_general(q, k.astype(BF16), (((1,), (1,)), ((), ())), preferred_element_type=F32)
            scores = qk_s * jnp.exp(dmat - m_t)
            numden = (decay * jnp.dot(q, state.astype(BF16), preferred_element_type=F32)
                      + jnp.dot(scores.astype(BF16), v1, preferred_element_type=F32))
            num = numden[:, :HEAD_DIM]
            den = numden[:, HEAD_DIM:]
            out = num / jnp.maximum(jnp.abs(den), jnp.exp(-m_t))
            out_ref[:, lo:hi] = out.astype(out_ref.dtype)
            total = bcum_col[last:last + 1, :]
            wlog = total - bcum_col + li_col
            m_new = jnp.maximum(total + m_prev, jnp.max(wlog, axis=0, keepdims=True))
            a = jnp.exp(total + m_prev - m_new)
            kw = (k * jnp.exp(wlog - m_new)).astype(BF16)
            upd = lax.dot_general(kw, v1, (((0,), (0,)), ((), ())), preferred_element_type=F32)
            s_sc[d, hh] = a * state + upd
            m_sc[d * N_HEADS + hh] = jnp.broadcast_to(m_new, m_sc.shape[1:])


def _mlstm(zqk, zv, g, gt, conv_w, conv_b, *, n_batch, seq, ctx_len):
    t = zqk.shape[0]
    nsteps = (seq + ctx_len) // CHUNK
    width = N_HEADS * HEAD_DIM
    per_chunk = CHUNK // HALO
    idx = functools.partial(_mlstm_chunk_index, n_batch=n_batch, seq=seq, ctx_len=ctx_len)

    def specs(d):
        rb = lambda b, j: idx(j, b, d)[0]
        return [
            pl.BlockSpec((CHUNK, 2 * width), lambda b, j: (rb(b, j), 0)),
            pl.BlockSpec((HALO, 2 * width), lambda b, j: (jnp.maximum(rb(b, j) * per_chunk - 1, 0), 0)),
            pl.BlockSpec((HALO, 2 * width),
                         lambda b, j: (jnp.minimum((rb(b, j) + 1) * per_chunk, t // HALO - 1), 0)),
            pl.BlockSpec((CHUNK, width), lambda b, j: (rb(b, j), 0)),
            pl.BlockSpec((CHUNK, LANES), lambda b, j: (rb(b, j), 0)),
            pl.BlockSpec((_N_GATE_COLS, CHUNK), lambda b, j: (0, rb(b, j))),
        ]

    out_spec = lambda d: pl.BlockSpec((CHUNK, width), lambda b, j: (idx(j, b, d)[0], 0))
    kern = functools.partial(_mlstm_kernel, n_batch=n_batch, seq=seq, ctx_len=ctx_len)
    return pl.pallas_call(
        kern,
        grid=(n_batch, nsteps),
        in_specs=specs(0) + specs(1) + [
            pl.BlockSpec(conv_w.shape, lambda b, j: (0, 0)),
            pl.BlockSpec(conv_b.shape, lambda b, j: (0, 0)),
        ],
        out_specs=[out_spec(0), out_spec(1)],
        out_shape=[jax.ShapeDtypeStruct((t, width), BF16)] * 2,
        scratch_shapes=[
            pltpu.VMEM((2, N_HEADS, HEAD_DIM, 2 * HEAD_DIM), F32),
            pltpu.VMEM((2 * N_HEADS, 8, LANES), F32),
            pltpu.VMEM((2, CHUNK + 2 * HALO, 2 * width), F32),
        ],
        compiler_params=_cparams("parallel", "arbitrary"),
    )(zqk, zqk, zqk, zv, g, gt, zqk, zqk, zqk, zv, g, gt, conv_w, conv_b)


def _pool_constants(ctx_len):
    tm = ROW_TILE
    masks = np.zeros((2, len(POOL_WINDOWS), tm, tm), np.float32)
    cnts = np.zeros((2, tm, len(POOL_WINDOWS) * POOL_GROUP_DIM), np.float32)
    for kind, length in enumerate((GRID_W, ctx_len)):
        for gi, w in enumerate(POOL_WINDOWS):
            for t in range(tm):
                base = (t // length) * length
                tl = t - base
                lo = min(max(tl - w // 2, 0), length)
                hi = min(max(tl + w // 2, 0), length)
                masks[kind, gi, t, base + lo:base + hi] = 1.0
                cnts[kind, t, gi * POOL_GROUP_DIM:(gi + 1) * POOL_GROUP_DIM] = hi - lo
    return jnp.asarray(masks, BF16), jnp.asarray(cnts, F32)


def _mix_kernel(x_ref, zp_ref, zuv_ref, zmg_ref, zo_ref, hf_ref, hb_ref, mod_ref,
                pm_ref, cnt_ref, pw_ref, ps_ref, sg_ref, sb_ref, sw_ref, sbm_ref, ng_ref,
                wbp_ref, wbm_ref, wbs_ref, wo_ref, l1g_ref, l1b_ref, wr_ref, br_ref,
                x1_ref, h2_ref, route_ref, cnt_out_ref, cnt_sc, *, alpha):
    tm, d = x_ref.shape
    i = pl.program_id(0)

    @pl.when(i == 0)
    def _():
        cnt_sc[...] = jnp.zeros_like(cnt_sc)

    lane256 = lax.broadcasted_iota(jnp.int32, (1, 4 * POOL_GROUP_DIM), 1) // POOL_GROUP_DIM

    xp = zp_ref[...]
    xp_f = xp.astype(F32)
    pooled = jnp.zeros(xp_f.shape, F32)
    for gi in range(len(POOL_WINDOWS)):
        s = jnp.dot(pm_ref[gi], xp, preferred_element_type=F32)
        pooled = jnp.where(lane256 == gi, s, pooled)
    diff = pooled / cnt_ref[...] - xp_f
    pool_o = jnp.dot(diff.astype(BF16), pw_ref[...], preferred_element_type=F32) * ps_ref[...]

    uv = zuv_ref[...].astype(F32)
    uv = 0.5 * uv * (1.0 + lax.erf(uv * (2.0 ** -0.5)))
    half = uv.shape[1] // 2
    u = uv[:, :half]
    vn = (_ln_plain(uv[:, half:]) * sg_ref[...] + sb_ref[...]).astype(BF16)
    mixed_chunks = []
    for c0 in range(0, tm, CHUNK):
        vc = vn[c0:c0 + CHUNK, :]
        mixed = jnp.zeros((CHUNK, half), F32)
        for gi in range(sw_ref.shape[0]):
            s = jnp.dot(sw_ref[gi], vc, preferred_element_type=F32)
            mixed = jnp.where(lane256 == gi, s, mixed)
        mixed_chunks.append(mixed + sbm_ref[...])
    sgu_o = u * jnp.concatenate(mixed_chunks, axis=0)

    hsum = hf_ref[...].astype(F32) + hb_ref[...].astype(F32)
    heads = []
    for hh in range(N_HEADS):
        hv = hsum[:, hh * HEAD_DIM:(hh + 1) * HEAD_DIM]
        mu = jnp.mean(hv, axis=-1, keepdims=True)
        hc = hv - mu
        var = jnp.mean(hc * hc, axis=-1, keepdims=True)
        heads.append(hc * lax.rsqrt(var + HEAD_NORM_EPS))
    hn = jnp.concatenate(heads, axis=1)
    mlstm_o = hn * ng_ref[...] * _sigmoid(zo_ref[...].astype(F32))

    y = _sigmoid(zmg_ref[:, 0:d].astype(F32)) * jnp.dot(pool_o.astype(BF16), wbp_ref[...],
                                                         preferred_element_type=F32)
    y += _sigmoid(zmg_ref[:, d:2 * d].astype(F32)) * jnp.dot(mlstm_o.astype(BF16), wbm_ref[...],
                                                             preferred_element_type=F32)
    y += _sigmoid(zmg_ref[:, 2 * d:3 * d].astype(F32)) * jnp.dot(sgu_o.astype(BF16), wbs_ref[...],
                                                                 preferred_element_type=F32)
    y2 = jnp.dot(y.astype(BF16), wo_ref[...], preferred_element_type=F32)

    g1 = mod_ref[:, 2 * d:3 * d]
    x1 = _ln_plain(alpha * x_ref[...] + g1 * y2) * l1g_ref[...] + l1b_ref[...]
    x1_ref[...] = x1
    h2 = _ln_plain(x1) * (1.0 + mod_ref[:, 4 * d:5 * d]) + mod_ref[:, 3 * d:4 * d]
    h2b = h2.astype(BF16)
    h2_ref[...] = h2b

    logits = jnp.dot(h2b, wr_ref[...], preferred_element_type=F32) + br_ref[...]
    lane = lax.broadcasted_iota(jnp.int32, logits.shape, 1)
    work = logits
    vals, idxs, hots = [], [], []
    for _ in range(TOP_K):
        mval = jnp.max(work, axis=-1, keepdims=True)
        idx = jnp.min(jnp.where(work == mval, lane, LANES), axis=-1, keepdims=True)
        hot = lane == idx
        vals.append(mval)
        idxs.append(idx)
        hots.append(hot)
        work = jnp.where(hot, -jnp.inf, work)
    exps = [jnp.exp(v - vals[0]) for v in vals]
    denom = exps[0] + exps[1] + exps[2] + exps[3]
    sel = (hots[0] | hots[1] | hots[2] | hots[3])
    r_i = lax.broadcasted_iota(jnp.int32, (tm, tm), 0)
    c_i = lax.broadcasted_iota(jnp.int32, (tm, tm), 1)
    before = (c_i < r_i).astype(BF16)
    sel_b = sel.astype(F32).astype(BF16)
    rank = jnp.dot(before, sel_b, preferred_element_type=F32) + cnt_sc[0:1, :]
    route = jnp.zeros(logits.shape, F32)
    for kk in range(TOP_K):
        rk = jnp.sum(jnp.where(hots[kk], rank, 0.0), axis=-1, keepdims=True)
        route = jnp.where(lane == kk, idxs[kk].astype(F32), route)
        route = jnp.where(lane == TOP_K + kk, exps[kk] / denom, route)
        route = jnp.where(lane == 2 * TOP_K + kk, rk, route)
    route_ref[...] = route
    new_cnt = cnt_sc[0:1, :] + jnp.sum(sel.astype(F32), axis=0, keepdims=True)
    cnt_sc[...] = jnp.broadcast_to(new_cnt, cnt_sc.shape)
    cnt_out_ref[...] = jnp.broadcast_to(new_cnt, cnt_out_ref.shape)


def _mix(x, zp, zuv, zmg, zo, hf, hb, mods, consts, *, n_rows, n_lat, seq, alpha):
    t, d = x.shape
    tm = ROW_TILE
    n_batch = mods.shape[0] - 1
    row = lambda i: (i, 0)
    const2 = lambda i: (0, 0)
    const3 = lambda i: (0, 0, 0)
    kind = lambda i: jnp.where(i * tm >= n_lat, 1, 0)

    def mod_idx(i):
        return (jnp.minimum((i * tm) // seq, n_batch), 0, 0)

    (pm, cnt, pw, ps, sg, sb, sw, sbm, ng, wbp, wbm, wbs, wo, l1g, l1b, wr, br) = consts
    in_specs = [
        pl.BlockSpec((tm, d), row),
        pl.BlockSpec((tm, zp.shape[1]), row),
        pl.BlockSpec((tm, zuv.shape[1]), row),
        pl.BlockSpec((tm, zmg.shape[1]), row),
        pl.BlockSpec((tm, zo.shape[1]), row),
        pl.BlockSpec((tm, hf.shape[1]), row),
        pl.BlockSpec((tm, hb.shape[1]), row),
        pl.BlockSpec((None, 1, mods.shape[2]), mod_idx),
        pl.BlockSpec((None,) + pm.shape[1:], lambda i: (kind(i), 0, 0, 0)),
        pl.BlockSpec((None,) + cnt.shape[1:], lambda i: (kind(i), 0, 0)),
        pl.BlockSpec(pw.shape, const2),
        pl.BlockSpec(ps.shape, const2),
        pl.BlockSpec(sg.shape, const2),
        pl.BlockSpec(sb.shape, const2),
        pl.BlockSpec(sw.shape, const3),
        pl.BlockSpec(sbm.shape, const2),
        pl.BlockSpec(ng.shape, const2),
        pl.BlockSpec(wbp.shape, const2),
        pl.BlockSpec(wbm.shape, const2),
        pl.BlockSpec(wbs.shape, const2),
        pl.BlockSpec(wo.shape, const2),
        pl.BlockSpec(l1g.shape, const2),
        pl.BlockSpec(l1b.shape, const2),
        pl.BlockSpec(wr.shape, const2),
        pl.BlockSpec(br.shape, const2),
    ]
    return pl.pallas_call(
        functools.partial(_mix_kernel, alpha=alpha),
        grid=(n_rows // tm,),
        in_specs=in_specs,
        out_specs=[
            pl.BlockSpec((tm, d), row),
            pl.BlockSpec((tm, d), row),
            pl.BlockSpec((tm, LANES), row),
            pl.BlockSpec((8, LANES), const2),
        ],
        out_shape=[
            jax.ShapeDtypeStruct((n_rows, d), F32),
            jax.ShapeDtypeStruct((n_rows, d), BF16),
            jax.ShapeDtypeStruct((n_rows, LANES), F32),
            jax.ShapeDtypeStruct((8, LANES), F32),
        ],
        scratch_shapes=[pltpu.VMEM((8, LANES), F32)],
        compiler_params=_cparams("arbitrary"),
    )(x, zp, zuv, zmg, zo, hf, hb, mods, pm, cnt, pw, ps, sg, sb, sw, sbm, ng,
      wbp, wbm, wbs, wo, l1g, l1b, wr, br)


def _expert_kernel(te_ref, nt_ref, xs_ref, wgu_ref, bgu_ref, wd_ref, bd_ref, ys_ref):
    i = pl.program_id(0)
    f = wd_ref.shape[0]

    @pl.when(i < nt_ref[0])
    def _():
        x = xs_ref[...]
        acc = jnp.zeros(ys_ref.shape, F32)
        for c0 in range(0, f, 512):
            gate = jnp.dot(x, wgu_ref[:, c0:c0 + 512], preferred_element_type=F32) + bgu_ref[:, c0:c0 + 512]
            up = (jnp.dot(x, wgu_ref[:, f + c0:f + c0 + 512], preferred_element_type=F32)
                  + bgu_ref[:, f + c0:f + c0 + 512])
            gate = jnp.minimum(gate, SWIGLU_LIMIT)
            up = jnp.clip(up, -SWIGLU_LIMIT, SWIGLU_LIMIT)
            glu = gate * _sigmoid(SWIGLU_ALPHA * gate)
            act = ((up + 1.0) * glu).astype(BF16)
            acc += jnp.dot(act, wd_ref[c0:c0 + 512, :], preferred_element_type=F32)
        ys_ref[...] = (acc + bd_ref[...]).astype(ys_ref.dtype)

    @pl.when(i >= nt_ref[0])
    def _():
        ys_ref[...] = jnp.zeros_like(ys_ref)


def _experts(tile_expert, n_tiles_used, xs, wgu, bgu, wd, bd):
    p, d = xs.shape
    tm = EXPERT_TILE
    n_exp, _, f2 = wgu.shape
    grid_spec = pltpu.PrefetchScalarGridSpec(
        num_scalar_prefetch=2,
        grid=(p // tm,),
        in_specs=[
            pl.BlockSpec((tm, d), lambda i, te, nt: (i, 0)),
            pl.BlockSpec((None, d, f2), lambda i, te, nt: (te[i], 0, 0)),
            pl.BlockSpec((None, 1, f2), lambda i, te, nt: (te[i], 0, 0)),
            pl.BlockSpec((None, f2 // 2, d), lambda i, te, nt: (te[i], 0, 0)),
            pl.BlockSpec((None, 1, d), lambda i, te, nt: (te[i], 0, 0)),
        ],
        out_specs=pl.BlockSpec((tm, d), lambda i, te, nt: (i, 0)),
    )
    return pl.pallas_call(
        _expert_kernel,
        grid_spec=grid_spec,
        out_shape=jax.ShapeDtypeStruct((p, d), BF16),
        compiler_params=_cparams("arbitrary"),
    )(tile_expert, n_tiles_used, xs, wgu, bgu, wd, bd)


def _combine_kernel(x1_ref, yg_ref, route_ref, mod_ref, g_ref, b_ref, x2_ref, *, alpha):
    d = x1_ref.shape[1]
    route = route_ref[...]
    f = jnp.zeros(x1_ref.shape, F32)
    for kk in range(TOP_K):
        f += route[:, TOP_K + kk:TOP_K + kk + 1] * yg_ref[:, kk * d:(kk + 1) * d].astype(F32)
    g2 = mod_ref[:, 5 * d:6 * d]
    x2_ref[...] = _ln_plain(alpha * x1_ref[...] + g2 * f) * g_ref[...] + b_ref[...]


def _combine(x1, yg, route, mods, g, b, *, seq, alpha):
    t, d = x1.shape
    tm = ROW_TILE
    n_batch = mods.shape[0] - 1
    row = lambda i: (i, 0)

    def mod_idx(i):
        return (jnp.minimum((i * tm) // seq, n_batch), 0, 0)

    return pl.pallas_call(
        functools.partial(_combine_kernel, alpha=alpha),
        grid=(t // tm,),
        in_specs=[
            pl.BlockSpec((tm, d), row),
            pl.BlockSpec((tm, TOP_K * d), row),
            pl.BlockSpec((tm, LANES), row),
            pl.BlockSpec((None, 1, mods.shape[2]), mod_idx),
            pl.BlockSpec(g.shape, lambda i: (0, 0)),
            pl.BlockSpec(b.shape, lambda i: (0, 0)),
        ],
        out_specs=pl.BlockSpec((tm, d), row),
        out_shape=jax.ShapeDtypeStruct((t, d), F32),
        compiler_params=_cparams("parallel"),
    )(x1, yg, route, mods, g, b)


def kernel(x, c, ctx, c_ctx, w_ada, b_ada, w_in, b_in, pool_w, pool_scale, qk_conv_w, qk_conv_b,
           mlstm_norm_g, sgu_ln_g, sgu_ln_b, sgu_w, sgu_b, w_br_pool, w_br_mlstm, w_br_sgu, w_out,
           ln1_g, ln1_b, w_router, b_router, w_gate_up, b_gate_up, w_down, b_down, ln2_g, ln2_b):
    n_batch, seq, d = x.shape
    ctx_len = ctx.shape[1]
    depth = w_in.shape[0]
    n_exp = w_router.shape[2]
    n_lat = n_batch * seq
    t_all = n_lat + n_batch * ctx_len
    assert ctx_len == ROW_TILE and seq % ROW_TILE == 0 and ROW_TILE % GRID_W == 0
    alpha = float((2 * depth) ** 0.25)

    rows = ((n_batch + 1 + 7) // 8) * 8
    cc = jnp.zeros((rows, d), F32).at[:n_batch].set(c).at[n_batch].set(c_ctx)
    mods_all = _ada_table(cc, w_ada, b_ada)[:, :n_batch + 1].reshape(depth, n_batch + 1, 1, 6 * d)

    pool_w_, mlw = 256, 512
    col = lambda a, lo, hi: a[..., lo:hi]
    g_lo = pool_w_ + 4 * mlw
    parts = ((pool_w_, pool_w_ + 2 * mlw), (pool_w_ + 2 * mlw, pool_w_ + 3 * mlw), (pool_w_ + 3 * mlw, g_lo),
             (g_lo + _N_GATE_COLS, g_lo + _N_GATE_COLS + 512), (0, pool_w_),
             (g_lo + _N_GATE_COLS + 512, w_in.shape[2]))
    w_main = jnp.concatenate([col(w_in, lo, hi) for lo, hi in parts], axis=-1).astype(BF16)
    b_main = jnp.concatenate([col(b_in, lo, hi) for lo, hi in parts], axis=-1)[:, None, :]
    w_gate = jnp.pad(col(w_in, g_lo, g_lo + _N_GATE_COLS), ((0, 0), (0, 0), (0, LANES - _N_GATE_COLS))).astype(BF16)
    b_gate = jnp.pad(col(b_in, g_lo, g_lo + _N_GATE_COLS), ((0, 0), (0, LANES - _N_GATE_COLS)))[:, None, :]
    w_gate_t = jnp.swapaxes(col(w_in, g_lo, g_lo + _N_GATE_COLS), 1, 2).astype(BF16)
    b_gate_t = col(b_in, g_lo, g_lo + _N_GATE_COLS)[:, :, None]

    pm, cnt = _pool_constants(ctx_len)
    eye = jnp.eye(len(POOL_WINDOWS), dtype=F32)
    pool_bd = jnp.einsum('lgcd,gh->lgchd', pool_w, eye).reshape(depth, 256, 256).astype(BF16)
    sgu_bias_map = jnp.repeat(jnp.swapaxes(sgu_b, 1, 2), sgu_ln_g.shape[1] // sgu_b.shape[1], axis=2)
    w_router_p = jnp.pad(w_router, ((0, 0), (0, 0), (0, LANES - n_exp))).astype(BF16)
    b_router_p = jnp.pad(b_router, ((0, 0), (0, LANES - n_exp)), constant_values=NEG_BIG)[:, None, :]
    wgu_b = w_gate_up.astype(BF16)
    wd_b = w_down.astype(BF16)
    r1 = lambda a: a[:, None, :]

    stream = jnp.concatenate([x.reshape(n_lat, d), ctx.reshape(n_batch * ctx_len, d)], axis=0)
    for i in range(depth):
        last = i == depth - 1
        mods = mods_all[i]
        zqk, zv, zo, zuv, zp, zmg, g, gt = _inproj(
            stream, mods, w_main[i], b_main[i], w_gate[i], b_gate[i], w_gate_t[i], b_gate_t[i],
            n_lat=n_lat, seq=seq)
        hf, hb = _mlstm(zqk, zv, g, gt, qk_conv_w[i], r1(qk_conv_b)[i],
                        n_batch=n_batch, seq=seq, ctx_len=ctx_len)
        n_rows = n_lat if last else t_all
        consts = (pm, cnt, pool_bd[i], r1(pool_scale)[i], r1(sgu_ln_g)[i], r1(sgu_ln_b)[i],
                  sgu_w[i].astype(BF16), sgu_bias_map[i], r1(mlstm_norm_g)[i],
                  w_br_pool[i].astype(BF16), w_br_mlstm[i].astype(BF16), w_br_sgu[i].astype(BF16),
                  w_out[i].astype(BF16), r1(ln1_g)[i], r1(ln1_b)[i], w_router_p[i], b_router_p[i])
        x1, h2, route, counts = _mix(stream, zp, zuv, zmg, zo, hf, hb, mods, consts,
                                     n_rows=n_rows, n_lat=n_lat, seq=seq, alpha=alpha)

        cnt_e = counts[0, :n_exp].astype(jnp.int32)
        padded = ((cnt_e + EXPERT_TILE - 1) // EXPERT_TILE) * EXPERT_TILE
        ends = jnp.cumsum(padded)
        starts = ends - padded
        e_idx = route[:, 0:TOP_K].astype(jnp.int32)
        rank = route[:, 2 * TOP_K:3 * TOP_K].astype(jnp.int32)
        pos = starts[e_idx] + rank
        p_rows = n_rows * TOP_K + n_exp * EXPERT_TILE
        n_tiles = p_rows // EXPERT_TILE
        tile_expert = jnp.minimum(
            jnp.searchsorted(ends, jnp.arange(n_tiles, dtype=jnp.int32) * EXPERT_TILE, side='right'),
            n_exp - 1).astype(jnp.int32)
        n_used = (ends[-1] // EXPERT_TILE).astype(jnp.int32).reshape(1)
        tok = jnp.broadcast_to(jnp.arange(n_rows, dtype=jnp.int32)[:, None], pos.shape)
        tok_of_pos = jnp.zeros((p_rows,), jnp.int32).at[pos.reshape(-1)].set(tok.reshape(-1))
        xs = jnp.take(h2, tok_of_pos, axis=0)
        ys = _experts(tile_expert, n_used, xs, wgu_b[i], r1(b_gate_up[i]), wd_b[i], r1(b_down[i]))
        yg = jnp.take(ys, pos.reshape(-1), axis=0).reshape(n_rows, TOP_K * d)
        stream = _combine(x1, yg, route, mods, r1(ln2_g)[i], r1(ln2_b)[i], seq=seq, alpha=alpha)
    return stream[:n_lat].reshape(n_batch, seq, d)
```

```python
import functools

import numpy as np
import jax
import jax.numpy as jnp
from jax import lax
from jax.experimental import pallas as pl
from jax.experimental.pallas import tpu as pltpu

F32 = jnp.float32
BF16 = jnp.bfloat16

GRID_W = 64
POOL_WINDOWS = (2, 4, 8, 16)
POOL_GROUP_DIM = 64
N_HEADS = 4
HEAD_DIM = 128
CHUNK = 128
TOP_K = 4
SWIGLU_LIMIT = 7.0
SWIGLU_ALPHA = 1.702
LN_EPS = 1e-5
HEAD_NORM_EPS = 1e-6

ROW_TILE = 256
EXPERT_TILE = 512
SUBLANES = 8
LANES = 128
NEG_BIG = -1e30
VMEM_LIMIT = 56 * 1024 * 1024


def _cparams(*sem):
    return pltpu.CompilerParams(dimension_semantics=sem, vmem_limit_bytes=VMEM_LIMIT)


def _ln_plain(x):
    mu = jnp.mean(x, axis=-1, keepdims=True)
    xc = x - mu
    var = jnp.mean(xc * xc, axis=-1, keepdims=True)
    return xc * lax.rsqrt(var + LN_EPS)


def _sigmoid(x):
    return 1.0 / (1.0 + jnp.exp(-x))


def _log_sigmoid(x):
    return jnp.minimum(x, 0.0) - jnp.log1p(jnp.exp(-jnp.abs(x)))


def _ada_kernel(c_ref, w_ref, b_ref, o_ref):
    c = c_ref[...]
    s = c * _sigmoid(c)
    o_ref[...] = jnp.dot(s, w_ref[...], preferred_element_type=F32,
                         precision=lax.Precision.HIGHEST) + b_ref[...]


def _ada_table(cc, w_ada, b_ada):
    depth, d, n = w_ada.shape
    rows = cc.shape[0]
    bn = 1536
    return pl.pallas_call(
        _ada_kernel,
        grid=(depth, n // bn),
        in_specs=[
            pl.BlockSpec((rows, d), lambda l, j: (0, 0)),
            pl.BlockSpec((None, d, bn), lambda l, j: (l, 0, j)),
            pl.BlockSpec((None, 1, bn), lambda l, j: (l, 0, j)),
        ],
        out_specs=pl.BlockSpec((None, rows, bn), lambda l, j: (l, 0, j)),
        out_shape=jax.ShapeDtypeStruct((depth, rows, n), F32),
        compiler_params=_cparams("parallel", "parallel"),
    )(cc, w_ada, b_ada.reshape(depth, 1, n))


_QK_WIDTH = 2 * N_HEADS * HEAD_DIM
_Z_PARTS = (("v", 512), ("o", 512), ("uv", 512), ("pool", 256), ("mg", 3072))
_N_GATE_COLS = 4 * N_HEADS


def _inproj_kernel(x_ref, xp_ref, xn_ref, mod_ref, w_ref, b_ref, wg_ref, bg_ref, wgt_ref, bgt_ref,
                   cw_ref, cb_ref, qk_ref, v_ref, o_ref, uv_ref, p_ref, mg_ref, g_ref, gt_ref, pad_sc,
                   *, n_lat, seq):
    tm, d = x_ref.shape
    i = pl.program_id(0)
    is_lat = i * tm < n_lat
    has_prev = jnp.logical_and(is_lat, (i * tm) % seq != 0).astype(F32)
    has_next = jnp.logical_and(is_lat, ((i + 1) * tm) % seq != 0).astype(F32)

    xe = jnp.concatenate([xp_ref[...], x_ref[...], xn_ref[...]], axis=0)
    he = (_ln_plain(xe) * (1.0 + mod_ref[:, d:2 * d]) + mod_ref[:, 0:d]).astype(BF16)
    hb = he[SUBLANES:SUBLANES + tm, :]

    half = _QK_WIDTH // 2
    for c0 in range(0, _QK_WIDTH, half):
        z = jnp.dot(he, w_ref[:, c0:c0 + half], preferred_element_type=F32) + b_ref[:, c0:c0 + half]
        pad_sc[...] = z
        pad_sc[SUBLANES - 1:SUBLANES, :] = pad_sc[SUBLANES - 1:SUBLANES, :] * has_prev
        pad_sc[SUBLANES + tm:SUBLANES + tm + 1, :] = pad_sc[SUBLANES + tm:SUBLANES + tm + 1, :] * has_next
        y = (cb_ref[:, c0:c0 + half]
             + pad_sc[SUBLANES - 1:SUBLANES - 1 + tm, :] * cw_ref[0:1, c0:c0 + half]
             + pad_sc[SUBLANES:SUBLANES + tm, :] * cw_ref[1:2, c0:c0 + half]
             + pad_sc[SUBLANES + 1:SUBLANES + 1 + tm, :] * cw_ref[2:3, c0:c0 + half])
        act = y * _sigmoid(y)
        if c0 >= half:
            act = act * (HEAD_DIM ** -0.5)
        qk_ref[:, c0:c0 + half] = act.astype(qk_ref.dtype)

    off = _QK_WIDTH
    for ref, (_, width) in zip((v_ref, o_ref, uv_ref, p_ref, mg_ref), _Z_PARTS):
        for c0 in range(0, width, 512):
            cw = min(512, width - c0)
            z = jnp.dot(hb, w_ref[:, off + c0:off + c0 + cw], preferred_element_type=F32)
            ref[:, c0:c0 + cw] = (z + b_ref[:, off + c0:off + c0 + cw]).astype(ref.dtype)
        off += width
    g_ref[...] = jnp.dot(hb, wg_ref[...], preferred_element_type=F32) + bg_ref[...]
    gt_ref[...] = lax.dot_general(wgt_ref[...], hb, (((1,), (1,)), ((), ())),
                                  preferred_element_type=F32) + bgt_ref[...]


def _inproj(x, mods, w, b, wg, bg, wgt, bgt, conv_w, conv_b, *, n_lat, seq):
    t, d = x.shape
    tm = ROW_TILE
    n_batch = mods.shape[0] - 1
    per_tile = tm // SUBLANES

    def mod_idx(i):
        return (jnp.minimum((i * tm) // seq, n_batch), 0, 0)

    const = lambda i: (0, 0)
    row = lambda i: (i, 0)
    parts = (("qk", _QK_WIDTH),) + _Z_PARTS
    out_shapes = [jax.ShapeDtypeStruct((t, width), BF16) for _, width in parts]
    out_specs = [pl.BlockSpec((tm, width), row) for _, width in parts]
    out_shapes += [jax.ShapeDtypeStruct((t, LANES), F32), jax.ShapeDtypeStruct((_N_GATE_COLS, t), F32)]
    out_specs += [pl.BlockSpec((tm, LANES), row), pl.BlockSpec((_N_GATE_COLS, tm), lambda i: (0, i))]
    return pl.pallas_call(
        functools.partial(_inproj_kernel, n_lat=n_lat, seq=seq),
        grid=(t // tm,),
        in_specs=[
            pl.BlockSpec((tm, d), row),
            pl.BlockSpec((SUBLANES, d), lambda i: (jnp.maximum(i * per_tile - 1, 0), 0)),
            pl.BlockSpec((SUBLANES, d), lambda i: (jnp.minimum((i + 1) * per_tile, t // SUBLANES - 1), 0)),
            pl.BlockSpec((None, 1, mods.shape[2]), mod_idx),
            pl.BlockSpec(w.shape, const),
            pl.BlockSpec(b.shape, const),
            pl.BlockSpec(wg.shape, const),
            pl.BlockSpec(bg.shape, const),
            pl.BlockSpec(wgt.shape, const),
            pl.BlockSpec(bgt.shape, const),
            pl.BlockSpec(conv_w.shape, const),
            pl.BlockSpec(conv_b.shape, const),
        ],
        out_specs=out_specs,
        out_shape=out_shapes,
        scratch_shapes=[pltpu.VMEM((tm + 2 * SUBLANES, _QK_WIDTH // 2), F32)],
        compiler_params=_cparams("parallel"),
    )(x, x, x, mods, w, b, wg, bg, wgt, bgt, conv_w, conv_b)


def _mlstm_row_block(j, b, direction, *, n_batch, seq, ctx_len):
    ncc = ctx_len // CHUNK
    ncl = seq // CHUNK
    is_ctx = j < ncc
    if direction == 0:
        c = jnp.where(is_ctx, j, j - ncc)
    else:
        c = jnp.where(is_ctx, ncc - 1 - j, ncl - 1 - (j - ncc))
    base = jnp.where(is_ctx, (n_batch * seq + b * ctx_len) // CHUNK, (b * seq) // CHUNK)
    return base + c


def _mlstm_kernel(q_f, k_f, v_f, g_f, gt_f, q_b, k_b, v_b, g_b, gt_b, hf_ref, hb_ref, s_sc, m_sc):
    j = pl.program_id(1)

    @pl.when(j == 0)
    def _():
        s_sc[...] = jnp.zeros_like(s_sc)
        m_sc[...] = jnp.zeros_like(m_sc)

    ri = lax.broadcasted_iota(jnp.int32, (CHUNK, CHUNK), 0)
    ci = lax.broadcasted_iota(jnp.int32, (CHUNK, CHUNK), 1)
    ones_tile = jnp.ones((CHUNK, HEAD_DIM), BF16)
    dirs = ((q_f, k_f, v_f, g_f, gt_f, hf_ref), (q_b, k_b, v_b, g_b, gt_b, hb_ref))
    combos = [(d, hh) for d in range(2) for hh in range(N_HEADS)]
    heads = lambda ref, hh: ref[:, hh * HEAD_DIM:(hh + 1) * HEAD_DIM]

    tri, g, gt, cum, cum_t = [], [], [], [], []
    for d, (_, _, _, g_ref, gt_ref, _) in enumerate(dirs):
        tri.append((ci <= ri) if d == 0 else (ci >= ri))
        tri_t = (ri <= ci) if d == 0 else (ri >= ci)
        g.append(g_ref[...])
        gt.append(gt_ref[...])
        cum.append(jnp.dot(tri[d].astype(F32), _log_sigmoid(g[d]), preferred_element_type=F32,
                           precision=lax.Precision.HIGHEST))
        cum_t.append(jnp.dot(_log_sigmoid(gt[d]), tri_t.astype(F32), preferred_element_type=F32,
                             precision=lax.Precision.HIGHEST))

    qk_s, v1, state, m_prev = {}, {}, {}, {}
    for d, hh in combos:
        q_ref, k_ref, v_ref = dirs[d][:3]
        qk_s[d, hh] = lax.dot_general(heads(q_ref, hh), heads(k_ref, hh), (((1,), (1,)), ((), ())),
                                      preferred_element_type=F32)
        v1[d, hh] = jnp.concatenate([heads(v_ref, hh), ones_tile], axis=1)
        state[d, hh] = s_sc[d, hh]
        m_prev[d, hh] = m_sc[d * N_HEADS + hh][0:1, 0:1]

    q_state = {}
    for d, hh in combos:
        q_state[d, hh] = jnp.dot(heads(dirs[d][0], hh), state[d, hh].astype(BF16), preferred_element_type=F32)

    bcum_col, dmat, m_intra = {}, {}, {}
    for d, hh in combos:
        col_i = 2 * d * N_HEADS + hh
        col_f = col_i + N_HEADS
        bcum_col[d, hh] = cum[d][:, col_f:col_f + 1]
        dmat[d, hh] = jnp.where(tri[d], bcum_col[d, hh] - cum_t[d][col_f:col_f + 1, :]
                                + gt[d][col_i:col_i + 1, :], -jnp.inf)
        m_intra[d, hh] = jnp.max(dmat[d, hh], axis=-1, keepdims=True)

    m_t, decay, scores = {}, {}, {}
    for d, hh in combos:
        inter = bcum_col[d, hh] + m_prev[d, hh]
        m_t[d, hh] = jnp.maximum(inter, m_intra[d, hh])
        decay[d, hh] = jnp.exp(inter - m_t[d, hh])
        scores[d, hh] = (qk_s[d, hh] * jnp.exp(dmat[d, hh] - m_t[d, hh])).astype(BF16)

    for d, hh in combos:
        numden = decay[d, hh] * q_state[d, hh] + jnp.dot(scores[d, hh], v1[d, hh], preferred_element_type=F32)
        num = numden[:, :HEAD_DIM]
        den = numden[:, HEAD_DIM:]
        out = num / jnp.maximum(jnp.abs(den), jnp.exp(-m_t[d, hh]))
        out_ref = dirs[d][5]
        out_ref[:, hh * HEAD_DIM:(hh + 1) * HEAD_DIM] = out.astype(out_ref.dtype)

    kw, a, m_new = {}, {}, {}
    for d, hh in combos:
        col_i = 2 * d * N_HEADS + hh
        last = CHUNK - 1 if d == 0 else 0
        total = bcum_col[d, hh][last:last + 1, :]
        wlog = total - bcum_col[d, hh] + g[d][:, col_i:col_i + 1]
        m_new[d, hh] = jnp.maximum(total + m_prev[d, hh], jnp.max(wlog, axis=0, keepdims=True))
        a[d, hh] = jnp.exp(total + m_prev[d, hh] - m_new[d, hh])
        kw[d, hh] = (heads(dirs[d][1], hh).astype(F32) * jnp.exp(wlog - m_new[d, hh])).astype(BF16)

    for d, hh in combos:
        upd = lax.dot_general(kw[d, hh], v1[d, hh], (((0,), (0,)), ((), ())), preferred_element_type=F32)
        s_sc[d, hh] = a[d, hh] * state[d, hh] + upd
        m_sc[d * N_HEADS + hh] = jnp.broadcast_to(m_new[d, hh], m_sc.shape[1:])


def _mlstm(zqk, zv, g, gt, *, n_batch, seq, ctx_len):
    t = zqk.shape[0]
    nsteps = (seq + ctx_len) // CHUNK
    width = N_HEADS * HEAD_DIM
    idx = functools.partial(_mlstm_row_block, n_batch=n_batch, seq=seq, ctx_len=ctx_len)

    def specs(d):
        rb = lambda b, j: idx(j, b, d)
        return [
            pl.BlockSpec((CHUNK, width), lambda b, j: (rb(b, j), 0)),
            pl.BlockSpec((CHUNK, width), lambda b, j: (rb(b, j), 1)),
            pl.BlockSpec((CHUNK, width), lambda b, j: (rb(b, j), 0)),
            pl.BlockSpec((CHUNK, LANES), lambda b, j: (rb(b, j), 0)),
            pl.BlockSpec((_N_GATE_COLS, CHUNK), lambda b, j: (0, rb(b, j))),
        ]

    out_spec = lambda d: pl.BlockSpec((CHUNK, width), lambda b, j: (idx(j, b, d), 0))
    return pl.pallas_call(
        _mlstm_kernel,
        grid=(n_batch, nsteps),
        in_specs=specs(0) + specs(1),
        out_specs=[out_spec(0), out_spec(1)],
        out_shape=[jax.ShapeDtypeStruct((t, width), BF16)] * 2,
        scratch_shapes=[
            pltpu.VMEM((2, N_HEADS, HEAD_DIM, 2 * HEAD_DIM), F32),
            pltpu.VMEM((2 * N_HEADS, 8, LANES), F32),
        ],
        compiler_params=_cparams("parallel", "arbitrary"),
    )(zqk, zqk, zv, g, gt, zqk, zqk, zv, g, gt)


def _pool_constants(ctx_len):
    tm = ROW_TILE
    masks = np.zeros((2, len(POOL_WINDOWS), tm, tm), np.float32)
    cnts = np.zeros((2, tm, len(POOL_WINDOWS) * POOL_GROUP_DIM), np.float32)
    for kind, length in enumerate((GRID_W, ctx_len)):
        for gi, w in enumerate(POOL_WINDOWS):
            for t in range(tm):
                base = (t // length) * length
                tl = t - base
                lo = min(max(tl - w // 2, 0), length)
                hi = min(max(tl + w // 2, 0), length)
                masks[kind, gi, t, base + lo:base + hi] = 1.0
                cnts[kind, t, gi * POOL_GROUP_DIM:(gi + 1) * POOL_GROUP_DIM] = hi - lo
    return jnp.asarray(masks, BF16), jnp.asarray(cnts, F32)


def _mix_kernel(x_ref, zp_ref, zuv_ref, zmg_ref, zo_ref, hf_ref, hb_ref, mod_ref,
                pm_ref, cnt_ref, pw_ref, ps_ref, sg_ref, sb_ref, sw_ref, sbm_ref, ng_ref,
                wbp_ref, wbm_ref, wbs_ref, wo_ref, l1g_ref, l1b_ref, wr_ref, br_ref,
                x1_ref, h2_ref, route_ref, cnt_out_ref, cnt_sc, *, alpha):
    tm, d = x_ref.shape
    i = pl.program_id(0)

    @pl.when(i == 0)
    def _():
        cnt_sc[...] = jnp.zeros_like(cnt_sc)

    lane256 = lax.broadcasted_iota(jnp.int32, (1, 4 * POOL_GROUP_DIM), 1) // POOL_GROUP_DIM

    xp = zp_ref[...]
    xp_f = xp.astype(F32)
    pooled = jnp.zeros(xp_f.shape, F32)
    for gi in range(len(POOL_WINDOWS)):
        s = jnp.dot(pm_ref[gi], xp, preferred_element_type=F32)
        pooled = jnp.where(lane256 == gi, s, pooled)
    diff = pooled / cnt_ref[...] - xp_f
    pool_o = jnp.dot(diff.astype(BF16), pw_ref[...], preferred_element_type=F32) * ps_ref[...]

    uv = zuv_ref[...].astype(F32)
    uv = 0.5 * uv * (1.0 + lax.erf(uv * (2.0 ** -0.5)))
    half = uv.shape[1] // 2
    u = uv[:, :half]
    vn = (_ln_plain(uv[:, half:]) * sg_ref[...] + sb_ref[...]).astype(BF16)
    mixed_chunks = []
    for c0 in range(0, tm, CHUNK):
        vc = vn[c0:c0 + CHUNK, :]
        mixed = jnp.zeros((CHUNK, half), F32)
        for gi in range(sw_ref.shape[0]):
            s = jnp.dot(sw_ref[gi], vc, preferred_element_type=F32)
            mixed = jnp.where(lane256 == gi, s, mixed)
        mixed_chunks.append(mixed + sbm_ref[...])
    sgu_o = u * jnp.concatenate(mixed_chunks, axis=0)

    hsum = hf_ref[...].astype(F32) + hb_ref[...].astype(F32)
    heads = []
    for hh in range(N_HEADS):
        hv = hsum[:, hh * HEAD_DIM:(hh + 1) * HEAD_DIM]
        mu = jnp.mean(hv, axis=-1, keepdims=True)
        hc = hv - mu
        var = jnp.mean(hc * hc, axis=-1, keepdims=True)
        heads.append(hc * lax.rsqrt(var + HEAD_NORM_EPS))
    hn = jnp.concatenate(heads, axis=1)
    mlstm_o = hn * ng_ref[...] * _sigmoid(zo_ref[...].astype(F32))

    y = _sigmoid(zmg_ref[:, 0:d].astype(F32)) * jnp.dot(pool_o.astype(BF16), wbp_ref[...],
                                                         preferred_element_type=F32)
    y += _sigmoid(zmg_ref[:, d:2 * d].astype(F32)) * jnp.dot(mlstm_o.astype(BF16), wbm_ref[...],
                                                             preferred_element_type=F32)
    y += _sigmoid(zmg_ref[:, 2 * d:3 * d].astype(F32)) * jnp.dot(sgu_o.astype(BF16), wbs_ref[...],
                                                                 preferred_element_type=F32)
    y2 = jnp.dot(y.astype(BF16), wo_ref[...], preferred_element_type=F32)

    g1 = mod_ref[:, 2 * d:3 * d]
    x1 = _ln_plain(alpha * x_ref[...] + g1 * y2) * l1g_ref[...] + l1b_ref[...]
    x1_ref[...] = x1
    h2 = _ln_plain(x1) * (1.0 + mod_ref[:, 4 * d:5 * d]) + mod_ref[:, 3 * d:4 * d]
    h2b = h2.astype(BF16)
    h2_ref[...] = h2b

    logits = jnp.dot(h2b, wr_ref[...], preferred_element_type=F32) + br_ref[...]
    lane = lax.broadcasted_iota(jnp.int32, logits.shape, 1)
    work = logits
    vals, idxs, hots = [], [], []
    for _ in range(TOP_K):
        mval = jnp.max(work, axis=-1, keepdims=True)
        idx = jnp.min(jnp.where(work == mval, lane, LANES), axis=-1, keepdims=True)
        hot = lane == idx
        vals.append(mval)
        idxs.append(idx)
        hots.append(hot)
        work = jnp.where(hot, -jnp.inf, work)
    exps = [jnp.exp(v - vals[0]) for v in vals]
    denom = exps[0] + exps[1] + exps[2] + exps[3]
    sel = (hots[0] | hots[1] | hots[2] | hots[3])
    r_i = lax.broadcasted_iota(jnp.int32, (tm, tm), 0)
    c_i = lax.broadcasted_iota(jnp.int32, (tm, tm), 1)
    before = (c_i < r_i).astype(BF16)
    sel_b = sel.astype(F32).astype(BF16)
    rank = jnp.dot(before, sel_b, preferred_element_type=F32) + cnt_sc[0:1, :]
    route = jnp.zeros(logits.shape, F32)
    for kk in range(TOP_K):
        rk = jnp.sum(jnp.where(hots[kk], rank, 0.0), axis=-1, keepdims=True)
        route = jnp.where(lane == kk, idxs[kk].astype(F32), route)
        route = jnp.where(lane == TOP_K + kk, exps[kk] / denom, route)
        route = jnp.where(lane == 2 * TOP_K + kk, rk, route)
    route_ref[...] = route
    new_cnt = cnt_sc[0:1, :] + jnp.sum(sel.astype(F32), axis=0, keepdims=True)
    cnt_sc[...] = jnp.broadcast_to(new_cnt, cnt_sc.shape)
    cnt_out_ref[...] = jnp.broadcast_to(new_cnt, cnt_out_ref.shape)


def _mix(x, zp, zuv, zmg, zo, hf, hb, mods, consts, *, n_rows, n_lat, seq, alpha):
    t, d = x.shape
    tm = ROW_TILE
    n_batch = mods.shape[0] - 1
    row = lambda i: (i, 0)
    const2 = lambda i: (0, 0)
    const3 = lambda i: (0, 0, 0)
    kind = lambda i: jnp.where(i * tm >= n_lat, 1, 0)

    def mod_idx(i):
        return (jnp.minimum((i * tm) // seq, n_batch), 0, 0)

    (pm, cnt, pw, ps, sg, sb, sw, sbm, ng, wbp, wbm, wbs, wo, l1g, l1b, wr, br) = consts
    in_specs = [
        pl.BlockSpec((tm, d), row),
        pl.BlockSpec((tm, zp.shape[1]), row),
        pl.BlockSpec((tm, zuv.shape[1]), row),
        pl.BlockSpec((tm, zmg.shape[1]), row),
        pl.BlockSpec((tm, zo.shape[1]), row),
        pl.BlockSpec((tm, hf.shape[1]), row),
        pl.BlockSpec((tm, hb.shape[1]), row),
        pl.BlockSpec((None, 1, mods.shape[2]), mod_idx),
        pl.BlockSpec((None,) + pm.shape[1:], lambda i: (kind(i), 0, 0, 0)),
        pl.BlockSpec((None,) + cnt.shape[1:], lambda i: (kind(i), 0, 0)),
        pl.BlockSpec(pw.shape, const2),
        pl.BlockSpec(ps.shape, const2),
        pl.BlockSpec(sg.shape, const2),
        pl.BlockSpec(sb.shape, const2),
        pl.BlockSpec(sw.shape, const3),
        pl.BlockSpec(sbm.shape, const2),
        pl.BlockSpec(ng.shape, const2),
        pl.BlockSpec(wbp.shape, const2),
        pl.BlockSpec(wbm.shape, const2),
        pl.BlockSpec(wbs.shape, const2),
        pl.BlockSpec(wo.shape, const2),
        pl.BlockSpec(l1g.shape, const2),
        pl.BlockSpec(l1b.shape, const2),
        pl.BlockSpec(wr.shape, const2),
        pl.BlockSpec(br.shape, const2),
    ]
    return pl.pallas_call(
        functools.partial(_mix_kernel, alpha=alpha),
        grid=(n_rows // tm,),
        in_specs=in_specs,
        out_specs=[
            pl.BlockSpec((tm, d), row),
            pl.BlockSpec((tm, d), row),
            pl.BlockSpec((tm, LANES), row),
            pl.BlockSpec((8, LANES), const2),
        ],
        out_shape=[
            jax.ShapeDtypeStruct((n_rows, d), F32),
            jax.ShapeDtypeStruct((n_rows, d), BF16),
            jax.ShapeDtypeStruct((n_rows, LANES), F32),
            jax.ShapeDtypeStruct((8, LANES), F32),
        ],
        scratch_shapes=[pltpu.VMEM((8, LANES), F32)],
        compiler_params=_cparams("arbitrary"),
    )(x, zp, zuv, zmg, zo, hf, hb, mods, pm, cnt, pw, ps, sg, sb, sw, sbm, ng,
      wbp, wbm, wbs, wo, l1g, l1b, wr, br)


def _expert_kernel(te_ref, nt_ref, xs_ref, wgu_ref, bgu_ref, wd_ref, bd_ref, ys_ref, wgu_sc, wd_sc):
    i = pl.program_id(0)
    f = wd_ref.shape[0]
    used = i < nt_ref[0]

    @pl.when(jnp.logical_and(used, jnp.logical_or(i == 0, te_ref[i] != te_ref[jnp.maximum(i - 1, 0)])))
    def _():
        for c0 in range(0, 2 * f, 512):
            wgu_sc[:, c0:c0 + 512] = wgu_ref[:, c0:c0 + 512].astype(BF16)
        for c0 in range(0, f, 256):
            wd_sc[c0:c0 + 256, :] = wd_ref[c0:c0 + 256, :].astype(BF16)

    @pl.when(used)
    def _():
        x = xs_ref[...]
        acc = jnp.zeros(ys_ref.shape, F32)
        for c0 in range(0, f, 512):
            gate = jnp.dot(x, wgu_sc[:, c0:c0 + 512], preferred_element_type=F32) + bgu_ref[:, c0:c0 + 512]
            up = (jnp.dot(x, wgu_sc[:, f + c0:f + c0 + 512], preferred_element_type=F32)
                  + bgu_ref[:, f + c0:f + c0 + 512])
            gate = jnp.minimum(gate, SWIGLU_LIMIT)
            up = jnp.clip(up, -SWIGLU_LIMIT, SWIGLU_LIMIT)
            glu = gate * _sigmoid(SWIGLU_ALPHA * gate)
            act = ((up + 1.0) * glu).astype(BF16)
            acc += jnp.dot(act, wd_sc[c0:c0 + 512, :], preferred_element_type=F32)
        ys_ref[...] = (acc + bd_ref[...]).astype(ys_ref.dtype)

    @pl.when(jnp.logical_not(used))
    def _():
        ys_ref[...] = jnp.zeros_like(ys_ref)


def _experts(tile_expert, n_tiles_used, xs, wgu, bgu, wd, bd):
    p, d = xs.shape
    tm = EXPERT_TILE
    n_exp, _, f2 = wgu.shape
    grid_spec = pltpu.PrefetchScalarGridSpec(
        num_scalar_prefetch=2,
        grid=(p // tm,),
        in_specs=[
            pl.BlockSpec((tm, d), lambda i, te, nt: (i, 0)),
            pl.BlockSpec((None, d, f2), lambda i, te, nt: (te[i], 0, 0)),
            pl.BlockSpec((None, 1, f2), lambda i, te, nt: (te[i], 0, 0)),
            pl.BlockSpec((None, f2 // 2, d), lambda i, te, nt: (te[i], 0, 0)),
            pl.BlockSpec((None, 1, d), lambda i, te, nt: (te[i], 0, 0)),
        ],
        out_specs=pl.BlockSpec((tm, d), lambda i, te, nt: (i, 0)),
        scratch_shapes=[pltpu.VMEM((d, f2), BF16), pltpu.VMEM((f2 // 2, d), BF16)],
    )
    return pl.pallas_call(
        _expert_kernel,
        grid_spec=grid_spec,
        out_shape=jax.ShapeDtypeStruct((p, d), BF16),
        compiler_params=_cparams("arbitrary"),
    )(tile_expert, n_tiles_used, xs, wgu, bgu, wd, bd)


def _combine_kernel(x1_ref, yg_ref, route_ref, mod_ref, g_ref, b_ref, x2_ref, *, alpha):
    d = x1_ref.shape[1]
    route = route_ref[...]
    f = jnp.zeros(x1_ref.shape, F32)
    for kk in range(TOP_K):
        f += route[:, TOP_K + kk:TOP_K + kk + 1] * yg_ref[kk].astype(F32)
    g2 = mod_ref[:, 5 * d:6 * d]
    x2_ref[...] = _ln_plain(alpha * x1_ref[...] + g2 * f) * g_ref[...] + b_ref[...]


def _combine(x1, yg, route, mods, g, b, *, seq, alpha):
    t, d = x1.shape
    tm = ROW_TILE
    n_batch = mods.shape[0] - 1
    row = lambda i: (i, 0)

    def mod_idx(i):
        return (jnp.minimum((i * tm) // seq, n_batch), 0, 0)

    return pl.pallas_call(
        functools.partial(_combine_kernel, alpha=alpha),
        grid=(t // tm,),
        in_specs=[
            pl.BlockSpec((tm, d), row),
            pl.BlockSpec((TOP_K, tm, d), lambda i: (0, i, 0)),
            pl.BlockSpec((tm, LANES), row),
            pl.BlockSpec((None, 1, mods.shape[2]), mod_idx),
            pl.BlockSpec(g.shape, lambda i: (0, 0)),
            pl.BlockSpec(b.shape, lambda i: (0, 0)),
        ],
        out_specs=pl.BlockSpec((tm, d), row),
        out_shape=jax.ShapeDtypeStruct((t, d), F32),
        compiler_params=_cparams("parallel"),
    )(x1, yg, route, mods, g, b)


def kernel(x, c, ctx, c_ctx, w_ada, b_ada, w_in, b_in, pool_w, pool_scale, qk_conv_w, qk_conv_b,
           mlstm_norm_g, sgu_ln_g, sgu_ln_b, sgu_w, sgu_b, w_br_pool, w_br_mlstm, w_br_sgu, w_out,
           ln1_g, ln1_b, w_router, b_router, w_gate_up, b_gate_up, w_down, b_down, ln2_g, ln2_b):
    n_batch, seq, d = x.shape
    ctx_len = ctx.shape[1]
    depth = w_in.shape[0]
    n_exp = w_router.shape[2]
    n_lat = n_batch * seq
    t_all = n_lat + n_batch * ctx_len
    assert ctx_len == ROW_TILE and seq % ROW_TILE == 0 and ROW_TILE % GRID_W == 0
    alpha = float((2 * depth) ** 0.25)

    rows = ((n_batch + 1 + 7) // 8) * 8
    cc = jnp.zeros((rows, d), F32).at[:n_batch].set(c).at[n_batch].set(c_ctx)
    mods_all = _ada_table(cc, w_ada, b_ada)[:, :n_batch + 1].reshape(depth, n_batch + 1, 1, 6 * d)

    pool_w_, mlw = 256, 512
    col = lambda a, lo, hi: a[..., lo:hi]
    g_lo = pool_w_ + 4 * mlw
    parts = ((pool_w_, pool_w_ + 2 * mlw), (pool_w_ + 2 * mlw, pool_w_ + 3 * mlw), (pool_w_ + 3 * mlw, g_lo),
             (g_lo + _N_GATE_COLS, g_lo + _N_GATE_COLS + 512), (0, pool_w_),
             (g_lo + _N_GATE_COLS + 512, w_in.shape[2]))
    w_main = jnp.concatenate([col(w_in, lo, hi) for lo, hi in parts], axis=-1).astype(BF16)
    b_main = jnp.concatenate([col(b_in, lo, hi) for lo, hi in parts], axis=-1)[:, None, :]
    w_gate = jnp.pad(col(w_in, g_lo, g_lo + _N_GATE_COLS), ((0, 0), (0, 0), (0, LANES - _N_GATE_COLS))).astype(BF16)
    b_gate = jnp.pad(col(b_in, g_lo, g_lo + _N_GATE_COLS), ((0, 0), (0, LANES - _N_GATE_COLS)))[:, None, :]
    w_gate_t = jnp.swapaxes(col(w_in, g_lo, g_lo + _N_GATE_COLS), 1, 2).astype(BF16)
    b_gate_t = col(b_in, g_lo, g_lo + _N_GATE_COLS)[:, :, None]

    pm, cnt = _pool_constants(ctx_len)
    eye = jnp.eye(len(POOL_WINDOWS), dtype=F32)
    pool_bd = jnp.einsum('lgcd,gh->lgchd', pool_w, eye).reshape(depth, 256, 256).astype(BF16)
    sgu_bias_map = jnp.repeat(jnp.swapaxes(sgu_b, 1, 2), sgu_ln_g.shape[1] // sgu_b.shape[1], axis=2)
    w_router_p = jnp.pad(w_router, ((0, 0), (0, 0), (0, LANES - n_exp))).astype(BF16)
    b_router_p = jnp.pad(b_router, ((0, 0), (0, LANES - n_exp)), constant_values=NEG_BIG)[:, None, :]
    r1 = lambda a: a[:, None, :]

    stream = jnp.concatenate([x.reshape(n_lat, d), ctx.reshape(n_batch * ctx_len, d)], axis=0)
    for i in range(depth):
        last = i == depth - 1
        mods = mods_all[i]
        zqk, zv, zo, zuv, zp, zmg, g, gt = _inproj(
            stream, mods, w_main[i], b_main[i], w_gate[i], b_gate[i], w_gate_t[i], b_gate_t[i],
            qk_conv_w[i], r1(qk_conv_b)[i], n_lat=n_lat, seq=seq)
        hf, hb = _mlstm(zqk, zv, g, gt, n_batch=n_batch, seq=seq, ctx_len=ctx_len)
        n_rows = n_lat if last else t_all
        consts = (pm, cnt, pool_bd[i], r1(pool_scale)[i], r1(sgu_ln_g)[i], r1(sgu_ln_b)[i],
                  sgu_w[i].astype(BF16), sgu_bias_map[i], r1(mlstm_norm_g)[i],
                  w_br_pool[i].astype(BF16), w_br_mlstm[i].astype(BF16), w_br_sgu[i].astype(BF16),
                  w_out[i].astype(BF16), r1(ln1_g)[i], r1(ln1_b)[i], w_router_p[i], b_router_p[i])
        x1, h2, route, counts = _mix(stream, zp, zuv, zmg, zo, hf, hb, mods, consts,
                                     n_rows=n_rows, n_lat=n_lat, seq=seq, alpha=alpha)

        cnt_e = counts[0, :n_exp].astype(jnp.int32)
        padded = ((cnt_e + EXPERT_TILE - 1) // EXPERT_TILE) * EXPERT_TILE
        ends = jnp.cumsum(padded)
        starts = ends - padded
        e_idx = route[:, 0:TOP_K].astype(jnp.int32)
        rank = route[:, 2 * TOP_K:3 * TOP_K].astype(jnp.int32)
        pos_t = (starts[e_idx] + rank).T
        p_rows = n_rows * TOP_K + n_exp * EXPERT_TILE
        n_tiles = p_rows // EXPERT_TILE
        tile_start = jnp.arange(n_tiles, dtype=jnp.int32) * EXPERT_TILE
        tile_expert = jnp.minimum(jnp.sum((ends[None, :] <= tile_start[:, None]).astype(jnp.int32), axis=1),
                                  n_exp - 1)
        n_used = (ends[-1] // EXPERT_TILE).astype(jnp.int32).reshape(1)
        tok = jnp.broadcast_to(jnp.arange(n_rows, dtype=jnp.int32)[None, :], pos_t.shape)
        tok_of_pos = jnp.zeros((p_rows,), jnp.int32).at[pos_t.reshape(-1)].set(
            tok.reshape(-1), unique_indices=True, mode="promise_in_bounds")
        xs = h2.at[tok_of_pos].get(mode="promise_in_bounds")
        ys = _experts(tile_expert, n_used, xs, w_gate_up[i], r1(b_gate_up[i]), w_down[i], r1(b_down[i]))
        yg = ys.at[pos_t.reshape(-1)].get(mode="promise_in_bounds", unique_indices=True)
        yg = yg.reshape(TOP_K, n_rows, d)
        stream = _combine(x1, yg, route, mods, r1(ln2_g)[i], r1(ln2_b)[i], seq=seq, alpha=alpha)
    return stream[:n_lat].reshape(n_batch, seq, d)
```

```python
import functools

import numpy as np
import jax
import jax.numpy as jnp
from jax import lax
from jax.experimental import pallas as pl
from jax.experimental.pallas import tpu as pltpu

F32 = jnp.float32
BF16 = jnp.bfloat16

GRID_W = 64
POOL_WINDOWS = (2, 4, 8, 16)
POOL_GROUP_DIM = 64
N_HEADS = 4
HEAD_DIM = 128
CHUNK = 128
TOP_K = 4
SWIGLU_LIMIT = 7.0
SWIGLU_ALPHA = 1.702
LN_EPS = 1e-5
HEAD_NORM_EPS = 1e-6

ROW_TILE = 256
EXPERT_TILE = 512
SUBLANES = 8
LANES = 128
NEG_BIG = -1e30
VMEM_LIMIT = 56 * 1024 * 1024


def _cparams(*sem):
    return pltpu.CompilerParams(dimension_semantics=sem, vmem_limit_bytes=VMEM_LIMIT)


def _ln_plain(x):
    mu = jnp.mean(x, axis=-1, keepdims=True)
    xc = x - mu
    var = jnp.mean(xc * xc, axis=-1, keepdims=True)
    return xc * lax.rsqrt(var + LN_EPS)


def _sigmoid(x):
    return 0.5 * jnp.tanh(0.5 * x) + 0.5


def _log_sigmoid(x):
    return jnp.minimum(x, 0.0) - jnp.log1p(jnp.exp(-jnp.abs(x)))


def _ada_kernel(c_ref, w_ref, b_ref, o_ref):
    c = c_ref[...]
    s = c * _sigmoid(c)
    o_ref[...] = jnp.dot(s, w_ref[...], preferred_element_type=F32,
                         precision=lax.Precision.HIGHEST) + b_ref[...]


def _ada_table(cc, w_ada, b_ada):
    depth, d, n = w_ada.shape
    rows = cc.shape[0]
    bn = 1536
    return pl.pallas_call(
        _ada_kernel,
        grid=(depth, n // bn),
        in_specs=[
            pl.BlockSpec((rows, d), lambda l, j: (0, 0)),
            pl.BlockSpec((None, d, bn), lambda l, j: (l, 0, j)),
            pl.BlockSpec((None, 1, bn), lambda l, j: (l, 0, j)),
        ],
        out_specs=pl.BlockSpec((None, rows, bn), lambda l, j: (l, 0, j)),
        out_shape=jax.ShapeDtypeStruct((depth, rows, n), F32),
        compiler_params=_cparams("parallel", "parallel"),
    )(cc, w_ada, b_ada.reshape(depth, 1, n))


_QK_WIDTH = 2 * N_HEADS * HEAD_DIM
_Z_PARTS = (("v", 512), ("o", 512), ("uv", 512), ("pool", 256), ("mg", 3072))
_N_GATE_COLS = 4 * N_HEADS


def _inproj_kernel(x_ref, xp_ref, xn_ref, mod_ref, w_ref, b_ref, wg_ref, bg_ref, wgt_ref, bgt_ref,
                   cw_ref, cb_ref, qk_ref, v_ref, o_ref, uv_ref, p_ref, mg_ref, g_ref, gt_ref, pad_sc,
                   *, n_lat, seq):
    tm, d = x_ref.shape
    i = pl.program_id(0)
    is_lat = i * tm < n_lat
    has_prev = jnp.logical_and(is_lat, (i * tm) % seq != 0).astype(F32)
    has_next = jnp.logical_and(is_lat, ((i + 1) * tm) % seq != 0).astype(F32)

    xe = jnp.concatenate([xp_ref[...], x_ref[...], xn_ref[...]], axis=0)
    he = (_ln_plain(xe) * (1.0 + mod_ref[:, d:2 * d]) + mod_ref[:, 0:d]).astype(BF16)
    hb = he[SUBLANES:SUBLANES + tm, :]

    half = _QK_WIDTH // 2
    for c0 in range(0, _QK_WIDTH, half):
        z = jnp.dot(he, w_ref[:, c0:c0 + half], preferred_element_type=F32) + b_ref[:, c0:c0 + half]
        pad_sc[...] = z
        pad_sc[SUBLANES - 1:SUBLANES, :] = pad_sc[SUBLANES - 1:SUBLANES, :] * has_prev
        pad_sc[SUBLANES + tm:SUBLANES + tm + 1, :] = pad_sc[SUBLANES + tm:SUBLANES + tm + 1, :] * has_next
        y = (cb_ref[:, c0:c0 + half]
             + pad_sc[SUBLANES - 1:SUBLANES - 1 + tm, :] * cw_ref[0:1, c0:c0 + half]
             + pad_sc[SUBLANES:SUBLANES + tm, :] * cw_ref[1:2, c0:c0 + half]
             + pad_sc[SUBLANES + 1:SUBLANES + 1 + tm, :] * cw_ref[2:3, c0:c0 + half])
        act = y * _sigmoid(y)
        if c0 >= half:
            act = act * (HEAD_DIM ** -0.5)
        qk_ref[:, c0:c0 + half] = act.astype(qk_ref.dtype)

    off = _QK_WIDTH
    for ref, (_, width) in zip((v_ref, o_ref, uv_ref, p_ref, mg_ref), _Z_PARTS):
        for c0 in range(0, width, 512):
            cw = min(512, width - c0)
            z = jnp.dot(hb, w_ref[:, off + c0:off + c0 + cw], preferred_element_type=F32)
            ref[:, c0:c0 + cw] = (z + b_ref[:, off + c0:off + c0 + cw]).astype(ref.dtype)
        off += width
    gg = jnp.dot(hb, wg_ref[...], preferred_element_type=F32) + bg_ref[...]
    gi, gf = gg[:, :LANES], gg[:, LANES:]
    lane = lax.broadcasted_iota(jnp.int32, (tm, LANES), 1)
    pos = lax.broadcasted_iota(jnp.int32, (tm, LANES), 0) % CHUNK
    fwd = lane < N_HEADS
    lsg = _log_sigmoid(gf)
    bcum = jnp.where(fwd, _chunk_scan(lsg, pos, 0, False, False), _chunk_scan(lsg, pos, 0, True, False))
    c = gi - bcum
    cmax = jnp.where(fwd, _chunk_scan(c, pos, 0, False, True), _chunk_scan(c, pos, 0, True, True))
    nd = 2 * N_HEADS
    g_ref[...] = jnp.where(lane < nd, bcum,
                           jnp.where(lane < 2 * nd, pltpu.roll(c, nd, axis=1),
                                     jnp.where(lane < 3 * nd, pltpu.roll(bcum + cmax, 2 * nd, axis=1),
                                               pltpu.roll(cmax, 3 * nd, axis=1))))
    ggt = lax.dot_general(wgt_ref[...], hb, (((1,), (1,)), ((), ())),
                          preferred_element_type=F32) + bgt_ref[...]
    git, gft = ggt[:nd, :], ggt[nd:, :]
    fwd_t = lax.broadcasted_iota(jnp.int32, (nd, tm), 0) < N_HEADS
    pos_t = lax.broadcasted_iota(jnp.int32, (nd, tm), 1) % CHUNK
    lsg_t = _log_sigmoid(gft)
    bcum_t = jnp.where(fwd_t, _chunk_scan(lsg_t, pos_t, 1, False, False), _chunk_scan(lsg_t, pos_t, 1, True, False))
    gt_ref[...] = git - bcum_t


def _chunk_scan(a, pos, axis, reverse, use_max):
    n = a.shape[axis]
    ident = -jnp.inf if use_max else 0.0
    sh = 1
    while sh < CHUNK:
        if reverse:
            moved = jnp.where(pos < CHUNK - sh, pltpu.roll(a, n - sh, axis=axis), ident)
        else:
            moved = jnp.where(pos >= sh, pltpu.roll(a, sh, axis=axis), ident)
        a = jnp.maximum(a, moved) if use_max else a + moved
        sh *= 2
    return a


def _inproj(x, mods, w, b, wg, bg, wgt, bgt, conv_w, conv_b, *, n_lat, seq):
    t, d = x.shape
    tm = ROW_TILE
    n_batch = mods.shape[0] - 1
    per_tile = tm // SUBLANES

    def mod_idx(i):
        return (jnp.minimum((i * tm) // seq, n_batch), 0, 0)

    const = lambda i: (0, 0)
    row = lambda i: (i, 0)
    parts = (("qk", _QK_WIDTH),) + _Z_PARTS
    out_shapes = [jax.ShapeDtypeStruct((t, width), BF16) for _, width in parts]
    out_specs = [pl.BlockSpec((tm, width), row) for _, width in parts]
    out_shapes += [jax.ShapeDtypeStruct((t, LANES), F32), jax.ShapeDtypeStruct((2 * N_HEADS, t), F32)]
    out_specs += [pl.BlockSpec((tm, LANES), row), pl.BlockSpec((2 * N_HEADS, tm), lambda i: (0, i))]
    return pl.pallas_call(
        functools.partial(_inproj_kernel, n_lat=n_lat, seq=seq),
        grid=(t // tm,),
        in_specs=[
            pl.BlockSpec((tm, d), row),
            pl.BlockSpec((SUBLANES, d), lambda i: (jnp.maximum(i * per_tile - 1, 0), 0)),
            pl.BlockSpec((SUBLANES, d), lambda i: (jnp.minimum((i + 1) * per_tile, t // SUBLANES - 1), 0)),
            pl.BlockSpec((None, 1, mods.shape[2]), mod_idx),
            pl.BlockSpec(w.shape, const),
            pl.BlockSpec(b.shape, const),
            pl.BlockSpec(wg.shape, const),
            pl.BlockSpec(bg.shape, const),
            pl.BlockSpec(wgt.shape, const),
            pl.BlockSpec(bgt.shape, const),
            pl.BlockSpec(conv_w.shape, const),
            pl.BlockSpec(conv_b.shape, const),
        ],
        out_specs=out_specs,
        out_shape=out_shapes,
        scratch_shapes=[pltpu.VMEM((tm + 2 * SUBLANES, _QK_WIDTH // 2), F32)],
        compiler_params=_cparams("parallel"),
    )(x, x, x, mods, w, b, wg, bg, wgt, bgt, conv_w, conv_b)


def _mlstm_row_block(j, b, direction, *, n_batch, seq, ctx_len):
    ncc = ctx_len // CHUNK
    ncl = seq // CHUNK
    is_ctx = j < ncc
    if direction == 0:
        c = jnp.where(is_ctx, j, j - ncc)
    else:
        c = jnp.where(is_ctx, ncc - 1 - j, ncl - 1 - (j - ncc))
    base = jnp.where(is_ctx, (n_batch * seq + b * ctx_len) // CHUNK, (b * seq) // CHUNK)
    return base + c


def _mlstm_kernel(q_f, k_f, v_f, g_f, gt_f, q_b, k_b, v_b, g_b, gt_b, hf_ref, hb_ref, s_sc, m_sc):
    j = pl.program_id(1)

    @pl.when(j == 0)
    def _():
        s_sc[...] = jnp.zeros_like(s_sc)
        m_sc[...] = jnp.zeros_like(m_sc)

    ri = lax.broadcasted_iota(jnp.int32, (CHUNK, CHUNK), 0)
    ci = lax.broadcasted_iota(jnp.int32, (CHUNK, CHUNK), 1)
    ones_tile = jnp.ones((CHUNK, HEAD_DIM), BF16)
    dirs = ((q_f, k_f, v_f, g_f, gt_f, hf_ref), (q_b, k_b, v_b, g_b, gt_b, hb_ref))
    combos = [(d, hh) for d in range(2) for hh in range(N_HEADS)]
    heads = lambda ref, hh: ref[:, hh * HEAD_DIM:(hh + 1) * HEAD_DIM]

    nd = 2 * N_HEADS
    tri = ((ci <= ri), (ci >= ri))
    g = (g_f[...], g_b[...])
    gt = (gt_f[...], gt_b[...])

    qk_s, v1, state, m_prev = {}, {}, {}, {}
    for d, hh in combos:
        q_ref, k_ref, v_ref = dirs[d][:3]
        qk_s[d, hh] = lax.dot_general(heads(q_ref, hh), heads(k_ref, hh), (((1,), (1,)), ((), ())),
                                      preferred_element_type=F32)
        v1[d, hh] = jnp.concatenate([heads(v_ref, hh), ones_tile], axis=1)
        state[d, hh] = s_sc[d, hh]
        m_prev[d, hh] = m_sc[d * N_HEADS + hh][0:1, 0:1]

    q_state = {}
    for d, hh in combos:
        q_state[d, hh] = jnp.dot(heads(dirs[d][0], hh), state[d, hh].astype(BF16), preferred_element_type=F32)

    m_t, decay, scores = {}, {}, {}
    for d, hh in combos:
        col = d * N_HEADS + hh
        bcum = g[d][:, col:col + 1]
        inter = bcum + m_prev[d, hh]
        m_t[d, hh] = jnp.maximum(inter, g[d][:, 2 * nd + col:2 * nd + col + 1])
        decay[d, hh] = jnp.exp(inter - m_t[d, hh])
        dlog = jnp.where(tri[d], (bcum - m_t[d, hh]) + gt[d][col:col + 1, :], -jnp.inf)
        scores[d, hh] = (qk_s[d, hh] * jnp.exp(dlog)).astype(BF16)

    for d, hh in combos:
        numden = decay[d, hh] * q_state[d, hh] + jnp.dot(scores[d, hh], v1[d, hh], preferred_element_type=F32)
        num = numden[:, :HEAD_DIM]
        den = numden[:, HEAD_DIM:]
        out = num / jnp.maximum(jnp.abs(den), jnp.exp(-m_t[d, hh]))
        out_ref = dirs[d][5]
        out_ref[:, hh * HEAD_DIM:(hh + 1) * HEAD_DIM] = out.astype(out_ref.dtype)

    kw, a, m_new = {}, {}, {}
    for d, hh in combos:
        col = d * N_HEADS + hh
        last = CHUNK - 1 if d == 0 else 0
        total = g[d][last:last + 1, col:col + 1]
        m_new[d, hh] = total + jnp.maximum(m_prev[d, hh], g[d][last:last + 1, 3 * nd + col:3 * nd + col + 1])
        a[d, hh] = jnp.exp(total + m_prev[d, hh] - m_new[d, hh])
        wk = jnp.exp(total + g[d][:, nd + col:nd + col + 1] - m_new[d, hh])
        kw[d, hh] = (heads(dirs[d][1], hh).astype(F32) * wk).astype(BF16)

    for d, hh in combos:
        upd = lax.dot_general(kw[d, hh], v1[d, hh], (((0,), (0,)), ((), ())), preferred_element_type=F32)
        s_sc[d, hh] = a[d, hh] * state[d, hh] + upd
        m_sc[d * N_HEADS + hh] = jnp.broadcast_to(m_new[d, hh], m_sc.shape[1:])


def _mlstm(zqk, zv, g, gt, *, n_batch, seq, ctx_len):
    t = zqk.shape[0]
    nsteps = (seq + ctx_len) // CHUNK
    width = N_HEADS * HEAD_DIM
    idx = functools.partial(_mlstm_row_block, n_batch=n_batch, seq=seq, ctx_len=ctx_len)

    def specs(d):
        rb = lambda b, j: idx(j, b, d)
        return [
            pl.BlockSpec((CHUNK, width), lambda b, j: (rb(b, j), 0)),
            pl.BlockSpec((CHUNK, width), lambda b, j: (rb(b, j), 1)),
            pl.BlockSpec((CHUNK, width), lambda b, j: (rb(b, j), 0)),
            pl.BlockSpec((CHUNK, LANES), lambda b, j: (rb(b, j), 0)),
            pl.BlockSpec((2 * N_HEADS, CHUNK), lambda b, j: (0, rb(b, j))),
        ]

    out_spec = lambda d: pl.BlockSpec((CHUNK, width), lambda b, j: (idx(j, b, d), 0))
    return pl.pallas_call(
        _mlstm_kernel,
        grid=(n_batch, nsteps),
        in_specs=specs(0) + specs(1),
        out_specs=[out_spec(0), out_spec(1)],
        out_shape=[jax.ShapeDtypeStruct((t, width), BF16)] * 2,
        scratch_shapes=[
            pltpu.VMEM((2, N_HEADS, HEAD_DIM, 2 * HEAD_DIM), F32),
            pltpu.VMEM((2 * N_HEADS, 8, LANES), F32),
        ],
        compiler_params=_cparams("parallel", "arbitrary"),
    )(zqk, zqk, zv, g, gt, zqk, zqk, zv, g, gt)


def _pool_constants(ctx_len):
    tm = ROW_TILE
    masks = np.zeros((2, len(POOL_WINDOWS), tm, tm), np.float32)
    cnts = np.zeros((2, tm, len(POOL_WINDOWS) * POOL_GROUP_DIM), np.float32)
    for kind, length in enumerate((GRID_W, ctx_len)):
        for gi, w in enumerate(POOL_WINDOWS):
            for t in range(tm):
                base = (t // length) * length
                tl = t - base
                lo = min(max(tl - w // 2, 0), length)
                hi = min(max(tl + w // 2, 0), length)
                masks[kind, gi, t, base + lo:base + hi] = 1.0
                cnts[kind, t, gi * POOL_GROUP_DIM:(gi + 1) * POOL_GROUP_DIM] = hi - lo
    return jnp.asarray(masks, BF16), jnp.asarray(cnts, F32)


def _mix_kernel(x_ref, zp_ref, zuv_ref, zmg_ref, zo_ref, hf_ref, hb_ref, mod_ref,
                pm_ref, cnt_ref, pw_ref, ps_ref, sg_ref, sb_ref, sw_ref, sbm_ref, ng_ref,
                wbp_ref, wbm_ref, wbs_ref, wo_ref, l1g_ref, l1b_ref, wr_ref, br_ref,
                x1_ref, h2_ref, route_ref, cnt_out_ref, cnt_sc, *, alpha):
    tm, d = x_ref.shape
    i = pl.program_id(0)

    @pl.when(i == 0)
    def _():
        cnt_sc[...] = jnp.zeros_like(cnt_sc)

    lane256 = lax.broadcasted_iota(jnp.int32, (1, 4 * POOL_GROUP_DIM), 1) // POOL_GROUP_DIM

    xp = zp_ref[...]
    xp_f = xp.astype(F32)
    pooled = jnp.zeros(xp_f.shape, F32)
    for gi in range(len(POOL_WINDOWS)):
        s = jnp.dot(pm_ref[gi], xp, preferred_element_type=F32)
        pooled = jnp.where(lane256 == gi, s, pooled)
    diff = pooled / cnt_ref[...] - xp_f
    pool_o = jnp.dot(diff.astype(BF16), pw_ref[...], preferred_element_type=F32) * ps_ref[...]

    uv = zuv_ref[...].astype(F32)
    uv = 0.5 * uv * (1.0 + lax.erf(uv * (2.0 ** -0.5)))
    half = uv.shape[1] // 2
    u = uv[:, :half]
    vn = (_ln_plain(uv[:, half:]) * sg_ref[...] + sb_ref[...]).astype(BF16)
    mixed_chunks = []
    for c0 in range(0, tm, CHUNK):
        vc = vn[c0:c0 + CHUNK, :]
        mixed = jnp.zeros((CHUNK, half), F32)
        for gi in range(sw_ref.shape[0]):
            s = jnp.dot(sw_ref[gi], vc, preferred_element_type=F32)
            mixed = jnp.where(lane256 == gi, s, mixed)
        mixed_chunks.append(mixed + sbm_ref[...])
    sgu_o = u * jnp.concatenate(mixed_chunks, axis=0)

    hsum = hf_ref[...].astype(F32) + hb_ref[...].astype(F32)
    heads = []
    for hh in range(N_HEADS):
        hv = hsum[:, hh * HEAD_DIM:(hh + 1) * HEAD_DIM]
        mu = jnp.mean(hv, axis=-1, keepdims=True)
        hc = hv - mu
        var = jnp.mean(hc * hc, axis=-1, keepdims=True)
        heads.append(hc * lax.rsqrt(var + HEAD_NORM_EPS))
    hn = jnp.concatenate(heads, axis=1)
    mlstm_o = hn * ng_ref[...] * _sigmoid(zo_ref[...].astype(F32))

    y = _sigmoid(zmg_ref[:, 0:d].astype(F32)) * jnp.dot(pool_o.astype(BF16), wbp_ref[...],
                                                         preferred_element_type=F32)
    y += _sigmoid(zmg_ref[:, d:2 * d].astype(F32)) * jnp.dot(mlstm_o.astype(BF16), wbm_ref[...],
                                                             preferred_element_type=F32)
    y += _sigmoid(zmg_ref[:, 2 * d:3 * d].astype(F32)) * jnp.dot(sgu_o.astype(BF16), wbs_ref[...],
                                                                 preferred_element_type=F32)
    y2 = jnp.dot(y.astype(BF16), wo_ref[...], preferred_element_type=F32)

    g1 = mod_ref[:, 2 * d:3 * d]
    x1 = _ln_plain(alpha * x_ref[...] + g1 * y2) * l1g_ref[...] + l1b_ref[...]
    x1_ref[...] = x1
    h2 = _ln_plain(x1) * (1.0 + mod_ref[:, 4 * d:5 * d]) + mod_ref[:, 3 * d:4 * d]
    h2b = h2.astype(BF16)
    h2_ref[...] = h2b

    logits = jnp.dot(h2b, wr_ref[...], preferred_element_type=F32) + br_ref[...]
    lane = lax.broadcasted_iota(jnp.int32, logits.shape, 1)
    work = logits
    vals, idxs, hots = [], [], []
    for _ in range(TOP_K):
        mval = jnp.max(work, axis=-1, keepdims=True)
        idx = jnp.min(jnp.where(work == mval, lane, LANES), axis=-1, keepdims=True)
        hot = lane == idx
        vals.append(mval)
        idxs.append(idx)
        hots.append(hot)
        work = jnp.where(hot, -jnp.inf, work)
    exps = [jnp.exp(v - vals[0]) for v in vals]
    denom = exps[0] + exps[1] + exps[2] + exps[3]
    sel = (hots[0] | hots[1] | hots[2] | hots[3])
    r_i = lax.broadcasted_iota(jnp.int32, (tm, tm), 0)
    c_i = lax.broadcasted_iota(jnp.int32, (tm, tm), 1)
    before = (c_i < r_i).astype(BF16)
    sel_b = sel.astype(F32).astype(BF16)
    rank = jnp.dot(before, sel_b, preferred_element_type=F32) + cnt_sc[0:1, :]
    route = jnp.zeros(logits.shape, F32)
    for kk in range(TOP_K):
        rk = jnp.sum(jnp.where(hots[kk], rank, 0.0), axis=-1, keepdims=True)
        route = jnp.where(lane == kk, idxs[kk].astype(F32), route)
        route = jnp.where(lane == TOP_K + kk, exps[kk] / denom, route)
        route = jnp.where(lane == 2 * TOP_K + kk, rk, route)
    route_ref[...] = route
    new_cnt = cnt_sc[0:1, :] + jnp.sum(sel.astype(F32), axis=0, keepdims=True)
    cnt_sc[...] = jnp.broadcast_to(new_cnt, cnt_sc.shape)
    cnt_out_ref[...] = jnp.broadcast_to(new_cnt, cnt_out_ref.shape)


def _mix(x, zp, zuv, zmg, zo, hf, hb, mods, consts, *, n_rows, n_lat, seq, alpha):
    t, d = x.shape
    tm = ROW_TILE
    n_batch = mods.shape[0] - 1
    row = lambda i: (i, 0)
    const2 = lambda i: (0, 0)
    const3 = lambda i: (0, 0, 0)
    kind = lambda i: jnp.where(i * tm >= n_lat, 1, 0)

    def mod_idx(i):
        return (jnp.minimum((i * tm) // seq, n_batch), 0, 0)

    (pm, cnt, pw, ps, sg, sb, sw, sbm, ng, wbp, wbm, wbs, wo, l1g, l1b, wr, br) = consts
    in_specs = [
        pl.BlockSpec((tm, d), row),
        pl.BlockSpec((tm, zp.shape[1]), row),
        pl.BlockSpec((tm, zuv.shape[1]), row),
        pl.BlockSpec((tm, zmg.shape[1]), row),
        pl.BlockSpec((tm, zo.shape[1]), row),
        pl.BlockSpec((tm, hf.shape[1]), row),
        pl.BlockSpec((tm, hb.shape[1]), row),
        pl.BlockSpec((None, 1, mods.shape[2]), mod_idx),
        pl.BlockSpec((None,) + pm.shape[1:], lambda i: (kind(i), 0, 0, 0)),
        pl.BlockSpec((None,) + cnt.shape[1:], lambda i: (kind(i), 0, 0)),
        pl.BlockSpec(pw.shape, const2),
        pl.BlockSpec(ps.shape, const2),
        pl.BlockSpec(sg.shape, const2),
        pl.BlockSpec(sb.shape, const2),
        pl.BlockSpec(sw.shape, const3),
        pl.BlockSpec(sbm.shape, const2),
        pl.BlockSpec(ng.shape, const2),
        pl.BlockSpec(wbp.shape, const2),
        pl.BlockSpec(wbm.shape, const2),
        pl.BlockSpec(wbs.shape, const2),
        pl.BlockSpec(wo.shape, const2),
        pl.BlockSpec(l1g.shape, const2),
        pl.BlockSpec(l1b.shape, const2),
        pl.BlockSpec(wr.shape, const2),
        pl.BlockSpec(br.shape, const2),
    ]
    return pl.pallas_call(
        functools.partial(_mix_kernel, alpha=alpha),
        grid=(n_rows // tm,),
        in_specs=in_specs,
        out_specs=[
            pl.BlockSpec((tm, d), row),
            pl.BlockSpec((tm, d), row),
            pl.BlockSpec((tm, LANES), row),
            pl.BlockSpec((8, LANES), const2),
        ],
        out_shape=[
            jax.ShapeDtypeStruct((n_rows, d), F32),
            jax.ShapeDtypeStruct((n_rows, d), BF16),
            jax.ShapeDtypeStruct((n_rows, LANES), F32),
            jax.ShapeDtypeStruct((8, LANES), F32),
        ],
        scratch_shapes=[pltpu.VMEM((8, LANES), F32)],
        compiler_params=_cparams("arbitrary"),
    )(x, zp, zuv, zmg, zo, hf, hb, mods, pm, cnt, pw, ps, sg, sb, sw, sbm, ng,
      wbp, wbm, wbs, wo, l1g, l1b, wr, br)


def _expert_kernel(te_ref, nt_ref, xs_ref, wgu_ref, bgu_ref, wd_ref, bd_ref, ys_ref, wgu_sc, wd_sc):
    i = pl.program_id(0)
    f = wd_ref.shape[0]
    used = i < nt_ref[0]

    @pl.when(jnp.logical_and(used, jnp.logical_or(i == 0, te_ref[i] != te_ref[jnp.maximum(i - 1, 0)])))
    def _():
        for c0 in range(0, 2 * f, 512):
            wgu_sc[:, c0:c0 + 512] = wgu_ref[:, c0:c0 + 512].astype(BF16)
        for c0 in range(0, f, 256):
            wd_sc[c0:c0 + 256, :] = wd_ref[c0:c0 + 256, :].astype(BF16)

    @pl.when(used)
    def _():
        x = xs_ref[...]
        acc = jnp.zeros(ys_ref.shape, F32)
        for c0 in range(0, f, 512):
            gate = jnp.dot(x, wgu_sc[:, c0:c0 + 512], preferred_element_type=F32) + bgu_ref[:, c0:c0 + 512]
            up = (jnp.dot(x, wgu_sc[:, f + c0:f + c0 + 512], preferred_element_type=F32)
                  + bgu_ref[:, f + c0:f + c0 + 512])
            gate = jnp.minimum(gate, SWIGLU_LIMIT)
            up = jnp.clip(up, -SWIGLU_LIMIT, SWIGLU_LIMIT)
            glu = gate * _sigmoid(SWIGLU_ALPHA * gate)
            act = ((up + 1.0) * glu).astype(BF16)
            acc += jnp.dot(act, wd_sc[c0:c0 + 512, :], preferred_element_type=F32)
        ys_ref[...] = (acc + bd_ref[...]).astype(ys_ref.dtype)

    @pl.when(jnp.logical_not(used))
    def _():
        ys_ref[...] = jnp.zeros_like(ys_ref)


def _experts(tile_expert, n_tiles_used, xs, wgu, bgu, wd, bd, *, layer):
    p, d = xs.shape
    tm = EXPERT_TILE
    f2 = wgu.shape[3]
    grid_spec = pltpu.PrefetchScalarGridSpec(
        num_scalar_prefetch=2,
        grid=(p // tm,),
        in_specs=[
            pl.BlockSpec((tm, d), lambda i, te, nt: (i, 0)),
            pl.BlockSpec((None, None, d, f2), lambda i, te, nt: (layer, te[i], 0, 0)),
            pl.BlockSpec((None, None, 1, f2), lambda i, te, nt: (layer, te[i], 0, 0)),
            pl.BlockSpec((None, None, f2 // 2, d), lambda i, te, nt: (layer, te[i], 0, 0)),
            pl.BlockSpec((None, None, 1, d), lambda i, te, nt: (layer, te[i], 0, 0)),
        ],
        out_specs=pl.BlockSpec((tm, d), lambda i, te, nt: (i, 0)),
        scratch_shapes=[pltpu.VMEM((d, f2), BF16), pltpu.VMEM((f2 // 2, d), BF16)],
    )
    return pl.pallas_call(
        _expert_kernel,
        grid_spec=grid_spec,
        out_shape=jax.ShapeDtypeStruct((p, d), BF16),
        compiler_params=_cparams("arbitrary"),
    )(tile_expert, n_tiles_used, xs, wgu, bgu, wd, bd)


def _combine_kernel(x1_ref, yg_ref, route_ref, mod_ref, g_ref, b_ref, x2_ref, *, alpha):
    d = x1_ref.shape[1]
    route = route_ref[...]
    f = jnp.zeros(x1_ref.shape, F32)
    for kk in range(TOP_K):
        f += route[:, TOP_K + kk:TOP_K + kk + 1] * yg_ref[kk].astype(F32)
    g2 = mod_ref[:, 5 * d:6 * d]
    x2_ref[...] = _ln_plain(alpha * x1_ref[...] + g2 * f) * g_ref[...] + b_ref[...]


def _combine(x1, yg, route, mods, g, b, *, seq, alpha):
    t, d = x1.shape
    tm = ROW_TILE
    n_batch = mods.shape[0] - 1
    row = lambda i: (i, 0)

    def mod_idx(i):
        return (jnp.minimum((i * tm) // seq, n_batch), 0, 0)

    return pl.pallas_call(
        functools.partial(_combine_kernel, alpha=alpha),
        grid=(t // tm,),
        in_specs=[
            pl.BlockSpec((tm, d), row),
            pl.BlockSpec((TOP_K, tm, d), lambda i: (0, i, 0)),
            pl.BlockSpec((tm, LANES), row),
            pl.BlockSpec((None, 1, mods.shape[2]), mod_idx),
            pl.BlockSpec(g.shape, lambda i: (0, 0)),
            pl.BlockSpec(b.shape, lambda i: (0, 0)),
        ],
        out_specs=pl.BlockSpec((tm, d), row),
        out_shape=jax.ShapeDtypeStruct((t, d), F32),
        compiler_params=_cparams("parallel"),
    )(x1, yg, route, mods, g, b)


def kernel(x, c, ctx, c_ctx, w_ada, b_ada, w_in, b_in, pool_w, pool_scale, qk_conv_w, qk_conv_b,
           mlstm_norm_g, sgu_ln_g, sgu_ln_b, sgu_w, sgu_b, w_br_pool, w_br_mlstm, w_br_sgu, w_out,
           ln1_g, ln1_b, w_router, b_router, w_gate_up, b_gate_up, w_down, b_down, ln2_g, ln2_b):
    n_batch, seq, d = x.shape
    ctx_len = ctx.shape[1]
    depth = w_in.shape[0]
    n_exp = w_router.shape[2]
    n_lat = n_batch * seq
    t_all = n_lat + n_batch * ctx_len
    assert ctx_len == ROW_TILE and seq % ROW_TILE == 0 and ROW_TILE % GRID_W == 0
    alpha = float((2 * depth) ** 0.25)

    rows = ((n_batch + 1 + 7) // 8) * 8
    cc = jnp.zeros((rows, d), F32).at[:n_batch].set(c).at[n_batch].set(c_ctx)
    mods_all = _ada_table(cc, w_ada, b_ada)[:, :n_batch + 1].reshape(depth, n_batch + 1, 1, 6 * d)

    pool_w_, mlw = 256, 512
    col = lambda a, lo, hi: a[..., lo:hi]
    g_lo = pool_w_ + 4 * mlw
    parts = ((pool_w_, pool_w_ + 2 * mlw), (pool_w_ + 2 * mlw, pool_w_ + 3 * mlw), (pool_w_ + 3 * mlw, g_lo),
             (g_lo + _N_GATE_COLS, g_lo + _N_GATE_COLS + 512), (0, pool_w_),
             (g_lo + _N_GATE_COLS + 512, w_in.shape[2]))
    w_main = jnp.concatenate([col(w_in, lo, hi) for lo, hi in parts], axis=-1).astype(BF16)
    b_main = jnp.concatenate([col(b_in, lo, hi) for lo, hi in parts], axis=-1)[:, None, :]
    gw, gb = col(w_in, g_lo, g_lo + _N_GATE_COLS), col(b_in, g_lo, g_lo + _N_GATE_COLS)
    nh = N_HEADS
    pick = lambda a, k: jnp.concatenate([a[..., k * nh:(k + 1) * nh], a[..., (k + 2) * nh:(k + 3) * nh]], axis=-1)
    lane_pad = lambda a: jnp.pad(a, [(0, 0)] * (a.ndim - 1) + [(0, LANES - 2 * nh)])
    w_gate = jnp.concatenate([lane_pad(pick(gw, 0)), lane_pad(pick(gw, 1))], axis=-1).astype(BF16)
    b_gate = jnp.concatenate([lane_pad(pick(gb, 0)), lane_pad(pick(gb, 1))], axis=-1)[:, None, :]
    w_gate_t = jnp.swapaxes(jnp.concatenate([pick(gw, 0), pick(gw, 1)], axis=-1), 1, 2).astype(BF16)
    b_gate_t = jnp.concatenate([pick(gb, 0), pick(gb, 1)], axis=-1)[:, :, None]

    pm, cnt = _pool_constants(ctx_len)
    eye = jnp.eye(len(POOL_WINDOWS), dtype=F32)
    pool_bd = jnp.einsum('lgcd,gh->lgchd', pool_w, eye).reshape(depth, 256, 256).astype(BF16)
    sgu_bias_map = jnp.repeat(jnp.swapaxes(sgu_b, 1, 2), sgu_ln_g.shape[1] // sgu_b.shape[1], axis=2)
    w_router_p = jnp.pad(w_router, ((0, 0), (0, 0), (0, LANES - n_exp))).astype(BF16)
    b_router_p = jnp.pad(b_router, ((0, 0), (0, LANES - n_exp)), constant_values=NEG_BIG)[:, None, :]
    r1 = lambda a: a[:, None, :]
    b_gu4 = b_gate_up[:, :, None, :]
    b_d4 = b_down[:, :, None, :]

    stream = jnp.concatenate([x.reshape(n_lat, d), ctx.reshape(n_batch * ctx_len, d)], axis=0)
    for i in range(depth):
        last = i == depth - 1
        mods = mods_all[i]
        zqk, zv, zo, zuv, zp, zmg, g, gt = _inproj(
            stream, mods, w_main[i], b_main[i], w_gate[i], b_gate[i], w_gate_t[i], b_gate_t[i],
            qk_conv_w[i], r1(qk_conv_b)[i], n_lat=n_lat, seq=seq)
        hf, hb = _mlstm(zqk, zv, g, gt, n_batch=n_batch, seq=seq, ctx_len=ctx_len)
        n_rows = n_lat if last else t_all
        consts = (pm, cnt, pool_bd[i], r1(pool_scale)[i], r1(sgu_ln_g)[i], r1(sgu_ln_b)[i],
                  sgu_w[i].astype(BF16), sgu_bias_map[i], r1(mlstm_norm_g)[i],
                  w_br_pool[i].astype(BF16), w_br_mlstm[i].astype(BF16), w_br_sgu[i].astype(BF16),
                  w_out[i].astype(BF16), r1(ln1_g)[i], r1(ln1_b)[i], w_router_p[i], b_router_p[i])
        x1, h2, route, counts = _mix(stream, zp, zuv, zmg, zo, hf, hb, mods, consts,
                                     n_rows=n_rows, n_lat=n_lat, seq=seq, alpha=alpha)

        cnt_e = counts[0, :n_exp].astype(jnp.int32)
        padded = ((cnt_e + EXPERT_TILE - 1) // EXPERT_TILE) * EXPERT_TILE
        ends = jnp.cumsum(padded)
        starts = ends - padded
        e_idx = route[:, 0:TOP_K].astype(jnp.int32)
        rank = route[:, 2 * TOP_K:3 * TOP_K].astype(jnp.int32)
        pos_t = (starts[e_idx] + rank).T
        p_rows = n_rows * TOP_K + n_exp * EXPERT_TILE
        n_tiles = p_rows // EXPERT_TILE
        tile_start = jnp.arange(n_tiles, dtype=jnp.int32) * EXPERT_TILE
        tile_expert = jnp.minimum(jnp.sum((ends[None, :] <= tile_start[:, None]).astype(jnp.int32), axis=1),
                                  n_exp - 1)
        n_used = (ends[-1] // EXPERT_TILE).astype(jnp.int32).reshape(1)
        tok = jnp.broadcast_to(jnp.arange(n_rows, dtype=jnp.int32)[None, :], pos_t.shape)
        tok_of_pos = jnp.zeros((p_rows,), jnp.int32).at[pos_t.reshape(-1)].set(
            tok.reshape(-1), unique_indices=True, mode="promise_in_bounds")
        xs = h2.at[tok_of_pos].get(mode="promise_in_bounds")
        ys = _experts(tile_expert, n_used, xs, w_gate_up, b_gu4, w_down, b_d4, layer=i)
        yg = ys.at[pos_t.reshape(-1)].get(mode="promise_in_bounds", unique_indices=True)
        yg = yg.reshape(TOP_K, n_rows, d)
        stream = _combine(x1, yg, route, mods, r1(ln2_g)[i], r1(ln2_b)[i], seq=seq, alpha=alpha)
    return stream[:n_lat].reshape(n_batch, seq, d)
```

```python
import functools

import numpy as np
import jax
import jax.numpy as jnp
from jax import lax
from jax.experimental import pallas as pl
from jax.experimental.pallas import tpu as pltpu
from jax.experimental.pallas import tpu_sc as plsc

F32 = jnp.float32
BF16 = jnp.bfloat16

GRID_W = 64
POOL_WINDOWS = (2, 4, 8, 16)
POOL_GROUP_DIM = 64
N_HEADS = 4
HEAD_DIM = 128
CHUNK = 128
TOP_K = 4
SWIGLU_LIMIT = 7.0
SWIGLU_ALPHA = 1.702
LN_EPS = 1e-5
HEAD_NORM_EPS = 1e-6

ROW_TILE = 256
EXPERT_TILE = 512
SUBLANES = 8
LANES = 128
NEG_BIG = -1e30
VMEM_LIMIT = 56 * 1024 * 1024


def _cparams(*sem):
    return pltpu.CompilerParams(dimension_semantics=sem, vmem_limit_bytes=VMEM_LIMIT)


def _ln_plain(x):
    mu = jnp.mean(x, axis=-1, keepdims=True)
    xc = x - mu
    var = jnp.mean(xc * xc, axis=-1, keepdims=True)
    return xc * lax.rsqrt(var + LN_EPS)


def _sigmoid(x):
    return 0.5 * jnp.tanh(0.5 * x) + 0.5


def _log_sigmoid(x):
    return jnp.minimum(x, 0.0) - jnp.log1p(jnp.exp(-jnp.abs(x)))


def _ada_kernel(c_ref, w_ref, b_ref, o_ref):
    c = c_ref[...]
    s = c * _sigmoid(c)
    o_ref[...] = jnp.dot(s, w_ref[...], preferred_element_type=F32,
                         precision=lax.Precision.HIGHEST) + b_ref[...]


def _ada_table(cc, w_ada, b_ada):
    depth, d, n = w_ada.shape
    rows = cc.shape[0]
    bn = 1536
    return pl.pallas_call(
        _ada_kernel,
        grid=(depth, n // bn),
        in_specs=[
            pl.BlockSpec((rows, d), lambda l, j: (0, 0)),
            pl.BlockSpec((None, d, bn), lambda l, j: (l, 0, j)),
            pl.BlockSpec((None, 1, bn), lambda l, j: (l, 0, j)),
        ],
        out_specs=pl.BlockSpec((None, rows, bn), lambda l, j: (l, 0, j)),
        out_shape=jax.ShapeDtypeStruct((depth, rows, n), F32),
        compiler_params=_cparams("parallel", "parallel"),
    )(cc, w_ada, b_ada.reshape(depth, 1, n))


_QK_WIDTH = 2 * N_HEADS * HEAD_DIM
_Z_PARTS = (("v", 512), ("o", 512), ("uv", 512), ("pool", 256), ("mg", 3072))
_N_GATE_COLS = 4 * N_HEADS


def _inproj_kernel(x_ref, xp_ref, xn_ref, mod_ref, w_ref, b_ref, wg_ref, bg_ref, wgt_ref, bgt_ref,
                   cw_ref, cb_ref, qk_ref, v_ref, o_ref, uv_ref, p_ref, mg_ref, g_ref, gt_ref, pad_sc,
                   *, n_lat, seq):
    tm, d = x_ref.shape
    i = pl.program_id(0)
    is_lat = i * tm < n_lat
    has_prev = jnp.logical_and(is_lat, (i * tm) % seq != 0).astype(F32)
    has_next = jnp.logical_and(is_lat, ((i + 1) * tm) % seq != 0).astype(F32)

    xe = jnp.concatenate([xp_ref[...], x_ref[...], xn_ref[...]], axis=0)
    he = (_ln_plain(xe) * (1.0 + mod_ref[:, d:2 * d]) + mod_ref[:, 0:d]).astype(BF16)
    hb = he[SUBLANES:SUBLANES + tm, :]

    half = _QK_WIDTH // 2
    for c0 in range(0, _QK_WIDTH, half):
        z = jnp.dot(he, w_ref[:, c0:c0 + half], preferred_element_type=F32) + b_ref[:, c0:c0 + half]
        pad_sc[...] = z
        pad_sc[SUBLANES - 1:SUBLANES, :] = pad_sc[SUBLANES - 1:SUBLANES, :] * has_prev
        pad_sc[SUBLANES + tm:SUBLANES + tm + 1, :] = pad_sc[SUBLANES + tm:SUBLANES + tm + 1, :] * has_next
        y = (cb_ref[:, c0:c0 + half]
             + pad_sc[SUBLANES - 1:SUBLANES - 1 + tm, :] * cw_ref[0:1, c0:c0 + half]
             + pad_sc[SUBLANES:SUBLANES + tm, :] * cw_ref[1:2, c0:c0 + half]
             + pad_sc[SUBLANES + 1:SUBLANES + 1 + tm, :] * cw_ref[2:3, c0:c0 + half])
        act = y * _sigmoid(y)
        if c0 >= half:
            act = act * (HEAD_DIM ** -0.5)
        qk_ref[:, c0:c0 + half] = act.astype(qk_ref.dtype)

    off = _QK_WIDTH
    for ref, (_, width) in zip((v_ref, o_ref, uv_ref, p_ref, mg_ref), _Z_PARTS):
        for c0 in range(0, width, 512):
            cw = min(512, width - c0)
            z = jnp.dot(hb, w_ref[:, off + c0:off + c0 + cw], preferred_element_type=F32)
            ref[:, c0:c0 + cw] = (z + b_ref[:, off + c0:off + c0 + cw]).astype(ref.dtype)
        off += width
    gg = jnp.dot(hb, wg_ref[...], preferred_element_type=F32) + bg_ref[...]
    gi, gf = gg[:, :LANES], gg[:, LANES:]
    lane = lax.broadcasted_iota(jnp.int32, (tm, LANES), 1)
    pos = lax.broadcasted_iota(jnp.int32, (tm, LANES), 0) % CHUNK
    fwd = lane < N_HEADS
    lsg = _log_sigmoid(gf)
    bcum = jnp.where(fwd, _chunk_scan(lsg, pos, 0, False, False), _chunk_scan(lsg, pos, 0, True, False))
    c = gi - bcum
    cmax = jnp.where(fwd, _chunk_scan(c, pos, 0, False, True), _chunk_scan(c, pos, 0, True, True))
    nd = 2 * N_HEADS
    g_ref[...] = jnp.where(lane < nd, bcum,
                           jnp.where(lane < 2 * nd, pltpu.roll(c, nd, axis=1),
                                     jnp.where(lane < 3 * nd, pltpu.roll(bcum + cmax, 2 * nd, axis=1),
                                               pltpu.roll(cmax, 3 * nd, axis=1))))
    ggt = lax.dot_general(wgt_ref[...], hb, (((1,), (1,)), ((), ())),
                          preferred_element_type=F32) + bgt_ref[...]
    git, gft = ggt[:nd, :], ggt[nd:, :]
    fwd_t = lax.broadcasted_iota(jnp.int32, (nd, tm), 0) < N_HEADS
    pos_t = lax.broadcasted_iota(jnp.int32, (nd, tm), 1) % CHUNK
    lsg_t = _log_sigmoid(gft)
    bcum_t = jnp.where(fwd_t, _chunk_scan(lsg_t, pos_t, 1, False, False), _chunk_scan(lsg_t, pos_t, 1, True, False))
    gt_ref[...] = git - bcum_t


def _chunk_scan(a, pos, axis, reverse, use_max):
    n = a.shape[axis]
    ident = -jnp.inf if use_max else 0.0
    sh = 1
    while sh < CHUNK:
        if reverse:
            moved = jnp.where(pos < CHUNK - sh, pltpu.roll(a, n - sh, axis=axis), ident)
        else:
            moved = jnp.where(pos >= sh, pltpu.roll(a, sh, axis=axis), ident)
        a = jnp.maximum(a, moved) if use_max else a + moved
        sh *= 2
    return a


def _inproj(x, mods, w, b, wg, bg, wgt, bgt, conv_w, conv_b, *, n_lat, seq):
    t, d = x.shape
    tm = ROW_TILE
    n_batch = mods.shape[0] - 1
    per_tile = tm // SUBLANES

    def mod_idx(i):
        return (jnp.minimum((i * tm) // seq, n_batch), 0, 0)

    const = lambda i: (0, 0)
    row = lambda i: (i, 0)
    parts = (("qk", _QK_WIDTH),) + _Z_PARTS
    out_shapes = [jax.ShapeDtypeStruct((t, width), BF16) for _, width in parts]
    out_specs = [pl.BlockSpec((tm, width), row) for _, width in parts]
    out_shapes += [jax.ShapeDtypeStruct((t, LANES), F32), jax.ShapeDtypeStruct((2 * N_HEADS, t), F32)]
    out_specs += [pl.BlockSpec((tm, LANES), row), pl.BlockSpec((2 * N_HEADS, tm), lambda i: (0, i))]
    return pl.pallas_call(
        functools.partial(_inproj_kernel, n_lat=n_lat, seq=seq),
        grid=(t // tm,),
        in_specs=[
            pl.BlockSpec((tm, d), row),
            pl.BlockSpec((SUBLANES, d), lambda i: (jnp.maximum(i * per_tile - 1, 0), 0)),
            pl.BlockSpec((SUBLANES, d), lambda i: (jnp.minimum((i + 1) * per_tile, t // SUBLANES - 1), 0)),
            pl.BlockSpec((None, 1, mods.shape[2]), mod_idx),
            pl.BlockSpec(w.shape, const),
            pl.BlockSpec(b.shape, const),
            pl.BlockSpec(wg.shape, const),
            pl.BlockSpec(bg.shape, const),
            pl.BlockSpec(wgt.shape, const),
            pl.BlockSpec(bgt.shape, const),
            pl.BlockSpec(conv_w.shape, const),
            pl.BlockSpec(conv_b.shape, const),
        ],
        out_specs=out_specs,
        out_shape=out_shapes,
        scratch_shapes=[pltpu.VMEM((tm + 2 * SUBLANES, _QK_WIDTH // 2), F32)],
        compiler_params=_cparams("parallel"),
    )(x, x, x, mods, w, b, wg, bg, wgt, bgt, conv_w, conv_b)


def _mlstm_row_block(j, b, direction, *, n_batch, seq, ctx_len):
    ncc = ctx_len // CHUNK
    ncl = seq // CHUNK
    is_ctx = j < ncc
    if direction == 0:
        c = jnp.where(is_ctx, j, j - ncc)
    else:
        c = jnp.where(is_ctx, ncc - 1 - j, ncl - 1 - (j - ncc))
    base = jnp.where(is_ctx, (n_batch * seq + b * ctx_len) // CHUNK, (b * seq) // CHUNK)
    return base + c


def _mlstm_kernel(q_f, k_f, v_f, g_f, gt_f, q_b, k_b, v_b, g_b, gt_b, hf_ref, hb_ref, s_sc, m_sc):
    j = pl.program_id(1)

    @pl.when(j == 0)
    def _():
        s_sc[...] = jnp.zeros_like(s_sc)
        m_sc[...] = jnp.zeros_like(m_sc)

    ri = lax.broadcasted_iota(jnp.int32, (CHUNK, CHUNK), 0)
    ci = lax.broadcasted_iota(jnp.int32, (CHUNK, CHUNK), 1)
    ones_tile = jnp.ones((CHUNK, HEAD_DIM), BF16)
    dirs = ((q_f, k_f, v_f, g_f, gt_f, hf_ref), (q_b, k_b, v_b, g_b, gt_b, hb_ref))
    combos = [(d, hh) for d in range(2) for hh in range(N_HEADS)]
    heads = lambda ref, hh: ref[:, hh * HEAD_DIM:(hh + 1) * HEAD_DIM]

    nd = 2 * N_HEADS
    tri = ((ci <= ri), (ci >= ri))
    g = (g_f[...], g_b[...])
    gt = (gt_f[...], gt_b[...])

    qk_s, v1, state, m_prev = {}, {}, {}, {}
    for d, hh in combos:
        q_ref, k_ref, v_ref = dirs[d][:3]
        qk_s[d, hh] = lax.dot_general(heads(q_ref, hh), heads(k_ref, hh), (((1,), (1,)), ((), ())),
                                      preferred_element_type=F32)
        v1[d, hh] = jnp.concatenate([heads(v_ref, hh), ones_tile], axis=1)
        state[d, hh] = s_sc[d, hh]
        m_prev[d, hh] = m_sc[d * N_HEADS + hh][0:1, 0:1]

    q_state = {}
    for d, hh in combos:
        q_state[d, hh] = jnp.dot(heads(dirs[d][0], hh), state[d, hh].astype(BF16), preferred_element_type=F32)

    m_t, decay, scores = {}, {}, {}
    for d, hh in combos:
        col = d * N_HEADS + hh
        bcum = g[d][:, col:col + 1]
        inter = bcum + m_prev[d, hh]
        m_t[d, hh] = jnp.maximum(inter, g[d][:, 2 * nd + col:2 * nd + col + 1])
        decay[d, hh] = jnp.exp(inter - m_t[d, hh])
        dlog = jnp.where(tri[d], (bcum - m_t[d, hh]) + gt[d][col:col + 1, :], -jnp.inf)
        scores[d, hh] = (qk_s[d, hh] * jnp.exp(dlog)).astype(BF16)

    for d, hh in combos:
        numden = decay[d, hh] * q_state[d, hh] + jnp.dot(scores[d, hh], v1[d, hh], preferred_element_type=F32)
        num = numden[:, :HEAD_DIM]
        den = numden[:, HEAD_DIM:]
        out = num / jnp.maximum(jnp.abs(den), jnp.exp(-m_t[d, hh]))
        out_ref = dirs[d][5]
        out_ref[:, hh * HEAD_DIM:(hh + 1) * HEAD_DIM] = out.astype(out_ref.dtype)

    kw, a, m_new = {}, {}, {}
    for d, hh in combos:
        col = d * N_HEADS + hh
        last = CHUNK - 1 if d == 0 else 0
        total = g[d][last:last + 1, col:col + 1]
        m_new[d, hh] = total + jnp.maximum(m_prev[d, hh], g[d][last:last + 1, 3 * nd + col:3 * nd + col + 1])
        a[d, hh] = jnp.exp(total + m_prev[d, hh] - m_new[d, hh])
        wk = jnp.exp(total + g[d][:, nd + col:nd + col + 1] - m_new[d, hh])
        kw[d, hh] = (heads(dirs[d][1], hh).astype(F32) * wk).astype(BF16)

    for d, hh in combos:
        upd = lax.dot_general(kw[d, hh], v1[d, hh], (((0,), (0,)), ((), ())), preferred_element_type=F32)
        s_sc[d, hh] = a[d, hh] * state[d, hh] + upd
        m_sc[d * N_HEADS + hh] = jnp.broadcast_to(m_new[d, hh], m_sc.shape[1:])


def _mlstm(zqk, zv, g, gt, *, n_batch, seq, ctx_len):
    t = zqk.shape[0]
    nsteps = (seq + ctx_len) // CHUNK
    width = N_HEADS * HEAD_DIM
    idx = functools.partial(_mlstm_row_block, n_batch=n_batch, seq=seq, ctx_len=ctx_len)

    def specs(d):
        rb = lambda b, j: idx(j, b, d)
        return [
            pl.BlockSpec((CHUNK, width), lambda b, j: (rb(b, j), 0)),
            pl.BlockSpec((CHUNK, width), lambda b, j: (rb(b, j), 1)),
            pl.BlockSpec((CHUNK, width), lambda b, j: (rb(b, j), 0)),
            pl.BlockSpec((CHUNK, LANES), lambda b, j: (rb(b, j), 0)),
            pl.BlockSpec((2 * N_HEADS, CHUNK), lambda b, j: (0, rb(b, j))),
        ]

    out_spec = lambda d: pl.BlockSpec((CHUNK, width), lambda b, j: (idx(j, b, d), 0))
    return pl.pallas_call(
        _mlstm_kernel,
        grid=(n_batch, nsteps),
        in_specs=specs(0) + specs(1),
        out_specs=[out_spec(0), out_spec(1)],
        out_shape=[jax.ShapeDtypeStruct((t, width), BF16)] * 2,
        scratch_shapes=[
            pltpu.VMEM((2, N_HEADS, HEAD_DIM, 2 * HEAD_DIM), F32),
            pltpu.VMEM((2 * N_HEADS, 8, LANES), F32),
        ],
        compiler_params=_cparams("parallel", "arbitrary"),
    )(zqk, zqk, zv, g, gt, zqk, zqk, zv, g, gt)


def _pool_constants(ctx_len):
    tm = ROW_TILE
    masks = np.zeros((2, len(POOL_WINDOWS), tm, tm), np.float32)
    cnts = np.zeros((2, tm, len(POOL_WINDOWS) * POOL_GROUP_DIM), np.float32)
    for kind, length in enumerate((GRID_W, ctx_len)):
        for gi, w in enumerate(POOL_WINDOWS):
            for t in range(tm):
                base = (t // length) * length
                tl = t - base
                lo = min(max(tl - w // 2, 0), length)
                hi = min(max(tl + w // 2, 0), length)
                masks[kind, gi, t, base + lo:base + hi] = 1.0
                cnts[kind, t, gi * POOL_GROUP_DIM:(gi + 1) * POOL_GROUP_DIM] = hi - lo
    return jnp.asarray(masks, BF16), jnp.asarray(cnts, F32)


def _mix_kernel(x_ref, zp_ref, zuv_ref, zmg_ref, zo_ref, hf_ref, hb_ref, mod_ref,
                pm_ref, cnt_ref, pw_ref, ps_ref, sg_ref, sb_ref, sw_ref, sbm_ref, ng_ref,
                wbp_ref, wbm_ref, wbs_ref, wo_ref, l1g_ref, l1b_ref, wr_ref, br_ref,
                x1_ref, h2_ref, route_ref, cnt_out_ref, cnt_sc, *, alpha):
    tm, d = x_ref.shape
    i = pl.program_id(0)

    @pl.when(i == 0)
    def _():
        cnt_sc[...] = jnp.zeros_like(cnt_sc)

    lane256 = lax.broadcasted_iota(jnp.int32, (1, 4 * POOL_GROUP_DIM), 1) // POOL_GROUP_DIM

    xp = zp_ref[...]
    xp_f = xp.astype(F32)
    pooled = jnp.zeros(xp_f.shape, F32)
    for gi in range(len(POOL_WINDOWS)):
        s = jnp.dot(pm_ref[gi], xp, preferred_element_type=F32)
        pooled = jnp.where(lane256 == gi, s, pooled)
    diff = pooled / cnt_ref[...] - xp_f
    pool_o = jnp.dot(diff.astype(BF16), pw_ref[...], preferred_element_type=F32) * ps_ref[...]

    uv = zuv_ref[...].astype(F32)
    uv = 0.5 * uv * (1.0 + lax.erf(uv * (2.0 ** -0.5)))
    half = uv.shape[1] // 2
    u = uv[:, :half]
    vn = (_ln_plain(uv[:, half:]) * sg_ref[...] + sb_ref[...]).astype(BF16)
    mixed_chunks = []
    for c0 in range(0, tm, CHUNK):
        vc = vn[c0:c0 + CHUNK, :]
        mixed = jnp.zeros((CHUNK, half), F32)
        for gi in range(sw_ref.shape[0]):
            s = jnp.dot(sw_ref[gi], vc, preferred_element_type=F32)
            mixed = jnp.where(lane256 == gi, s, mixed)
        mixed_chunks.append(mixed + sbm_ref[...])
    sgu_o = u * jnp.concatenate(mixed_chunks, axis=0)

    hsum = hf_ref[...].astype(F32) + hb_ref[...].astype(F32)
    heads = []
    for hh in range(N_HEADS):
        hv = hsum[:, hh * HEAD_DIM:(hh + 1) * HEAD_DIM]
        mu = jnp.mean(hv, axis=-1, keepdims=True)
        hc = hv - mu
        var = jnp.mean(hc * hc, axis=-1, keepdims=True)
        heads.append(hc * lax.rsqrt(var + HEAD_NORM_EPS))
    hn = jnp.concatenate(heads, axis=1)
    mlstm_o = hn * ng_ref[...] * _sigmoid(zo_ref[...].astype(F32))

    y = _sigmoid(zmg_ref[:, 0:d].astype(F32)) * jnp.dot(pool_o.astype(BF16), wbp_ref[...],
                                                         preferred_element_type=F32)
    y += _sigmoid(zmg_ref[:, d:2 * d].astype(F32)) * jnp.dot(mlstm_o.astype(BF16), wbm_ref[...],
                                                             preferred_element_type=F32)
    y += _sigmoid(zmg_ref[:, 2 * d:3 * d].astype(F32)) * jnp.dot(sgu_o.astype(BF16), wbs_ref[...],
                                                                 preferred_element_type=F32)
    y2 = jnp.dot(y.astype(BF16), wo_ref[...], preferred_element_type=F32)

    g1 = mod_ref[:, 2 * d:3 * d]
    x1 = _ln_plain(alpha * x_ref[...] + g1 * y2) * l1g_ref[...] + l1b_ref[...]
    x1_ref[...] = x1
    h2 = _ln_plain(x1) * (1.0 + mod_ref[:, 4 * d:5 * d]) + mod_ref[:, 3 * d:4 * d]
    h2b = h2.astype(BF16)
    bits = lax.bitcast_convert_type(h2b.astype(F32), jnp.uint32)
    h2_ref[...] = (bits[:, d // 2:] & jnp.uint32(0xFFFF0000)) | (bits[:, :d // 2] >> 16)

    logits = jnp.dot(h2b, wr_ref[...], preferred_element_type=F32) + br_ref[...]
    lane = lax.broadcasted_iota(jnp.int32, logits.shape, 1)
    work = logits
    vals, idxs, hots = [], [], []
    for _ in range(TOP_K):
        mval = jnp.max(work, axis=-1, keepdims=True)
        idx = jnp.min(jnp.where(work == mval, lane, LANES), axis=-1, keepdims=True)
        hot = lane == idx
        vals.append(mval)
        idxs.append(idx)
        hots.append(hot)
        work = jnp.where(hot, -jnp.inf, work)
    exps = [jnp.exp(v - vals[0]) for v in vals]
    denom = exps[0] + exps[1] + exps[2] + exps[3]
    sel = (hots[0] | hots[1] | hots[2] | hots[3])
    r_i = lax.broadcasted_iota(jnp.int32, (tm, tm), 0)
    c_i = lax.broadcasted_iota(jnp.int32, (tm, tm), 1)
    before = (c_i < r_i).astype(BF16)
    sel_b = sel.astype(F32).astype(BF16)
    rank = jnp.dot(before, sel_b, preferred_element_type=F32) + cnt_sc[0:1, :]
    route = jnp.zeros(logits.shape, F32)
    for kk in range(TOP_K):
        rk = jnp.sum(jnp.where(hots[kk], rank, 0.0), axis=-1, keepdims=True)
        route = jnp.where(lane == kk, idxs[kk].astype(F32), route)
        route = jnp.where(lane == TOP_K + kk, exps[kk] / denom, route)
        route = jnp.where(lane == 2 * TOP_K + kk, rk, route)
    route_ref[...] = route
    new_cnt = cnt_sc[0:1, :] + jnp.sum(sel.astype(F32), axis=0, keepdims=True)
    cnt_sc[...] = jnp.broadcast_to(new_cnt, cnt_sc.shape)
    cnt_out_ref[...] = jnp.broadcast_to(new_cnt, cnt_out_ref.shape)


def _mix(x, zp, zuv, zmg, zo, hf, hb, mods, consts, *, n_rows, n_lat, seq, alpha):
    t, d = x.shape
    tm = ROW_TILE
    n_batch = mods.shape[0] - 1
    row = lambda i: (i, 0)
    const2 = lambda i: (0, 0)
    const3 = lambda i: (0, 0, 0)
    kind = lambda i: jnp.where(i * tm >= n_lat, 1, 0)

    def mod_idx(i):
        return (jnp.minimum((i * tm) // seq, n_batch), 0, 0)

    (pm, cnt, pw, ps, sg, sb, sw, sbm, ng, wbp, wbm, wbs, wo, l1g, l1b, wr, br) = consts
    in_specs = [
        pl.BlockSpec((tm, d), row),
        pl.BlockSpec((tm, zp.shape[1]), row),
        pl.BlockSpec((tm, zuv.shape[1]), row),
        pl.BlockSpec((tm, zmg.shape[1]), row),
        pl.BlockSpec((tm, zo.shape[1]), row),
        pl.BlockSpec((tm, hf.shape[1]), row),
        pl.BlockSpec((tm, hb.shape[1]), row),
        pl.BlockSpec((None, 1, mods.shape[2]), mod_idx),
        pl.BlockSpec((None,) + pm.shape[1:], lambda i: (kind(i), 0, 0, 0)),
        pl.BlockSpec((None,) + cnt.shape[1:], lambda i: (kind(i), 0, 0)),
        pl.BlockSpec(pw.shape, const2),
        pl.BlockSpec(ps.shape, const2),
        pl.BlockSpec(sg.shape, const2),
        pl.BlockSpec(sb.shape, const2),
        pl.BlockSpec(sw.shape, const3),
        pl.BlockSpec(sbm.shape, const2),
        pl.BlockSpec(ng.shape, const2),
        pl.BlockSpec(wbp.shape, const2),
        pl.BlockSpec(wbm.shape, const2),
        pl.BlockSpec(wbs.shape, const2),
        pl.BlockSpec(wo.shape, const2),
        pl.BlockSpec(l1g.shape, const2),
        pl.BlockSpec(l1b.shape, const2),
        pl.BlockSpec(wr.shape, const2),
        pl.BlockSpec(br.shape, const2),
    ]
    return pl.pallas_call(
        functools.partial(_mix_kernel, alpha=alpha),
        grid=(n_rows // tm,),
        in_specs=in_specs,
        out_specs=[
            pl.BlockSpec((tm, d), row),
            pl.BlockSpec((tm, d // 2), row),
            pl.BlockSpec((tm, LANES), row),
            pl.BlockSpec((8, LANES), const2),
        ],
        out_shape=[
            jax.ShapeDtypeStruct((n_rows, d), F32),
            jax.ShapeDtypeStruct((n_rows, d // 2), jnp.uint32),
            jax.ShapeDtypeStruct((n_rows, LANES), F32),
            jax.ShapeDtypeStruct((8, LANES), F32),
        ],
        scratch_shapes=[pltpu.VMEM((8, LANES), F32)],
        compiler_params=_cparams("arbitrary"),
    )(x, zp, zuv, zmg, zo, hf, hb, mods, pm, cnt, pw, ps, sg, sb, sw, sbm, ng,
      wbp, wbm, wbs, wo, l1g, l1b, wr, br)


SCATTER_WINDOW = 128
SCATTER_PARTS = 2


def _scatter_rows(x, idx, n_out, part):
    n = x.shape[0]
    w = x.shape[1] // SCATTER_PARTS
    n_idx = idx.shape[0]
    mesh = plsc.VectorSubcoreMesh(core_axis_name="core", subcore_axis_name="subcore")

    @pl.kernel(out_type=jax.ShapeDtypeStruct((n_out, w), x.dtype), mesh=mesh, scratch_types=[])
    def scatter_kernel(x_hbm, i_hbm, o_hbm):
        def body(x_vmem, i_vmem):
            for k in range(n_idx):
                pltpu.sync_copy(x_vmem, o_hbm.at[i_vmem.at[k]])

        pltpu.emit_pipeline(
            body,
            grid=(n // SCATTER_WINDOW,),
            in_specs=[pl.BlockSpec((SCATTER_WINDOW, w), lambda i: (i, part)),
                      pl.BlockSpec((n_idx, SCATTER_WINDOW), lambda i: (0, i))],
            out_specs=[],
            core_axis_name=("core", "subcore"),
            dimension_semantics=(pltpu.PARALLEL,),
        )(x_hbm, i_hbm)

    return scatter_kernel(x, idx)


def _expert_kernel(te_ref, nt_ref, xa_ref, xb_ref, wgu_ref, bgu_ref, wd_ref, bd_ref, ys_ref, wgu_sc, wd_sc):
    i = pl.program_id(0)
    f = wd_ref.shape[0]
    used = i < nt_ref[0]

    @pl.when(jnp.logical_and(used, jnp.logical_or(i == 0, te_ref[i] != te_ref[jnp.maximum(i - 1, 0)])))
    def _():
        for c0 in range(0, 2 * f, 512):
            wgu_sc[:, c0:c0 + 512] = wgu_ref[:, c0:c0 + 512].astype(BF16)
        for c0 in range(0, f, 256):
            wd_sc[c0:c0 + 256, :] = wd_ref[c0:c0 + 256, :].astype(BF16)

    @pl.when(used)
    def _():
        wa, wb = xa_ref[...], xb_ref[...]
        low = lambda wd: lax.bitcast_convert_type(wd << 16, F32)
        high = lambda wd: lax.bitcast_convert_type(wd & jnp.uint32(0xFFFF0000), F32)
        x = jnp.concatenate([low(wa), low(wb), high(wa), high(wb)], axis=1).astype(BF16)
        acc = jnp.zeros(ys_ref.shape, F32)
        for c0 in range(0, f, 512):
            gate = jnp.dot(x, wgu_sc[:, c0:c0 + 512], preferred_element_type=F32) + bgu_ref[:, c0:c0 + 512]
            up = (jnp.dot(x, wgu_sc[:, f + c0:f + c0 + 512], preferred_element_type=F32)
                  + bgu_ref[:, f + c0:f + c0 + 512])
            gate = jnp.minimum(gate, SWIGLU_LIMIT)
            up = jnp.clip(up, -SWIGLU_LIMIT, SWIGLU_LIMIT)
            glu = gate * _sigmoid(SWIGLU_ALPHA * gate)
            act = ((up + 1.0) * glu).astype(BF16)
            acc += jnp.dot(act, wd_sc[c0:c0 + 512, :], preferred_element_type=F32)
        ys_ref[...] = (acc + bd_ref[...]).astype(ys_ref.dtype)

    @pl.when(jnp.logical_not(used))
    def _():
        ys_ref[...] = jnp.zeros_like(ys_ref)


def _experts(tile_expert, n_tiles_used, xa, xb, wgu, bgu, wd, bd, *, layer):
    p = xa.shape[0]
    d = 2 * SCATTER_PARTS * xa.shape[1]
    tm = EXPERT_TILE
    f2 = wgu.shape[3]
    grid_spec = pltpu.PrefetchScalarGridSpec(
        num_scalar_prefetch=2,
        grid=(p // tm,),
        in_specs=[
            pl.BlockSpec((tm, xa.shape[1]), lambda i, te, nt: (i, 0)),
            pl.BlockSpec((tm, xb.shape[1]), lambda i, te, nt: (i, 0)),
            pl.BlockSpec((None, None, d, f2), lambda i, te, nt: (layer, te[i], 0, 0)),
            pl.BlockSpec((None, None, 1, f2), lambda i, te, nt: (layer, te[i], 0, 0)),
            pl.BlockSpec((None, None, f2 // 2, d), lambda i, te, nt: (layer, te[i], 0, 0)),
            pl.BlockSpec((None, None, 1, d), lambda i, te, nt: (layer, te[i], 0, 0)),
        ],
        out_specs=pl.BlockSpec((tm, d), lambda i, te, nt: (i, 0)),
        scratch_shapes=[pltpu.VMEM((d, f2), BF16), pltpu.VMEM((f2 // 2, d), BF16)],
    )
    return pl.pallas_call(
        _expert_kernel,
        grid_spec=grid_spec,
        out_shape=jax.ShapeDtypeStruct((p, d), BF16),
        compiler_params=_cparams("arbitrary"),
    )(tile_expert, n_tiles_used, xa, xb, wgu, bgu, wd, bd)


def _combine_kernel(x1_ref, yg_ref, route_ref, mod_ref, g_ref, b_ref, x2_ref, *, alpha):
    d = x1_ref.shape[1]
    route = route_ref[...]
    f = jnp.zeros(x1_ref.shape, F32)
    for kk in range(TOP_K):
        f += route[:, TOP_K + kk:TOP_K + kk + 1] * yg_ref[kk].astype(F32)
    g2 = mod_ref[:, 5 * d:6 * d]
    x2_ref[...] = _ln_plain(alpha * x1_ref[...] + g2 * f) * g_ref[...] + b_ref[...]


def _combine(x1, yg, route, mods, g, b, *, seq, alpha):
    t, d = x1.shape
    tm = ROW_TILE
    n_batch = mods.shape[0] - 1
    row = lambda i: (i, 0)

    def mod_idx(i):
        return (jnp.minimum((i * tm) // seq, n_batch), 0, 0)

    return pl.pallas_call(
        functools.partial(_combine_kernel, alpha=alpha),
        grid=(t // tm,),
        in_specs=[
            pl.BlockSpec((tm, d), row),
            pl.BlockSpec((TOP_K, tm, d), lambda i: (0, i, 0)),
            pl.BlockSpec((tm, LANES), row),
            pl.BlockSpec((None, 1, mods.shape[2]), mod_idx),
            pl.BlockSpec(g.shape, lambda i: (0, 0)),
            pl.BlockSpec(b.shape, lambda i: (0, 0)),
        ],
        out_specs=pl.BlockSpec((tm, d), row),
        out_shape=jax.ShapeDtypeStruct((t, d), F32),
        compiler_params=_cparams("parallel"),
    )(x1, yg, route, mods, g, b)


def kernel(x, c, ctx, c_ctx, w_ada, b_ada, w_in, b_in, pool_w, pool_scale, qk_conv_w, qk_conv_b,
           mlstm_norm_g, sgu_ln_g, sgu_ln_b, sgu_w, sgu_b, w_br_pool, w_br_mlstm, w_br_sgu, w_out,
           ln1_g, ln1_b, w_router, b_router, w_gate_up, b_gate_up, w_down, b_down, ln2_g, ln2_b):
    n_batch, seq, d = x.shape
    ctx_len = ctx.shape[1]
    depth = w_in.shape[0]
    n_exp = w_router.shape[2]
    n_lat = n_batch * seq
    t_all = n_lat + n_batch * ctx_len
    assert ctx_len == ROW_TILE and seq % ROW_TILE == 0 and ROW_TILE % GRID_W == 0
    alpha = float((2 * depth) ** 0.25)

    rows = ((n_batch + 1 + 7) // 8) * 8
    cc = jnp.zeros((rows, d), F32).at[:n_batch].set(c).at[n_batch].set(c_ctx)
    mods_all = _ada_table(cc, w_ada, b_ada)[:, :n_batch + 1].reshape(depth, n_batch + 1, 1, 6 * d)

    pool_w_, mlw = 256, 512
    col = lambda a, lo, hi: a[..., lo:hi]
    g_lo = pool_w_ + 4 * mlw
    parts = ((pool_w_, pool_w_ + 2 * mlw), (pool_w_ + 2 * mlw, pool_w_ + 3 * mlw), (pool_w_ + 3 * mlw, g_lo),
             (g_lo + _N_GATE_COLS, g_lo + _N_GATE_COLS + 512), (0, pool_w_),
             (g_lo + _N_GATE_COLS + 512, w_in.shape[2]))
    w_main = jnp.concatenate([col(w_in, lo, hi) for lo, hi in parts], axis=-1).astype(BF16)
    b_main = jnp.concatenate([col(b_in, lo, hi) for lo, hi in parts], axis=-1)[:, None, :]
    gw, gb = col(w_in, g_lo, g_lo + _N_GATE_COLS), col(b_in, g_lo, g_lo + _N_GATE_COLS)
    nh = N_HEADS
    pick = lambda a, k: jnp.concatenate([a[..., k * nh:(k + 1) * nh], a[..., (k + 2) * nh:(k + 3) * nh]], axis=-1)
    lane_pad = lambda a: jnp.pad(a, [(0, 0)] * (a.ndim - 1) + [(0, LANES - 2 * nh)])
    w_gate = jnp.concatenate([lane_pad(pick(gw, 0)), lane_pad(pick(gw, 1))], axis=-1).astype(BF16)
    b_gate = jnp.concatenate([lane_pad(pick(gb, 0)), lane_pad(pick(gb, 1))], axis=-1)[:, None, :]
    w_gate_t = jnp.swapaxes(jnp.concatenate([pick(gw, 0), pick(gw, 1)], axis=-1), 1, 2).astype(BF16)
    b_gate_t = jnp.concatenate([pick(gb, 0), pick(gb, 1)], axis=-1)[:, :, None]

    pm, cnt = _pool_constants(ctx_len)
    eye = jnp.eye(len(POOL_WINDOWS), dtype=F32)
    pool_bd = jnp.einsum('lgcd,gh->lgchd', pool_w, eye).reshape(depth, 256, 256).astype(BF16)
    sgu_bias_map = jnp.repeat(jnp.swapaxes(sgu_b, 1, 2), sgu_ln_g.shape[1] // sgu_b.shape[1], axis=2)
    w_router_p = jnp.pad(w_router, ((0, 0), (0, 0), (0, LANES - n_exp))).astype(BF16)
    b_router_p = jnp.pad(b_router, ((0, 0), (0, LANES - n_exp)), constant_values=NEG_BIG)[:, None, :]
    r1 = lambda a: a[:, None, :]
    b_gu4 = b_gate_up[:, :, None, :]
    b_d4 = b_down[:, :, None, :]

    stream = jnp.concatenate([x.reshape(n_lat, d), ctx.reshape(n_batch * ctx_len, d)], axis=0)
    for i in range(depth):
        last = i == depth - 1
        mods = mods_all[i]
        zqk, zv, zo, zuv, zp, zmg, g, gt = _inproj(
            stream, mods, w_main[i], b_main[i], w_gate[i], b_gate[i], w_gate_t[i], b_gate_t[i],
            qk_conv_w[i], r1(qk_conv_b)[i], n_lat=n_lat, seq=seq)
        hf, hb = _mlstm(zqk, zv, g, gt, n_batch=n_batch, seq=seq, ctx_len=ctx_len)
        n_rows = n_lat if last else t_all
        consts = (pm, cnt, pool_bd[i], r1(pool_scale)[i], r1(sgu_ln_g)[i], r1(sgu_ln_b)[i],
                  sgu_w[i].astype(BF16), sgu_bias_map[i], r1(mlstm_norm_g)[i],
                  w_br_pool[i].astype(BF16), w_br_mlstm[i].astype(BF16), w_br_sgu[i].astype(BF16),
                  w_out[i].astype(BF16), r1(ln1_g)[i], r1(ln1_b)[i], w_router_p[i], b_router_p[i])
        x1, h2, route, counts = _mix(stream, zp, zuv, zmg, zo, hf, hb, mods, consts,
                                     n_rows=n_rows, n_lat=n_lat, seq=seq, alpha=alpha)

        cnt_e = counts[0, :n_exp].astype(jnp.int32)
        padded = ((cnt_e + EXPERT_TILE - 1) // EXPERT_TILE) * EXPERT_TILE
        ends = jnp.cumsum(padded)
        starts = ends - padded
        e_idx = route[:, 0:TOP_K].astype(jnp.int32)
        rank = route[:, 2 * TOP_K:3 * TOP_K].astype(jnp.int32)
        pos_t = (starts[e_idx] + rank).T
        p_rows = n_rows * TOP_K + n_exp * EXPERT_TILE
        n_tiles = p_rows // EXPERT_TILE
        tile_start = jnp.arange(n_tiles, dtype=jnp.int32) * EXPERT_TILE
        tile_expert = jnp.minimum(jnp.sum((ends[None, :] <= tile_start[:, None]).astype(jnp.int32), axis=1),
                                  n_exp - 1)
        n_used = (ends[-1] // EXPERT_TILE).astype(jnp.int32).reshape(1)
        xs_a = _scatter_rows(h2, pos_t, p_rows, 0)
        xs_b = _scatter_rows(h2, pos_t, p_rows, 1)
        ys = _experts(tile_expert, n_used, xs_a, xs_b, w_gate_up, b_gu4, w_down, b_d4, layer=i)
        yg = ys.at[pos_t.reshape(-1)].get(mode="promise_in_bounds", unique_indices=True)
        yg = yg.reshape(TOP_K, n_rows, d)
        stream = _combine(x1, yg, route, mods, r1(ln2_g)[i], r1(ln2_b)[i], seq=seq, alpha=alpha)
    return stream[:n_lat].reshape(n_batch, seq, d)
```

```python
import functools

import numpy as np
import jax
import jax.numpy as jnp
from jax import lax
from jax.experimental import pallas as pl
from jax.experimental.pallas import tpu as pltpu
from jax.experimental.pallas import tpu_sc as plsc

F32 = jnp.float32
BF16 = jnp.bfloat16

GRID_W = 64
POOL_WINDOWS = (2, 4, 8, 16)
POOL_GROUP_DIM = 64
N_HEADS = 4
HEAD_DIM = 128
CHUNK = 128
TOP_K = 4
SWIGLU_LIMIT = 7.0
SWIGLU_ALPHA = 1.702
LN_EPS = 1e-5
HEAD_NORM_EPS = 1e-6

ROW_TILE = 256
EXPERT_TILE = 512
SUBLANES = 8
LANES = 128
NEG_BIG = -1e30
VMEM_LIMIT = 56 * 1024 * 1024


def _cparams(*sem):
    return pltpu.CompilerParams(dimension_semantics=sem, vmem_limit_bytes=VMEM_LIMIT)


def _ln_plain(x):
    mu = jnp.mean(x, axis=-1, keepdims=True)
    xc = x - mu
    var = jnp.mean(xc * xc, axis=-1, keepdims=True)
    return xc * lax.rsqrt(var + LN_EPS)


def _sigmoid(x):
    return 0.5 * jnp.tanh(0.5 * x) + 0.5


def _log_sigmoid(x):
    return jnp.minimum(x, 0.0) - jnp.log1p(jnp.exp(-jnp.abs(x)))


def _pack_rows(x):
    half = x.shape[1] // 2
    bits = lax.bitcast_convert_type(x.astype(BF16).astype(F32), jnp.uint32)
    return (bits[:, half:] & jnp.uint32(0xFFFF0000)) | (bits[:, :half] >> 16)


def _unpack_rows(parts):
    low = [lax.bitcast_convert_type(w << 16, F32) for w in parts]
    high = [lax.bitcast_convert_type(w & jnp.uint32(0xFFFF0000), F32) for w in parts]
    return jnp.concatenate(low + high, axis=1)


def _ada_kernel(c_ref, w_ref, b_ref, o_ref):
    c = c_ref[...]
    s = c * _sigmoid(c)
    o_ref[...] = jnp.dot(s, w_ref[...], preferred_element_type=F32,
                         precision=lax.Precision.HIGHEST) + b_ref[...]


def _ada_table(cc, w_ada, b_ada):
    depth, d, n = w_ada.shape
    rows = cc.shape[0]
    bn = 1536
    return pl.pallas_call(
        _ada_kernel,
        grid=(depth, n // bn),
        in_specs=[
            pl.BlockSpec((rows, d), lambda l, j: (0, 0)),
            pl.BlockSpec((None, d, bn), lambda l, j: (l, 0, j)),
            pl.BlockSpec((None, 1, bn), lambda l, j: (l, 0, j)),
        ],
        out_specs=pl.BlockSpec((None, rows, bn), lambda l, j: (l, 0, j)),
        out_shape=jax.ShapeDtypeStruct((depth, rows, n), F32),
        compiler_params=_cparams("parallel", "parallel"),
    )(cc, w_ada, b_ada.reshape(depth, 1, n))


_QK_WIDTH = 2 * N_HEADS * HEAD_DIM
_Z_PARTS = (("v", 512), ("o", 512), ("uv", 512), ("pool", 256), ("mg", 3072))
_N_GATE_COLS = 4 * N_HEADS


def _inproj_kernel(x_ref, xp_ref, xn_ref, mod_ref, w_ref, b_ref, wg_ref, bg_ref, wgt_ref, bgt_ref,
                   cw_ref, cb_ref, qk_ref, v_ref, o_ref, uv_ref, p_ref, mg_ref, g_ref, gt_ref, pad_sc,
                   *, n_lat, seq):
    tm, d = x_ref.shape
    i = pl.program_id(0)
    is_lat = i * tm < n_lat
    has_prev = jnp.logical_and(is_lat, (i * tm) % seq != 0).astype(F32)
    has_next = jnp.logical_and(is_lat, ((i + 1) * tm) % seq != 0).astype(F32)

    xe = jnp.concatenate([xp_ref[...], x_ref[...], xn_ref[...]], axis=0)
    he = (_ln_plain(xe) * (1.0 + mod_ref[:, d:2 * d]) + mod_ref[:, 0:d]).astype(BF16)
    hb = he[SUBLANES:SUBLANES + tm, :]

    half = _QK_WIDTH // 2
    for c0 in range(0, _QK_WIDTH, half):
        z = jnp.dot(he, w_ref[:, c0:c0 + half], preferred_element_type=F32) + b_ref[:, c0:c0 + half]
        pad_sc[...] = z
        pad_sc[SUBLANES - 1:SUBLANES, :] = pad_sc[SUBLANES - 1:SUBLANES, :] * has_prev
        pad_sc[SUBLANES + tm:SUBLANES + tm + 1, :] = pad_sc[SUBLANES + tm:SUBLANES + tm + 1, :] * has_next
        y = (cb_ref[:, c0:c0 + half]
             + pad_sc[SUBLANES - 1:SUBLANES - 1 + tm, :] * cw_ref[0:1, c0:c0 + half]
             + pad_sc[SUBLANES:SUBLANES + tm, :] * cw_ref[1:2, c0:c0 + half]
             + pad_sc[SUBLANES + 1:SUBLANES + 1 + tm, :] * cw_ref[2:3, c0:c0 + half])
        act = y * _sigmoid(y)
        if c0 >= half:
            act = act * (HEAD_DIM ** -0.5)
        qk_ref[:, c0:c0 + half] = act.astype(qk_ref.dtype)

    off = _QK_WIDTH
    for ref, (_, width) in zip((v_ref, o_ref, uv_ref, p_ref, mg_ref), _Z_PARTS):
        for c0 in range(0, width, 512):
            cw = min(512, width - c0)
            z = jnp.dot(hb, w_ref[:, off + c0:off + c0 + cw], preferred_element_type=F32)
            ref[:, c0:c0 + cw] = (z + b_ref[:, off + c0:off + c0 + cw]).astype(ref.dtype)
        off += width
    gg = jnp.dot(hb, wg_ref[...], preferred_element_type=F32) + bg_ref[...]
    gi, gf = gg[:, :LANES], gg[:, LANES:]
    lane = lax.broadcasted_iota(jnp.int32, (tm, LANES), 1)
    pos = lax.broadcasted_iota(jnp.int32, (tm, LANES), 0) % CHUNK
    fwd = lane < N_HEADS
    lsg = _log_sigmoid(gf)
    bcum = jnp.where(fwd, _chunk_scan(lsg, pos, 0, False, False), _chunk_scan(lsg, pos, 0, True, False))
    c = gi - bcum
    cmax = jnp.where(fwd, _chunk_scan(c, pos, 0, False, True), _chunk_scan(c, pos, 0, True, True))
    nd = 2 * N_HEADS
    g_ref[...] = jnp.where(lane < nd, bcum,
                           jnp.where(lane < 2 * nd, pltpu.roll(c, nd, axis=1),
                                     jnp.where(lane < 3 * nd, pltpu.roll(bcum + cmax, 2 * nd, axis=1),
                                               pltpu.roll(cmax, 3 * nd, axis=1))))
    ggt = lax.dot_general(wgt_ref[...], hb, (((1,), (1,)), ((), ())),
                          preferred_element_type=F32) + bgt_ref[...]
    git, gft = ggt[:nd, :], ggt[nd:, :]
    fwd_t = lax.broadcasted_iota(jnp.int32, (nd, tm), 0) < N_HEADS
    pos_t = lax.broadcasted_iota(jnp.int32, (nd, tm), 1) % CHUNK
    lsg_t = _log_sigmoid(gft)
    bcum_t = jnp.where(fwd_t, _chunk_scan(lsg_t, pos_t, 1, False, False), _chunk_scan(lsg_t, pos_t, 1, True, False))
    gt_ref[...] = git - bcum_t


def _chunk_scan(a, pos, axis, reverse, use_max):
    n = a.shape[axis]
    ident = -jnp.inf if use_max else 0.0
    sh = 1
    while sh < CHUNK:
        if reverse:
            moved = jnp.where(pos < CHUNK - sh, pltpu.roll(a, n - sh, axis=axis), ident)
        else:
            moved = jnp.where(pos >= sh, pltpu.roll(a, sh, axis=axis), ident)
        a = jnp.maximum(a, moved) if use_max else a + moved
        sh *= 2
    return a


def _inproj(x, mods, w, b, wg, bg, wgt, bgt, conv_w, conv_b, *, n_lat, seq):
    t, d = x.shape
    tm = ROW_TILE
    n_batch = mods.shape[0] - 1
    per_tile = tm // SUBLANES

    def mod_idx(i):
        return (jnp.minimum((i * tm) // seq, n_batch), 0, 0)

    const = lambda i: (0, 0)
    row = lambda i: (i, 0)
    parts = (("qk", _QK_WIDTH),) + _Z_PARTS
    out_shapes = [jax.ShapeDtypeStruct((t, width), BF16) for _, width in parts]
    out_specs = [pl.BlockSpec((tm, width), row) for _, width in parts]
    out_shapes += [jax.ShapeDtypeStruct((t, LANES), F32), jax.ShapeDtypeStruct((2 * N_HEADS, t), F32)]
    out_specs += [pl.BlockSpec((tm, LANES), row), pl.BlockSpec((2 * N_HEADS, tm), lambda i: (0, i))]
    return pl.pallas_call(
        functools.partial(_inproj_kernel, n_lat=n_lat, seq=seq),
        grid=(t // tm,),
        in_specs=[
            pl.BlockSpec((tm, d), row),
            pl.BlockSpec((SUBLANES, d), lambda i: (jnp.maximum(i * per_tile - 1, 0), 0)),
            pl.BlockSpec((SUBLANES, d), lambda i: (jnp.minimum((i + 1) * per_tile, t // SUBLANES - 1), 0)),
            pl.BlockSpec((None, 1, mods.shape[2]), mod_idx),
            pl.BlockSpec(w.shape, const),
            pl.BlockSpec(b.shape, const),
            pl.BlockSpec(wg.shape, const),
            pl.BlockSpec(bg.shape, const),
            pl.BlockSpec(wgt.shape, const),
            pl.BlockSpec(bgt.shape, const),
            pl.BlockSpec(conv_w.shape, const),
            pl.BlockSpec(conv_b.shape, const),
        ],
        out_specs=out_specs,
        out_shape=out_shapes,
        scratch_shapes=[pltpu.VMEM((tm + 2 * SUBLANES, _QK_WIDTH // 2), F32)],
        compiler_params=_cparams("parallel"),
    )(x, x, x, mods, w, b, wg, bg, wgt, bgt, conv_w, conv_b)


def _mlstm_row_block(j, b, direction, *, n_batch, seq, ctx_len):
    ncc = ctx_len // CHUNK
    ncl = seq // CHUNK
    is_ctx = j < ncc
    if direction == 0:
        c = jnp.where(is_ctx, j, j - ncc)
    else:
        c = jnp.where(is_ctx, ncc - 1 - j, ncl - 1 - (j - ncc))
    base = jnp.where(is_ctx, (n_batch * seq + b * ctx_len) // CHUNK, (b * seq) // CHUNK)
    return base + c


def _mlstm_kernel(q_f, k_f, v_f, g_f, gt_f, q_b, k_b, v_b, g_b, gt_b, hf_ref, hb_ref, s_sc, m_sc):
    j = pl.program_id(1)

    @pl.when(j == 0)
    def _():
        s_sc[...] = jnp.zeros_like(s_sc)
        m_sc[...] = jnp.zeros_like(m_sc)

    ri = lax.broadcasted_iota(jnp.int32, (CHUNK, CHUNK), 0)
    ci = lax.broadcasted_iota(jnp.int32, (CHUNK, CHUNK), 1)
    ones_tile = jnp.ones((CHUNK, HEAD_DIM), BF16)
    dirs = ((q_f, k_f, v_f, g_f, gt_f, hf_ref), (q_b, k_b, v_b, g_b, gt_b, hb_ref))
    combos = [(d, hh) for d in range(2) for hh in range(N_HEADS)]
    heads = lambda ref, hh: ref[:, hh * HEAD_DIM:(hh + 1) * HEAD_DIM]

    nd = 2 * N_HEADS
    tri = ((ci <= ri), (ci >= ri))
    g = (g_f[...], g_b[...])
    gt = (gt_f[...], gt_b[...])

    qk_s, v1, state, m_prev = {}, {}, {}, {}
    for d, hh in combos:
        q_ref, k_ref, v_ref = dirs[d][:3]
        qk_s[d, hh] = lax.dot_general(heads(q_ref, hh), heads(k_ref, hh), (((1,), (1,)), ((), ())),
                                      preferred_element_type=F32)
        v1[d, hh] = jnp.concatenate([heads(v_ref, hh), ones_tile], axis=1)
        state[d, hh] = s_sc[d, hh]
        m_prev[d, hh] = m_sc[d * N_HEADS + hh][0:1, 0:1]

    q_state = {}
    for d, hh in combos:
        q_state[d, hh] = jnp.dot(heads(dirs[d][0], hh), state[d, hh].astype(BF16), preferred_element_type=F32)

    m_t, decay, scores = {}, {}, {}
    for d, hh in combos:
        col = d * N_HEADS + hh
        bcum = g[d][:, col:col + 1]
        inter = bcum + m_prev[d, hh]
        m_t[d, hh] = jnp.maximum(inter, g[d][:, 2 * nd + col:2 * nd + col + 1])
        decay[d, hh] = jnp.exp(inter - m_t[d, hh])
        dlog = jnp.where(tri[d], (bcum - m_t[d, hh]) + gt[d][col:col + 1, :], -jnp.inf)
        scores[d, hh] = (qk_s[d, hh] * jnp.exp(dlog)).astype(BF16)

    for d, hh in combos:
        numden = decay[d, hh] * q_state[d, hh] + jnp.dot(scores[d, hh], v1[d, hh], preferred_element_type=F32)
        num = numden[:, :HEAD_DIM]
        den = numden[:, HEAD_DIM:]
        out = num / jnp.maximum(jnp.abs(den), jnp.exp(-m_t[d, hh]))
        out_ref = dirs[d][5]
        out_ref[:, hh * HEAD_DIM:(hh + 1) * HEAD_DIM] = out.astype(out_ref.dtype)

    kw, a, m_new = {}, {}, {}
    for d, hh in combos:
        col = d * N_HEADS + hh
        last = CHUNK - 1 if d == 0 else 0
        total = g[d][last:last + 1, col:col + 1]
        m_new[d, hh] = total + jnp.maximum(m_prev[d, hh], g[d][last:last + 1, 3 * nd + col:3 * nd + col + 1])
        a[d, hh] = jnp.exp(total + m_prev[d, hh] - m_new[d, hh])
        wk = jnp.exp(total + g[d][:, nd + col:nd + col + 1] - m_new[d, hh])
        kw[d, hh] = (heads(dirs[d][1], hh).astype(F32) * wk).astype(BF16)

    for d, hh in combos:
        upd = lax.dot_general(kw[d, hh], v1[d, hh], (((0,), (0,)), ((), ())), preferred_element_type=F32)
        s_sc[d, hh] = a[d, hh] * state[d, hh] + upd
        m_sc[d * N_HEADS + hh] = jnp.broadcast_to(m_new[d, hh], m_sc.shape[1:])


def _mlstm(zqk, zv, g, gt, *, n_batch, seq, ctx_len):
    t = zqk.shape[0]
    nsteps = (seq + ctx_len) // CHUNK
    width = N_HEADS * HEAD_DIM
    idx = functools.partial(_mlstm_row_block, n_batch=n_batch, seq=seq, ctx_len=ctx_len)

    def specs(d):
        rb = lambda b, j: idx(j, b, d)
        return [
            pl.BlockSpec((CHUNK, width), lambda b, j: (rb(b, j), 0)),
            pl.BlockSpec((CHUNK, width), lambda b, j: (rb(b, j), 1)),
            pl.BlockSpec((CHUNK, width), lambda b, j: (rb(b, j), 0)),
            pl.BlockSpec((CHUNK, LANES), lambda b, j: (rb(b, j), 0)),
            pl.BlockSpec((2 * N_HEADS, CHUNK), lambda b, j: (0, rb(b, j))),
        ]

    out_spec = lambda d: pl.BlockSpec((CHUNK, width), lambda b, j: (idx(j, b, d), 0))
    return pl.pallas_call(
        _mlstm_kernel,
        grid=(n_batch, nsteps),
        in_specs=specs(0) + specs(1),
        out_specs=[out_spec(0), out_spec(1)],
        out_shape=[jax.ShapeDtypeStruct((t, width), BF16)] * 2,
        scratch_shapes=[
            pltpu.VMEM((2, N_HEADS, HEAD_DIM, 2 * HEAD_DIM), F32),
            pltpu.VMEM((2 * N_HEADS, 8, LANES), F32),
        ],
        compiler_params=_cparams("parallel", "arbitrary"),
    )(zqk, zqk, zv, g, gt, zqk, zqk, zv, g, gt)


def _pool_constants(ctx_len):
    tm = ROW_TILE
    masks = np.zeros((2, len(POOL_WINDOWS), tm, tm), np.float32)
    cnts = np.zeros((2, tm, len(POOL_WINDOWS) * POOL_GROUP_DIM), np.float32)
    for kind, length in enumerate((GRID_W, ctx_len)):
        for gi, w in enumerate(POOL_WINDOWS):
            for t in range(tm):
                base = (t // length) * length
                tl = t - base
                lo = min(max(tl - w // 2, 0), length)
                hi = min(max(tl + w // 2, 0), length)
                masks[kind, gi, t, base + lo:base + hi] = 1.0
                cnts[kind, t, gi * POOL_GROUP_DIM:(gi + 1) * POOL_GROUP_DIM] = hi - lo
    return jnp.asarray(masks, BF16), jnp.asarray(cnts, F32)


def _mix_kernel(x_ref, zp_ref, zuv_ref, zmg_ref, zo_ref, hf_ref, hb_ref, mod_ref,
                pm_ref, cnt_ref, pw_ref, ps_ref, sg_ref, sb_ref, sw_ref, sbm_ref, ng_ref,
                wbp_ref, wbm_ref, wbs_ref, wo_ref, l1g_ref, l1b_ref, wr_ref, br_ref,
                x1_ref, h2_ref, route_ref, cnt_out_ref, cnt_sc, *, alpha):
    tm, d = x_ref.shape
    i = pl.program_id(0)

    @pl.when(i == 0)
    def _():
        cnt_sc[...] = jnp.zeros_like(cnt_sc)

    lane256 = lax.broadcasted_iota(jnp.int32, (1, 4 * POOL_GROUP_DIM), 1) // POOL_GROUP_DIM

    xp = zp_ref[...]
    xp_f = xp.astype(F32)
    pooled = jnp.zeros(xp_f.shape, F32)
    for gi in range(len(POOL_WINDOWS)):
        s = jnp.dot(pm_ref[gi], xp, preferred_element_type=F32)
        pooled = jnp.where(lane256 == gi, s, pooled)
    diff = pooled / cnt_ref[...] - xp_f
    pool_o = jnp.dot(diff.astype(BF16), pw_ref[...], preferred_element_type=F32) * ps_ref[...]

    uv = zuv_ref[...].astype(F32)
    uv = 0.5 * uv * (1.0 + lax.erf(uv * (2.0 ** -0.5)))
    half = uv.shape[1] // 2
    u = uv[:, :half]
    vn = (_ln_plain(uv[:, half:]) * sg_ref[...] + sb_ref[...]).astype(BF16)
    mixed_chunks = []
    for c0 in range(0, tm, CHUNK):
        vc = vn[c0:c0 + CHUNK, :]
        mixed = jnp.zeros((CHUNK, half), F32)
        for gi in range(sw_ref.shape[0]):
            s = jnp.dot(sw_ref[gi], vc, preferred_element_type=F32)
            mixed = jnp.where(lane256 == gi, s, mixed)
        mixed_chunks.append(mixed + sbm_ref[...])
    sgu_o = u * jnp.concatenate(mixed_chunks, axis=0)

    hsum = hf_ref[...].astype(F32) + hb_ref[...].astype(F32)
    heads = []
    for hh in range(N_HEADS):
        hv = hsum[:, hh * HEAD_DIM:(hh + 1) * HEAD_DIM]
        mu = jnp.mean(hv, axis=-1, keepdims=True)
        hc = hv - mu
        var = jnp.mean(hc * hc, axis=-1, keepdims=True)
        heads.append(hc * lax.rsqrt(var + HEAD_NORM_EPS))
    hn = jnp.concatenate(heads, axis=1)
    mlstm_o = hn * ng_ref[...] * _sigmoid(zo_ref[...].astype(F32))

    y = _sigmoid(zmg_ref[:, 0:d].astype(F32)) * jnp.dot(pool_o.astype(BF16), wbp_ref[...],
                                                         preferred_element_type=F32)
    y += _sigmoid(zmg_ref[:, d:2 * d].astype(F32)) * jnp.dot(mlstm_o.astype(BF16), wbm_ref[...],
                                                             preferred_element_type=F32)
    y += _sigmoid(zmg_ref[:, 2 * d:3 * d].astype(F32)) * jnp.dot(sgu_o.astype(BF16), wbs_ref[...],
                                                                 preferred_element_type=F32)
    y2 = jnp.dot(y.astype(BF16), wo_ref[...], preferred_element_type=F32)

    g1 = mod_ref[:, 2 * d:3 * d]
    x1 = _ln_plain(alpha * x_ref[...] + g1 * y2) * l1g_ref[...] + l1b_ref[...]
    x1_ref[...] = x1
    h2 = _ln_plain(x1) * (1.0 + mod_ref[:, 4 * d:5 * d]) + mod_ref[:, 3 * d:4 * d]
    h2b = h2.astype(BF16)
    h2_ref[...] = _pack_rows(h2b)

    logits = jnp.dot(h2b, wr_ref[...], preferred_element_type=F32) + br_ref[...]
    lane = lax.broadcasted_iota(jnp.int32, logits.shape, 1)
    work = logits
    vals, idxs, hots = [], [], []
    for _ in range(TOP_K):
        mval = jnp.max(work, axis=-1, keepdims=True)
        idx = jnp.min(jnp.where(work == mval, lane, LANES), axis=-1, keepdims=True)
        hot = lane == idx
        vals.append(mval)
        idxs.append(idx)
        hots.append(hot)
        work = jnp.where(hot, -jnp.inf, work)
    exps = [jnp.exp(v - vals[0]) for v in vals]
    denom = exps[0] + exps[1] + exps[2] + exps[3]
    sel = (hots[0] | hots[1] | hots[2] | hots[3])
    r_i = lax.broadcasted_iota(jnp.int32, (tm, tm), 0)
    c_i = lax.broadcasted_iota(jnp.int32, (tm, tm), 1)
    before = (c_i < r_i).astype(BF16)
    sel_b = sel.astype(F32).astype(BF16)
    rank = jnp.dot(before, sel_b, preferred_element_type=F32) + cnt_sc[0:1, :]
    route = jnp.zeros(logits.shape, F32)
    for kk in range(TOP_K):
        rk = jnp.sum(jnp.where(hots[kk], rank, 0.0), axis=-1, keepdims=True)
        route = jnp.where(lane == kk, idxs[kk].astype(F32), route)
        route = jnp.where(lane == TOP_K + kk, exps[kk] / denom, route)
        route = jnp.where(lane == 2 * TOP_K + kk, rk, route)
    route_ref[...] = route
    new_cnt = cnt_sc[0:1, :] + jnp.sum(sel.astype(F32), axis=0, keepdims=True)
    cnt_sc[...] = jnp.broadcast_to(new_cnt, cnt_sc.shape)
    cnt_out_ref[...] = jnp.broadcast_to(new_cnt, cnt_out_ref.shape)


def _mix(x, zp, zuv, zmg, zo, hf, hb, mods, consts, *, n_rows, n_lat, seq, alpha):
    t, d = x.shape
    tm = ROW_TILE
    n_batch = mods.shape[0] - 1
    row = lambda i: (i, 0)
    const2 = lambda i: (0, 0)
    const3 = lambda i: (0, 0, 0)
    kind = lambda i: jnp.where(i * tm >= n_lat, 1, 0)

    def mod_idx(i):
        return (jnp.minimum((i * tm) // seq, n_batch), 0, 0)

    (pm, cnt, pw, ps, sg, sb, sw, sbm, ng, wbp, wbm, wbs, wo, l1g, l1b, wr, br) = consts
    in_specs = [
        pl.BlockSpec((tm, d), row),
        pl.BlockSpec((tm, zp.shape[1]), row),
        pl.BlockSpec((tm, zuv.shape[1]), row),
        pl.BlockSpec((tm, zmg.shape[1]), row),
        pl.BlockSpec((tm, zo.shape[1]), row),
        pl.BlockSpec((tm, hf.shape[1]), row),
        pl.BlockSpec((tm, hb.shape[1]), row),
        pl.BlockSpec((None, 1, mods.shape[2]), mod_idx),
        pl.BlockSpec((None,) + pm.shape[1:], lambda i: (kind(i), 0, 0, 0)),
        pl.BlockSpec((None,) + cnt.shape[1:], lambda i: (kind(i), 0, 0)),
        pl.BlockSpec(pw.shape, const2),
        pl.BlockSpec(ps.shape, const2),
        pl.BlockSpec(sg.shape, const2),
        pl.BlockSpec(sb.shape, const2),
        pl.BlockSpec(sw.shape, const3),
        pl.BlockSpec(sbm.shape, const2),
        pl.BlockSpec(ng.shape, const2),
        pl.BlockSpec(wbp.shape, const2),
        pl.BlockSpec(wbm.shape, const2),
        pl.BlockSpec(wbs.shape, const2),
        pl.BlockSpec(wo.shape, const2),
        pl.BlockSpec(l1g.shape, const2),
        pl.BlockSpec(l1b.shape, const2),
        pl.BlockSpec(wr.shape, const2),
        pl.BlockSpec(br.shape, const2),
    ]
    return pl.pallas_call(
        functools.partial(_mix_kernel, alpha=alpha),
        grid=(n_rows // tm,),
        in_specs=in_specs,
        out_specs=[
            pl.BlockSpec((tm, d), row),
            pl.BlockSpec((tm, d // 2), row),
            pl.BlockSpec((tm, LANES), row),
            pl.BlockSpec((8, LANES), const2),
        ],
        out_shape=[
            jax.ShapeDtypeStruct((n_rows, d), F32),
            jax.ShapeDtypeStruct((n_rows, d // 2), jnp.uint32),
            jax.ShapeDtypeStruct((n_rows, LANES), F32),
            jax.ShapeDtypeStruct((8, LANES), F32),
        ],
        scratch_shapes=[pltpu.VMEM((8, LANES), F32)],
        compiler_params=_cparams("arbitrary"),
    )(x, zp, zuv, zmg, zo, hf, hb, mods, pm, cnt, pw, ps, sg, sb, sw, sbm, ng,
      wbp, wbm, wbs, wo, l1g, l1b, wr, br)


SCATTER_WINDOW = 128
SCATTER_PARTS = 2


def _scatter_rows(x, idx, n_out, part):
    n = x.shape[0]
    w = x.shape[1] // SCATTER_PARTS
    n_idx = idx.shape[0]
    mesh = plsc.VectorSubcoreMesh(core_axis_name="core", subcore_axis_name="subcore")

    @pl.kernel(out_type=jax.ShapeDtypeStruct((n_out, w), x.dtype), mesh=mesh, scratch_types=[])
    def scatter_kernel(x_hbm, i_hbm, o_hbm):
        def body(x_vmem, i_vmem):
            for k in range(n_idx):
                pltpu.sync_copy(x_vmem, o_hbm.at[i_vmem.at[k]])

        pltpu.emit_pipeline(
            body,
            grid=(n // SCATTER_WINDOW,),
            in_specs=[pl.BlockSpec((SCATTER_WINDOW, w), lambda i: (i, part)),
                      pl.BlockSpec((n_idx, SCATTER_WINDOW), lambda i: (0, i))],
            out_specs=[],
            core_axis_name=("core", "subcore"),
            dimension_semantics=(pltpu.PARALLEL,),
        )(x_hbm, i_hbm)

    return scatter_kernel(x, idx)


def _gather_rows(x, idx):
    w = x.shape[1]
    n_idx = idx.shape[1]
    mesh = plsc.VectorSubcoreMesh(core_axis_name="core", subcore_axis_name="subcore")

    @pl.kernel(out_type=jax.ShapeDtypeStruct((n_idx, w), x.dtype), mesh=mesh, scratch_types=[])
    def gather_kernel(x_hbm, i_hbm, o_hbm):
        def body(i_vmem, o_vmem):
            pltpu.sync_copy(x_hbm.at[i_vmem.at[0]], o_vmem)

        pltpu.emit_pipeline(
            body,
            grid=(n_idx // SCATTER_WINDOW,),
            in_specs=[pl.BlockSpec((1, SCATTER_WINDOW), lambda i: (0, i))],
            out_specs=[pl.BlockSpec((SCATTER_WINDOW, w), lambda i: (i, 0))],
            core_axis_name=("core", "subcore"),
            dimension_semantics=(pltpu.PARALLEL,),
        )(i_hbm, o_hbm)

    return gather_kernel(x, idx)


def _expert_kernel(te_ref, nt_ref, xa_ref, xb_ref, wgu_ref, bgu_ref, wd_ref, bd_ref, ya_ref, yb_ref,
                   wgu_sc, wd_sc):
    i = pl.program_id(0)
    f = wd_ref.shape[0]
    used = i < nt_ref[0]

    @pl.when(jnp.logical_and(used, jnp.logical_or(i == 0, te_ref[i] != te_ref[jnp.maximum(i - 1, 0)])))
    def _():
        for c0 in range(0, 2 * f, 512):
            wgu_sc[:, c0:c0 + 512] = wgu_ref[:, c0:c0 + 512].astype(BF16)
        for c0 in range(0, f, 256):
            wd_sc[c0:c0 + 256, :] = wd_ref[c0:c0 + 256, :].astype(BF16)

    @pl.when(used)
    def _():
        x = _unpack_rows([xa_ref[...], xb_ref[...]]).astype(BF16)
        acc = jnp.zeros((x.shape[0], wd_ref.shape[1]), F32)
        for c0 in range(0, f, 512):
            gate = jnp.dot(x, wgu_sc[:, c0:c0 + 512], preferred_element_type=F32) + bgu_ref[:, c0:c0 + 512]
            up = (jnp.dot(x, wgu_sc[:, f + c0:f + c0 + 512], preferred_element_type=F32)
                  + bgu_ref[:, f + c0:f + c0 + 512])
            gate = jnp.minimum(gate, SWIGLU_LIMIT)
            up = jnp.clip(up, -SWIGLU_LIMIT, SWIGLU_LIMIT)
            glu = gate * _sigmoid(SWIGLU_ALPHA * gate)
            act = ((up + 1.0) * glu).astype(BF16)
            acc += jnp.dot(act, wd_sc[c0:c0 + 512, :], preferred_element_type=F32)
        words = _pack_rows(acc + bd_ref[...])
        ya_ref[...] = words[:, :ya_ref.shape[1]]
        yb_ref[...] = words[:, ya_ref.shape[1]:]

    @pl.when(jnp.logical_not(used))
    def _():
        ya_ref[...] = jnp.zeros_like(ya_ref)
        yb_ref[...] = jnp.zeros_like(yb_ref)


def _experts(tile_expert, n_tiles_used, xa, xb, wgu, bgu, wd, bd, *, layer):
    p = xa.shape[0]
    d = 2 * SCATTER_PARTS * xa.shape[1]
    tm = EXPERT_TILE
    f2 = wgu.shape[3]
    grid_spec = pltpu.PrefetchScalarGridSpec(
        num_scalar_prefetch=2,
        grid=(p // tm,),
        in_specs=[
            pl.BlockSpec((tm, xa.shape[1]), lambda i, te, nt: (i, 0)),
            pl.BlockSpec((tm, xb.shape[1]), lambda i, te, nt: (i, 0)),
            pl.BlockSpec((None, None, d, f2), lambda i, te, nt: (layer, te[i], 0, 0)),
            pl.BlockSpec((None, None, 1, f2), lambda i, te, nt: (layer, te[i], 0, 0)),
            pl.BlockSpec((None, None, f2 // 2, d), lambda i, te, nt: (layer, te[i], 0, 0)),
            pl.BlockSpec((None, None, 1, d), lambda i, te, nt: (layer, te[i], 0, 0)),
        ],
        out_specs=[pl.BlockSpec((tm, xa.shape[1]), lambda i, te, nt: (i, 0)),
                   pl.BlockSpec((tm, xb.shape[1]), lambda i, te, nt: (i, 0))],
        scratch_shapes=[pltpu.VMEM((d, f2), BF16), pltpu.VMEM((f2 // 2, d), BF16)],
    )
    return pl.pallas_call(
        _expert_kernel,
        grid_spec=grid_spec,
        out_shape=[jax.ShapeDtypeStruct(xa.shape, xa.dtype), jax.ShapeDtypeStruct(xb.shape, xb.dtype)],
        compiler_params=_cparams("arbitrary"),
    )(tile_expert, n_tiles_used, xa, xb, wgu, bgu, wd, bd)


def _combine_kernel(x1_ref, ya_ref, yb_ref, route_ref, mod_ref, g_ref, b_ref, x2_ref, *, alpha):
    d = x1_ref.shape[1]
    route = route_ref[...]
    f = jnp.zeros(x1_ref.shape, F32)
    for kk in range(TOP_K):
        f += route[:, TOP_K + kk:TOP_K + kk + 1] * _unpack_rows([ya_ref[kk], yb_ref[kk]])
    g2 = mod_ref[:, 5 * d:6 * d]
    x2_ref[...] = _ln_plain(alpha * x1_ref[...] + g2 * f) * g_ref[...] + b_ref[...]


def _combine(x1, yg_a, yg_b, route, mods, g, b, *, seq, alpha):
    t, d = x1.shape
    tm = ROW_TILE
    n_batch = mods.shape[0] - 1
    row = lambda i: (i, 0)

    def mod_idx(i):
        return (jnp.minimum((i * tm) // seq, n_batch), 0, 0)

    return pl.pallas_call(
        functools.partial(_combine_kernel, alpha=alpha),
        grid=(t // tm,),
        in_specs=[
            pl.BlockSpec((tm, d), row),
            pl.BlockSpec((TOP_K, tm, yg_a.shape[2]), lambda i: (0, i, 0)),
            pl.BlockSpec((TOP_K, tm, yg_b.shape[2]), lambda i: (0, i, 0)),
            pl.BlockSpec((tm, LANES), row),
            pl.BlockSpec((None, 1, mods.shape[2]), mod_idx),
            pl.BlockSpec(g.shape, lambda i: (0, 0)),
            pl.BlockSpec(b.shape, lambda i: (0, 0)),
        ],
        out_specs=pl.BlockSpec((tm, d), row),
        out_shape=jax.ShapeDtypeStruct((t, d), F32),
        compiler_params=_cparams("parallel"),
    )(x1, yg_a, yg_b, route, mods, g, b)


def kernel(x, c, ctx, c_ctx, w_ada, b_ada, w_in, b_in, pool_w, pool_scale, qk_conv_w, qk_conv_b,
           mlstm_norm_g, sgu_ln_g, sgu_ln_b, sgu_w, sgu_b, w_br_pool, w_br_mlstm, w_br_sgu, w_out,
           ln1_g, ln1_b, w_router, b_router, w_gate_up, b_gate_up, w_down, b_down, ln2_g, ln2_b):
    n_batch, seq, d = x.shape
    ctx_len = ctx.shape[1]
    depth = w_in.shape[0]
    n_exp = w_router.shape[2]
    n_lat = n_batch * seq
    t_all = n_lat + n_batch * ctx_len
    assert ctx_len == ROW_TILE and seq % ROW_TILE == 0 and ROW_TILE % GRID_W == 0
    alpha = float((2 * depth) ** 0.25)

    rows = ((n_batch + 1 + 7) // 8) * 8
    cc = jnp.zeros((rows, d), F32).at[:n_batch].set(c).at[n_batch].set(c_ctx)
    mods_all = _ada_table(cc, w_ada, b_ada)[:, :n_batch + 1].reshape(depth, n_batch + 1, 1, 6 * d)

    pool_w_, mlw = 256, 512
    col = lambda a, lo, hi: a[..., lo:hi]
    g_lo = pool_w_ + 4 * mlw
    parts = ((pool_w_, pool_w_ + 2 * mlw), (pool_w_ + 2 * mlw, pool_w_ + 3 * mlw), (pool_w_ + 3 * mlw, g_lo),
             (g_lo + _N_GATE_COLS, g_lo + _N_GATE_COLS + 512), (0, pool_w_),
             (g_lo + _N_GATE_COLS + 512, w_in.shape[2]))
    w_main = jnp.concatenate([col(w_in, lo, hi) for lo, hi in parts], axis=-1).astype(BF16)
    b_main = jnp.concatenate([col(b_in, lo, hi) for lo, hi in parts], axis=-1)[:, None, :]
    gw, gb = col(w_in, g_lo, g_lo + _N_GATE_COLS), col(b_in, g_lo, g_lo + _N_GATE_COLS)
    nh = N_HEADS
    pick = lambda a, k: jnp.concatenate([a[..., k * nh:(k + 1) * nh], a[..., (k + 2) * nh:(k + 3) * nh]], axis=-1)
    lane_pad = lambda a: jnp.pad(a, [(0, 0)] * (a.ndim - 1) + [(0, LANES - 2 * nh)])
    w_gate = jnp.concatenate([lane_pad(pick(gw, 0)), lane_pad(pick(gw, 1))], axis=-1).astype(BF16)
    b_gate = jnp.concatenate([lane_pad(pick(gb, 0)), lane_pad(pick(gb, 1))], axis=-1)[:, None, :]
    w_gate_t = jnp.swapaxes(jnp.concatenate([pick(gw, 0), pick(gw, 1)], axis=-1), 1, 2).astype(BF16)
    b_gate_t = jnp.concatenate([pick(gb, 0), pick(gb, 1)], axis=-1)[:, :, None]

    pm, cnt = _pool_constants(ctx_len)
    eye = jnp.eye(len(POOL_WINDOWS), dtype=F32)
    pool_bd = jnp.einsum('lgcd,gh->lgchd', pool_w, eye).reshape(depth, 256, 256).astype(BF16)
    sgu_bias_map = jnp.repeat(jnp.swapaxes(sgu_b, 1, 2), sgu_ln_g.shape[1] // sgu_b.shape[1], axis=2)
    w_router_p = jnp.pad(w_router, ((0, 0), (0, 0), (0, LANES - n_exp))).astype(BF16)
    b_router_p = jnp.pad(b_router, ((0, 0), (0, LANES - n_exp)), constant_values=NEG_BIG)[:, None, :]
    r1 = lambda a: a[:, None, :]
    b_gu4 = b_gate_up[:, :, None, :]
    b_d4 = b_down[:, :, None, :]

    stream = jnp.concatenate([x.reshape(n_lat, d), ctx.reshape(n_batch * ctx_len, d)], axis=0)
    for i in range(depth):
        last = i == depth - 1
        mods = mods_all[i]
        zqk, zv, zo, zuv, zp, zmg, g, gt = _inproj(
            stream, mods, w_main[i], b_main[i], w_gate[i], b_gate[i], w_gate_t[i], b_gate_t[i],
            qk_conv_w[i], r1(qk_conv_b)[i], n_lat=n_lat, seq=seq)
        hf, hb = _mlstm(zqk, zv, g, gt, n_batch=n_batch, seq=seq, ctx_len=ctx_len)
        n_rows = n_lat if last else t_all
        consts = (pm, cnt, pool_bd[i], r1(pool_scale)[i], r1(sgu_ln_g)[i], r1(sgu_ln_b)[i],
                  sgu_w[i].astype(BF16), sgu_bias_map[i], r1(mlstm_norm_g)[i],
                  w_br_pool[i].astype(BF16), w_br_mlstm[i].astype(BF16), w_br_sgu[i].astype(BF16),
                  w_out[i].astype(BF16), r1(ln1_g)[i], r1(ln1_b)[i], w_router_p[i], b_router_p[i])
        x1, h2, route, counts = _mix(stream, zp, zuv, zmg, zo, hf, hb, mods, consts,
                                     n_rows=n_rows, n_lat=n_lat, seq=seq, alpha=alpha)

        cnt_e = counts[0, :n_exp].astype(jnp.int32)
        padded = ((cnt_e + EXPERT_TILE - 1) // EXPERT_TILE) * EXPERT_TILE
        ends = jnp.cumsum(padded)
        starts = ends - padded
        e_idx = route[:, 0:TOP_K].astype(jnp.int32)
        rank = route[:, 2 * TOP_K:3 * TOP_K].astype(jnp.int32)
        pos_t = (starts[e_idx] + rank).T
        p_rows = n_rows * TOP_K + n_exp * EXPERT_TILE
        n_tiles = p_rows // EXPERT_TILE
        tile_start = jnp.arange(n_tiles, dtype=jnp.int32) * EXPERT_TILE
        tile_expert = jnp.minimum(jnp.sum((ends[None, :] <= tile_start[:, None]).astype(jnp.int32), axis=1),
                                  n_exp - 1)
        n_used = (ends[-1] // EXPERT_TILE).astype(jnp.int32).reshape(1)
        xs_a = _scatter_rows(h2, pos_t, p_rows, 0)
        xs_b = _scatter_rows(h2, pos_t, p_rows, 1)
        ys_a, ys_b = _experts(tile_expert, n_used, xs_a, xs_b, w_gate_up, b_gu4, w_down, b_d4, layer=i)
        pos_flat = pos_t.reshape(1, TOP_K * n_rows)
        yg_a = _gather_rows(ys_a, pos_flat).reshape(TOP_K, n_rows, -1)
        yg_b = _gather_rows(ys_b, pos_flat).reshape(TOP_K, n_rows, -1)
        stream = _combine(x1, yg_a, yg_b, route, mods, r1(ln2_g)[i], r1(ln2_b)[i], seq=seq, alpha=alpha)
    return stream[:n_lat].reshape(n_batch, seq, d)
```

```python
import functools

import numpy as np
import jax
import jax.numpy as jnp
from jax import lax
from jax.experimental import pallas as pl
from jax.experimental.pallas import tpu as pltpu
from jax.experimental.pallas import tpu_sc as plsc

F32 = jnp.float32
BF16 = jnp.bfloat16

GRID_W = 64
POOL_WINDOWS = (2, 4, 8, 16)
POOL_GROUP_DIM = 64
N_HEADS = 4
HEAD_DIM = 128
CHUNK = 128
TOP_K = 4
SWIGLU_LIMIT = 7.0
SWIGLU_ALPHA = 1.702
LN_EPS = 1e-5
HEAD_NORM_EPS = 1e-6

ROW_TILE = 256
EXPERT_TILE = 512
N_STREAMS = 2
SUBLANES = 8
LANES = 128
NEG_BIG = -1e30
VMEM_LIMIT = 56 * 1024 * 1024


def _cparams(*sem):
    return pltpu.CompilerParams(dimension_semantics=sem, vmem_limit_bytes=VMEM_LIMIT)


def _ln_plain(x):
    mu = jnp.mean(x, axis=-1, keepdims=True)
    xc = x - mu
    var = jnp.mean(xc * xc, axis=-1, keepdims=True)
    return xc * lax.rsqrt(var + LN_EPS)


def _sigmoid(x):
    return 0.5 * jnp.tanh(0.5 * x) + 0.5


def _log_sigmoid(x):
    return jnp.minimum(x, 0.0) - jnp.log1p(jnp.exp(-jnp.abs(x)))


def _pack_rows(x):
    half = x.shape[1] // 2
    bits = lax.bitcast_convert_type(x.astype(BF16).astype(F32), jnp.uint32)
    return (bits[:, half:] & jnp.uint32(0xFFFF0000)) | (bits[:, :half] >> 16)


def _unpack_rows(parts):
    low = [lax.bitcast_convert_type(w << 16, F32) for w in parts]
    high = [lax.bitcast_convert_type(w & jnp.uint32(0xFFFF0000), F32) for w in parts]
    return jnp.concatenate(low + high, axis=1)


def _ada_kernel(c_ref, w_ref, b_ref, o_ref):
    c = c_ref[...]
    s = c * _sigmoid(c)
    o_ref[...] = jnp.dot(s, w_ref[...], preferred_element_type=F32,
                         precision=lax.Precision.HIGHEST) + b_ref[...]


def _ada_table(cc, w_ada, b_ada):
    depth, d, n = w_ada.shape
    rows = cc.shape[0]
    bn = 1536
    return pl.pallas_call(
        _ada_kernel,
        grid=(depth, n // bn),
        in_specs=[
            pl.BlockSpec((rows, d), lambda l, j: (0, 0)),
            pl.BlockSpec((None, d, bn), lambda l, j: (l, 0, j)),
            pl.BlockSpec((None, 1, bn), lambda l, j: (l, 0, j)),
        ],
        out_specs=pl.BlockSpec((None, rows, bn), lambda l, j: (l, 0, j)),
        out_shape=jax.ShapeDtypeStruct((depth, rows, n), F32),
        compiler_params=_cparams("parallel", "parallel"),
    )(cc, w_ada, b_ada.reshape(depth, 1, n))


_QK_WIDTH = 2 * N_HEADS * HEAD_DIM
_Z_PARTS = (("v", 512), ("o", 512), ("uv", 512), ("pool", 256), ("mg", 3072))
_N_GATE_COLS = 4 * N_HEADS


def _inproj_kernel(x_ref, xp_ref, xn_ref, mod_ref, w_ref, b_ref, wgt_ref, bgt_ref,
                   cw_ref, cb_ref, qk_ref, v_ref, o_ref, uv_ref, p_ref, mg_ref, g_ref, gt_ref, pad_sc,
                   *, n_lat, seq):
    tm, d = x_ref.shape
    i = pl.program_id(0)
    is_lat = i * tm < n_lat
    has_prev = jnp.logical_and(is_lat, (i * tm) % seq != 0).astype(F32)
    has_next = jnp.logical_and(is_lat, ((i + 1) * tm) % seq != 0).astype(F32)

    xe = jnp.concatenate([xp_ref[...], x_ref[...], xn_ref[...]], axis=0)
    he = (_ln_plain(xe) * (1.0 + mod_ref[:, d:2 * d]) + mod_ref[:, 0:d]).astype(BF16)
    hb = he[SUBLANES:SUBLANES + tm, :]

    half = _QK_WIDTH // 2

    def qk_project(part):
        c0 = part * half
        z = jnp.dot(he, w_ref[:, c0:c0 + half], preferred_element_type=F32) + b_ref[:, c0:c0 + half]
        pad_sc[part] = z
        pad_sc[part, SUBLANES - 1:SUBLANES, :] = pad_sc[part, SUBLANES - 1:SUBLANES, :] * has_prev
        pad_sc[part, SUBLANES + tm:SUBLANES + tm + 1, :] = (
            pad_sc[part, SUBLANES + tm:SUBLANES + tm + 1, :] * has_next)

    def qk_conv(part):
        c0 = part * half
        y = (cb_ref[:, c0:c0 + half]
             + pad_sc[part, SUBLANES - 1:SUBLANES - 1 + tm, :] * cw_ref[0:1, c0:c0 + half]
             + pad_sc[part, SUBLANES:SUBLANES + tm, :] * cw_ref[1:2, c0:c0 + half]
             + pad_sc[part, SUBLANES + 1:SUBLANES + 1 + tm, :] * cw_ref[2:3, c0:c0 + half])
        act = y * _sigmoid(y)
        if part == 1:
            act = act * (HEAD_DIM ** -0.5)
        qk_ref[:, c0:c0 + half] = act.astype(qk_ref.dtype)

    def plain(ref, off, c0, cw):
        z = jnp.dot(hb, w_ref[:, off + c0:off + c0 + cw], preferred_element_type=F32)
        ref[:, c0:c0 + cw] = (z + b_ref[:, off + c0:off + c0 + cw]).astype(ref.dtype)

    mxu_tasks = []
    off = _QK_WIDTH
    for ref, (_, width) in zip((v_ref, o_ref, uv_ref, p_ref, mg_ref), _Z_PARTS):
        for c0 in range(0, width, 512):
            mxu_tasks.append(functools.partial(plain, ref, off, c0, min(512, width - c0)))
        off += width
    vpu_tasks = {2: functools.partial(_gate_prep, hb, wgt_ref, bgt_ref, g_ref, gt_ref),
                 4: functools.partial(qk_conv, 0), 6: functools.partial(qk_conv, 1)}
    qk_project(0)
    qk_project(1)
    for n, task in enumerate(mxu_tasks):
        if n in vpu_tasks:
            vpu_tasks[n]()
        task()


def _gate_prep(hb, wgt_ref, bgt_ref, g_ref, gt_ref):
    tm = hb.shape[0]
    nd = 2 * N_HEADS
    ggt = lax.dot_general(wgt_ref[...], hb, (((1,), (1,)), ((), ())),
                          preferred_element_type=F32) + bgt_ref[...]
    git, gft = ggt[:nd, :], ggt[nd:, :]
    fwd = lax.broadcasted_iota(jnp.int32, (nd, tm), 0) < N_HEADS
    pos = lax.broadcasted_iota(jnp.int32, (nd, tm), 1) % CHUNK
    lsg = _log_sigmoid(gft)
    bcum = jnp.where(fwd, _chunk_scan(lsg, pos, 1, False, False), _chunk_scan(lsg, pos, 1, True, False))
    c = git - bcum
    cmax = jnp.where(fwd, _chunk_scan(c, pos, 1, False, True), _chunk_scan(c, pos, 1, True, True))
    gt_ref[...] = c
    rows = jnp.concatenate([bcum, c, bcum + cmax, cmax, jnp.zeros((LANES - 4 * nd, tm), F32)], axis=0)
    g_ref[...] = rows.T


def _chunk_scan(a, pos, axis, reverse, use_max):
    n = a.shape[axis]
    ident = -jnp.inf if use_max else 0.0
    sh = 1
    while sh < CHUNK:
        if reverse:
            moved = jnp.where(pos < CHUNK - sh, pltpu.roll(a, n - sh, axis=axis), ident)
        else:
            moved = jnp.where(pos >= sh, pltpu.roll(a, sh, axis=axis), ident)
        a = jnp.maximum(a, moved) if use_max else a + moved
        sh *= 2
    return a


def _inproj(x, mods, w, b, wgt, bgt, conv_w, conv_b, *, n_lat, seq):
    t, d = x.shape
    tm = ROW_TILE
    n_batch = mods.shape[0] - 1
    per_tile = tm // SUBLANES

    def mod_idx(i):
        return (jnp.minimum((i * tm) // seq, n_batch), 0, 0)

    const = lambda i: (0, 0)
    row = lambda i: (i, 0)
    parts = (("qk", _QK_WIDTH),) + _Z_PARTS
    out_shapes = [jax.ShapeDtypeStruct((t, width), BF16) for _, width in parts]
    out_specs = [pl.BlockSpec((tm, width), row) for _, width in parts]
    out_shapes += [jax.ShapeDtypeStruct((t, LANES), F32), jax.ShapeDtypeStruct((2 * N_HEADS, t), F32)]
    out_specs += [pl.BlockSpec((tm, LANES), row), pl.BlockSpec((2 * N_HEADS, tm), lambda i: (0, i))]
    return pl.pallas_call(
        functools.partial(_inproj_kernel, n_lat=n_lat, seq=seq),
        grid=(t // tm,),
        in_specs=[
            pl.BlockSpec((tm, d), row),
            pl.BlockSpec((SUBLANES, d), lambda i: (jnp.maximum(i * per_tile - 1, 0), 0)),
            pl.BlockSpec((SUBLANES, d), lambda i: (jnp.minimum((i + 1) * per_tile, t // SUBLANES - 1), 0)),
            pl.BlockSpec((None, 1, mods.shape[2]), mod_idx),
            pl.BlockSpec(w.shape, const),
            pl.BlockSpec(b.shape, const),
            pl.BlockSpec(wgt.shape, const),
            pl.BlockSpec(bgt.shape, const),
            pl.BlockSpec(conv_w.shape, const),
            pl.BlockSpec(conv_b.shape, const),
        ],
        out_specs=out_specs,
        out_shape=out_shapes,
        scratch_shapes=[pltpu.VMEM((2, tm + 2 * SUBLANES, _QK_WIDTH // 2), F32)],
        compiler_params=_cparams("parallel"),
    )(x, x, x, mods, w, b, wgt, bgt, conv_w, conv_b)


def _mlstm_row_block(j, b, direction, *, n_batch, seq, ctx_len):
    ncc = ctx_len // CHUNK
    ncl = seq // CHUNK
    is_ctx = j < ncc
    if direction == 0:
        c = jnp.where(is_ctx, j, j - ncc)
    else:
        c = jnp.where(is_ctx, ncc - 1 - j, ncl - 1 - (j - ncc))
    base = jnp.where(is_ctx, (n_batch * seq + b * ctx_len) // CHUNK, (b * seq) // CHUNK)
    return base + c


def _mlstm_kernel(q_f, k_f, v_f, g_f, gt_f, q_b, k_b, v_b, g_b, gt_b, hf_ref, hb_ref, s_sc, m_sc):
    j = pl.program_id(1)

    @pl.when(j == 0)
    def _():
        s_sc[...] = jnp.zeros_like(s_sc)
        m_sc[...] = jnp.zeros_like(m_sc)

    ri = lax.broadcasted_iota(jnp.int32, (CHUNK, CHUNK), 0)
    ci = lax.broadcasted_iota(jnp.int32, (CHUNK, CHUNK), 1)
    ones_tile = jnp.ones((CHUNK, HEAD_DIM), BF16)
    dirs = ((q_f, k_f, v_f, g_f, gt_f, hf_ref), (q_b, k_b, v_b, g_b, gt_b, hb_ref))
    combos = [(d, hh) for d in range(2) for hh in range(N_HEADS)]
    heads = lambda ref, hh: ref[:, hh * HEAD_DIM:(hh + 1) * HEAD_DIM]

    nd = 2 * N_HEADS
    tri = ((ci <= ri), (ci >= ri))
    g = (g_f[...], g_b[...])
    gt = (gt_f[...], gt_b[...])

    qk_s, v1, state, m_prev = {}, {}, {}, {}
    for d, hh in combos:
        q_ref, k_ref, v_ref = dirs[d][:3]
        qk_s[d, hh] = lax.dot_general(heads(q_ref, hh), heads(k_ref, hh), (((1,), (1,)), ((), ())),
                                      preferred_element_type=F32)
        v1[d, hh] = jnp.concatenate([heads(v_ref, hh), ones_tile], axis=1)
        state[d, hh] = s_sc[d, hh]
        m_prev[d, hh] = m_sc[d * N_HEADS + hh][0:1, 0:1]

    q_state = {}
    for d, hh in combos:
        q_state[d, hh] = jnp.dot(heads(dirs[d][0], hh), state[d, hh].astype(BF16), preferred_element_type=F32)

    m_t, decay, scores = {}, {}, {}
    for d, hh in combos:
        col = d * N_HEADS + hh
        bcum = g[d][:, col:col + 1]
        inter = bcum + m_prev[d, hh]
        m_t[d, hh] = jnp.maximum(inter, g[d][:, 2 * nd + col:2 * nd + col + 1])
        decay[d, hh] = jnp.exp(inter - m_t[d, hh])
        dlog = jnp.where(tri[d], (bcum - m_t[d, hh]) + gt[d][col:col + 1, :], -jnp.inf)
        scores[d, hh] = (qk_s[d, hh] * jnp.exp(dlog)).astype(BF16)

    for d, hh in combos:
        numden = decay[d, hh] * q_state[d, hh] + jnp.dot(scores[d, hh], v1[d, hh], preferred_element_type=F32)
        num = numden[:, :HEAD_DIM]
        den = numden[:, HEAD_DIM:]
        out = num / jnp.maximum(jnp.abs(den), jnp.exp(-m_t[d, hh]))
        out_ref = dirs[d][5]
        out_ref[:, hh * HEAD_DIM:(hh + 1) * HEAD_DIM] = out.astype(out_ref.dtype)

    kw, a, m_new = {}, {}, {}
    for d, hh in combos:
        col = d * N_HEADS + hh
        last = CHUNK - 1 if d == 0 else 0
        total = g[d][last:last + 1, col:col + 1]
        m_new[d, hh] = total + jnp.maximum(m_prev[d, hh], g[d][last:last + 1, 3 * nd + col:3 * nd + col + 1])
        a[d, hh] = jnp.exp(total + m_prev[d, hh] - m_new[d, hh])
        wk = jnp.exp(total + g[d][:, nd + col:nd + col + 1] - m_new[d, hh])
        kw[d, hh] = (heads(dirs[d][1], hh).astype(F32) * wk).astype(BF16)

    for d, hh in combos:
        upd = lax.dot_general(kw[d, hh], v1[d, hh], (((0,), (0,)), ((), ())), preferred_element_type=F32)
        s_sc[d, hh] = a[d, hh] * state[d, hh] + upd
        m_sc[d * N_HEADS + hh] = jnp.broadcast_to(m_new[d, hh], m_sc.shape[1:])


def _mlstm(zqk, zv, g, gt, *, n_batch, seq, ctx_len):
    t = zqk.shape[0]
    nsteps = (seq + ctx_len) // CHUNK
    width = N_HEADS * HEAD_DIM
    idx = functools.partial(_mlstm_row_block, n_batch=n_batch, seq=seq, ctx_len=ctx_len)

    def specs(d):
        rb = lambda b, j: idx(j, b, d)
        return [
            pl.BlockSpec((CHUNK, width), lambda b, j: (rb(b, j), 0)),
            pl.BlockSpec((CHUNK, width), lambda b, j: (rb(b, j), 1)),
            pl.BlockSpec((CHUNK, width), lambda b, j: (rb(b, j), 0)),
            pl.BlockSpec((CHUNK, LANES), lambda b, j: (rb(b, j), 0)),
            pl.BlockSpec((2 * N_HEADS, CHUNK), lambda b, j: (0, rb(b, j))),
        ]

    out_spec = lambda d: pl.BlockSpec((CHUNK, width), lambda b, j: (idx(j, b, d), 0))
    return pl.pallas_call(
        _mlstm_kernel,
        grid=(n_batch, nsteps),
        in_specs=specs(0) + specs(1),
        out_specs=[out_spec(0), out_spec(1)],
        out_shape=[jax.ShapeDtypeStruct((t, width), BF16)] * 2,
        scratch_shapes=[
            pltpu.VMEM((2, N_HEADS, HEAD_DIM, 2 * HEAD_DIM), F32),
            pltpu.VMEM((2 * N_HEADS, 8, LANES), F32),
        ],
        compiler_params=_cparams("parallel", "arbitrary"),
    )(zqk, zqk, zv, g, gt, zqk, zqk, zv, g, gt)


def _pool_constants(ctx_len):
    tm = ROW_TILE
    masks = np.zeros((2, len(POOL_WINDOWS), tm, tm), np.float32)
    cnts = np.zeros((2, tm, len(POOL_WINDOWS) * POOL_GROUP_DIM), np.float32)
    for kind, length in enumerate((GRID_W, ctx_len)):
        for gi, w in enumerate(POOL_WINDOWS):
            for t in range(tm):
                base = (t // length) * length
                tl = t - base
                lo = min(max(tl - w // 2, 0), length)
                hi = min(max(tl + w // 2, 0), length)
                masks[kind, gi, t, base + lo:base + hi] = 1.0
                cnts[kind, t, gi * POOL_GROUP_DIM:(gi + 1) * POOL_GROUP_DIM] = hi - lo
    return jnp.asarray(masks, BF16), jnp.asarray(cnts, F32)


def _mix_kernel(x_ref, zp_ref, zuv_ref, zmg_ref, zo_ref, hf_ref, hb_ref, mod_ref,
                pm_ref, cnt_ref, pw_ref, ps_ref, sg_ref, sb_ref, sw_ref, sbm_ref, ng_ref,
                wbp_ref, wbm_ref, wbs_ref, wo_ref, l1g_ref, l1b_ref, wr_ref, br_ref,
                x1_ref, h2_ref, route_ref, cnt_out_ref, cnt_sc, *, alpha):
    tm, d = x_ref.shape
    i = pl.program_id(0)

    @pl.when(i == 0)
    def _():
        cnt_sc[...] = jnp.zeros_like(cnt_sc)

    lane256 = lax.broadcasted_iota(jnp.int32, (1, 4 * POOL_GROUP_DIM), 1) // POOL_GROUP_DIM

    xp = zp_ref[...]
    xp_f = xp.astype(F32)
    pooled = jnp.zeros(xp_f.shape, F32)
    for gi in range(len(POOL_WINDOWS)):
        s = jnp.dot(pm_ref[gi], xp, preferred_element_type=F32)
        pooled = jnp.where(lane256 == gi, s, pooled)
    diff = pooled / cnt_ref[...] - xp_f
    pool_o = jnp.dot(diff.astype(BF16), pw_ref[...], preferred_element_type=F32) * ps_ref[...]

    uv = zuv_ref[...].astype(F32)
    uv = 0.5 * uv * (1.0 + lax.erf(uv * (2.0 ** -0.5)))
    half = uv.shape[1] // 2
    u = uv[:, :half]
    vn = (_ln_plain(uv[:, half:]) * sg_ref[...] + sb_ref[...]).astype(BF16)
    mixed_chunks = []
    for c0 in range(0, tm, CHUNK):
        vc = vn[c0:c0 + CHUNK, :]
        mixed = jnp.zeros((CHUNK, half), F32)
        for gi in range(sw_ref.shape[0]):
            s = jnp.dot(sw_ref[gi], vc, preferred_element_type=F32)
            mixed = jnp.where(lane256 == gi, s, mixed)
        mixed_chunks.append(mixed + sbm_ref[...])
    sgu_o = u * jnp.concatenate(mixed_chunks, axis=0)

    hsum = hf_ref[...].astype(F32) + hb_ref[...].astype(F32)
    heads = []
    for hh in range(N_HEADS):
        hv = hsum[:, hh * HEAD_DIM:(hh + 1) * HEAD_DIM]
        mu = jnp.mean(hv, axis=-1, keepdims=True)
        hc = hv - mu
        var = jnp.mean(hc * hc, axis=-1, keepdims=True)
        heads.append(hc * lax.rsqrt(var + HEAD_NORM_EPS))
    hn = jnp.concatenate(heads, axis=1)
    mlstm_o = hn * ng_ref[...] * _sigmoid(zo_ref[...].astype(F32))

    y = _sigmoid(zmg_ref[:, 0:d].astype(F32)) * jnp.dot(pool_o.astype(BF16), wbp_ref[...],
                                                         preferred_element_type=F32)
    y += _sigmoid(zmg_ref[:, d:2 * d].astype(F32)) * jnp.dot(mlstm_o.astype(BF16), wbm_ref[...],
                                                             preferred_element_type=F32)
    y += _sigmoid(zmg_ref[:, 2 * d:3 * d].astype(F32)) * jnp.dot(sgu_o.astype(BF16), wbs_ref[...],
                                                                 preferred_element_type=F32)
    y2 = jnp.dot(y.astype(BF16), wo_ref[...], preferred_element_type=F32)

    g1 = mod_ref[:, 2 * d:3 * d]
    x1 = _ln_plain(alpha * x_ref[...] + g1 * y2) * l1g_ref[...] + l1b_ref[...]
    x1_ref[...] = x1
    h2 = _ln_plain(x1) * (1.0 + mod_ref[:, 4 * d:5 * d]) + mod_ref[:, 3 * d:4 * d]
    h2b = h2.astype(BF16)
    h2_ref[...] = _pack_rows(h2b)

    logits = jnp.dot(h2b, wr_ref[...], preferred_element_type=F32) + br_ref[...]
    lane = lax.broadcasted_iota(jnp.int32, logits.shape, 1)
    work = logits
    vals, idxs, hots = [], [], []
    for _ in range(TOP_K):
        mval = jnp.max(work, axis=-1, keepdims=True)
        idx = jnp.min(jnp.where(work == mval, lane, LANES), axis=-1, keepdims=True)
        hot = lane == idx
        vals.append(mval)
        idxs.append(idx)
        hots.append(hot)
        work = jnp.where(hot, -jnp.inf, work)
    exps = [jnp.exp(v - vals[0]) for v in vals]
    denom = exps[0] + exps[1] + exps[2] + exps[3]
    sel = (hots[0] | hots[1] | hots[2] | hots[3])
    r_i = lax.broadcasted_iota(jnp.int32, (tm, tm), 0)
    c_i = lax.broadcasted_iota(jnp.int32, (tm, tm), 1)
    before = (c_i < r_i).astype(BF16)
    sel_b = sel.astype(F32).astype(BF16)
    rank = jnp.dot(before, sel_b, preferred_element_type=F32) + cnt_sc[0:1, :]
    route = jnp.zeros(logits.shape, F32)
    for kk in range(TOP_K):
        rk = jnp.sum(jnp.where(hots[kk], rank, 0.0), axis=-1, keepdims=True)
        route = jnp.where(lane == kk, idxs[kk].astype(F32), route)
        route = jnp.where(lane == TOP_K + kk, exps[kk] / denom, route)
        route = jnp.where(lane == 2 * TOP_K + kk, rk, route)
    route_ref[...] = route
    new_cnt = cnt_sc[0:1, :] + jnp.sum(sel.astype(F32), axis=0, keepdims=True)
    cnt_sc[...] = jnp.broadcast_to(new_cnt, cnt_sc.shape)
    cnt_out_ref[...] = jnp.broadcast_to(new_cnt, cnt_out_ref.shape)


def _mix(x, zp, zuv, zmg, zo, hf, hb, mods, consts, *, n_rows, n_lat, seq, alpha):
    t, d = x.shape
    tm = ROW_TILE
    n_batch = mods.shape[0] - 1
    row = lambda i: (i, 0)
    const2 = lambda i: (0, 0)
    const3 = lambda i: (0, 0, 0)
    kind = lambda i: jnp.where(i * tm >= n_lat, 1, 0)

    def mod_idx(i):
        return (jnp.minimum((i * tm) // seq, n_batch), 0, 0)

    (pm, cnt, pw, ps, sg, sb, sw, sbm, ng, wbp, wbm, wbs, wo, l1g, l1b, wr, br) = consts
    in_specs = [
        pl.BlockSpec((tm, d), row),
        pl.BlockSpec((tm, zp.shape[1]), row),
        pl.BlockSpec((tm, zuv.shape[1]), row),
        pl.BlockSpec((tm, zmg.shape[1]), row),
        pl.BlockSpec((tm, zo.shape[1]), row),
        pl.BlockSpec((tm, hf.shape[1]), row),
        pl.BlockSpec((tm, hb.shape[1]), row),
        pl.BlockSpec((None, 1, mods.shape[2]), mod_idx),
        pl.BlockSpec((None,) + pm.shape[1:], lambda i: (kind(i), 0, 0, 0)),
        pl.BlockSpec((None,) + cnt.shape[1:], lambda i: (kind(i), 0, 0)),
        pl.BlockSpec(pw.shape, const2),
        pl.BlockSpec(ps.shape, const2),
        pl.BlockSpec(sg.shape, const2),
        pl.BlockSpec(sb.shape, const2),
        pl.BlockSpec(sw.shape, const3),
        pl.BlockSpec(sbm.shape, const2),
        pl.BlockSpec(ng.shape, const2),
        pl.BlockSpec(wbp.shape, const2),
        pl.BlockSpec(wbm.shape, const2),
        pl.BlockSpec(wbs.shape, const2),
        pl.BlockSpec(wo.shape, const2),
        pl.BlockSpec(l1g.shape, const2),
        pl.BlockSpec(l1b.shape, const2),
        pl.BlockSpec(wr.shape, const2),
        pl.BlockSpec(br.shape, const2),
    ]
    return pl.pallas_call(
        functools.partial(_mix_kernel, alpha=alpha),
        grid=(n_rows // tm,),
        in_specs=in_specs,
        out_specs=[
            pl.BlockSpec((tm, d), row),
            pl.BlockSpec((tm, d // 2), row),
            pl.BlockSpec((tm, LANES), row),
            pl.BlockSpec((8, LANES), const2),
        ],
        out_shape=[
            jax.ShapeDtypeStruct((n_rows, d), F32),
            jax.ShapeDtypeStruct((n_rows, d // 2), jnp.uint32),
            jax.ShapeDtypeStruct((n_rows, LANES), F32),
            jax.ShapeDtypeStruct((8, LANES), F32),
        ],
        scratch_shapes=[pltpu.VMEM((8, LANES), F32)],
        compiler_params=_cparams("arbitrary"),
    )(x, zp, zuv, zmg, zo, hf, hb, mods, pm, cnt, pw, ps, sg, sb, sw, sbm, ng,
      wbp, wbm, wbs, wo, l1g, l1b, wr, br)


SCATTER_WINDOW = 128
SCATTER_PARTS = 2


def _scatter_rows(x, idx, n_out, part):
    n = x.shape[0]
    w = x.shape[1] // SCATTER_PARTS
    n_idx = idx.shape[0]
    mesh = plsc.VectorSubcoreMesh(core_axis_name="core", subcore_axis_name="subcore")

    @pl.kernel(out_type=jax.ShapeDtypeStruct((n_out, w), x.dtype), mesh=mesh, scratch_types=[])
    def scatter_kernel(x_hbm, i_hbm, o_hbm):
        def body(x_vmem, i_vmem):
            for k in range(n_idx):
                pltpu.sync_copy(x_vmem, o_hbm.at[i_vmem.at[k]])

        pltpu.emit_pipeline(
            body,
            grid=(n // SCATTER_WINDOW,),
            in_specs=[pl.BlockSpec((SCATTER_WINDOW, w), lambda i: (i, part)),
                      pl.BlockSpec((n_idx, SCATTER_WINDOW), lambda i: (0, i))],
            out_specs=[],
            core_axis_name=("core", "subcore"),
            dimension_semantics=(pltpu.PARALLEL,),
        )(x_hbm, i_hbm)

    return scatter_kernel(x, idx)


def _gather_rows(x, idx):
    w = x.shape[1]
    n_idx = idx.shape[1]
    mesh = plsc.VectorSubcoreMesh(core_axis_name="core", subcore_axis_name="subcore")

    @pl.kernel(out_type=jax.ShapeDtypeStruct((n_idx, w), x.dtype), mesh=mesh, scratch_types=[])
    def gather_kernel(x_hbm, i_hbm, o_hbm):
        def body(i_vmem, o_vmem):
            pltpu.sync_copy(x_hbm.at[i_vmem.at[0]], o_vmem)

        pltpu.emit_pipeline(
            body,
            grid=(n_idx // SCATTER_WINDOW,),
            in_specs=[pl.BlockSpec((1, SCATTER_WINDOW), lambda i: (0, i))],
            out_specs=[pl.BlockSpec((SCATTER_WINDOW, w), lambda i: (i, 0))],
            core_axis_name=("core", "subcore"),
            dimension_semantics=(pltpu.PARALLEL,),
        )(i_hbm, o_hbm)

    return gather_kernel(x, idx)


def _expert_kernel(te_ref, nt_ref, xa_ref, xb_ref, wgu_ref, bgu_ref, wd_ref, bd_ref, ya_ref, yb_ref,
                   wgu_sc, wd_sc):
    i = pl.program_id(0)
    f = wd_ref.shape[0]
    used = i < nt_ref[0]

    @pl.when(jnp.logical_and(used, jnp.logical_or(i == 0, te_ref[i] != te_ref[jnp.maximum(i - 1, 0)])))
    def _():
        for c0 in range(0, 2 * f, 512):
            wgu_sc[:, c0:c0 + 512] = wgu_ref[:, c0:c0 + 512].astype(BF16)
        for c0 in range(0, f, 256):
            wd_sc[c0:c0 + 256, :] = wd_ref[c0:c0 + 256, :].astype(BF16)

    @pl.when(used)
    def _():
        x = _unpack_rows([xa_ref[...], xb_ref[...]]).astype(BF16)
        acc = jnp.zeros((x.shape[0], wd_ref.shape[1]), F32)
        for c0 in range(0, f, 512):
            gate = jnp.dot(x, wgu_sc[:, c0:c0 + 512], preferred_element_type=F32) + bgu_ref[:, c0:c0 + 512]
            up = (jnp.dot(x, wgu_sc[:, f + c0:f + c0 + 512], preferred_element_type=F32)
                  + bgu_ref[:, f + c0:f + c0 + 512])
            gate = jnp.minimum(gate, SWIGLU_LIMIT)
            up = jnp.clip(up, -SWIGLU_LIMIT, SWIGLU_LIMIT)
            glu = gate * _sigmoid(SWIGLU_ALPHA * gate)
            act = ((up + 1.0) * glu).astype(BF16)
            acc += jnp.dot(act, wd_sc[c0:c0 + 512, :], preferred_element_type=F32)
        words = _pack_rows(acc + bd_ref[...])
        ya_ref[...] = words[:, :ya_ref.shape[1]]
        yb_ref[...] = words[:, ya_ref.shape[1]:]

    @pl.when(jnp.logical_not(used))
    def _():
        ya_ref[...] = jnp.zeros_like(ya_ref)
        yb_ref[...] = jnp.zeros_like(yb_ref)


def _experts(tile_expert, n_tiles_used, xa, xb, wgu, bgu, wd, bd, *, layer):
    p = xa.shape[0]
    d = 2 * SCATTER_PARTS * xa.shape[1]
    tm = EXPERT_TILE
    f2 = wgu.shape[3]
    grid_spec = pltpu.PrefetchScalarGridSpec(
        num_scalar_prefetch=2,
        grid=(p // tm,),
        in_specs=[
            pl.BlockSpec((tm, xa.shape[1]), lambda i, te, nt: (i, 0)),
            pl.BlockSpec((tm, xb.shape[1]), lambda i, te, nt: (i, 0)),
            pl.BlockSpec((None, None, d, f2), lambda i, te, nt: (layer, te[i], 0, 0)),
            pl.BlockSpec((None, None, 1, f2), lambda i, te, nt: (layer, te[i], 0, 0)),
            pl.BlockSpec((None, None, f2 // 2, d), lambda i, te, nt: (layer, te[i], 0, 0)),
            pl.BlockSpec((None, None, 1, d), lambda i, te, nt: (layer, te[i], 0, 0)),
        ],
        out_specs=[pl.BlockSpec((tm, xa.shape[1]), lambda i, te, nt: (i, 0)),
                   pl.BlockSpec((tm, xb.shape[1]), lambda i, te, nt: (i, 0))],
        scratch_shapes=[pltpu.VMEM((d, f2), BF16), pltpu.VMEM((f2 // 2, d), BF16)],
    )
    return pl.pallas_call(
        _expert_kernel,
        grid_spec=grid_spec,
        out_shape=[jax.ShapeDtypeStruct(xa.shape, xa.dtype), jax.ShapeDtypeStruct(xb.shape, xb.dtype)],
        compiler_params=_cparams("arbitrary"),
    )(tile_expert, n_tiles_used, xa, xb, wgu, bgu, wd, bd)


def _combine_kernel(x1_ref, ya_ref, yb_ref, route_ref, mod_ref, g_ref, b_ref, *rest, alpha):
    x2_ref = rest[-1]
    d = x1_ref.shape[1]
    route = route_ref[...]
    f = jnp.zeros(x1_ref.shape, F32)
    for kk in range(TOP_K):
        f += route[:, TOP_K + kk:TOP_K + kk + 1] * _unpack_rows([ya_ref[kk], yb_ref[kk]])
    g2 = mod_ref[:, 5 * d:6 * d]
    x2_ref[...] = _ln_plain(alpha * x1_ref[...] + g2 * f) * g_ref[...] + b_ref[...]


def _combine(x1, yg_a, yg_b, route, mods, g, b, *, seq, alpha, dest=None, dest_rows=None, tile_offset=0):
    t, d = x1.shape
    tm = ROW_TILE
    n_batch = mods.shape[0] - 1
    row = lambda i: (i, 0)

    def mod_idx(i):
        return (jnp.minimum((i * tm) // seq, n_batch), 0, 0)

    in_specs = [
        pl.BlockSpec((tm, d), row),
        pl.BlockSpec((TOP_K, tm, yg_a.shape[2]), lambda i: (0, i, 0)),
        pl.BlockSpec((TOP_K, tm, yg_b.shape[2]), lambda i: (0, i, 0)),
        pl.BlockSpec((tm, LANES), row),
        pl.BlockSpec((None, 1, mods.shape[2]), mod_idx),
        pl.BlockSpec(g.shape, lambda i: (0, 0)),
        pl.BlockSpec(b.shape, lambda i: (0, 0)),
    ]
    args = [x1, yg_a, yg_b, route, mods, g, b]
    aliases = {}
    if dest is not None:
        in_specs.append(pl.BlockSpec(memory_space=pl.ANY))
        args.append(dest)
        aliases = {len(args) - 1: 0}
    return pl.pallas_call(
        functools.partial(_combine_kernel, alpha=alpha),
        grid=(t // tm,),
        in_specs=in_specs,
        out_specs=pl.BlockSpec((tm, d), lambda i: (i + tile_offset, 0)),
        out_shape=jax.ShapeDtypeStruct((t if dest_rows is None else dest_rows, d), F32),
        input_output_aliases=aliases,
        compiler_params=_cparams("parallel"),
    )(*args)


def kernel(x, c, ctx, c_ctx, w_ada, b_ada, w_in, b_in, pool_w, pool_scale, qk_conv_w, qk_conv_b,
           mlstm_norm_g, sgu_ln_g, sgu_ln_b, sgu_w, sgu_b, w_br_pool, w_br_mlstm, w_br_sgu, w_out,
           ln1_g, ln1_b, w_router, b_router, w_gate_up, b_gate_up, w_down, b_down, ln2_g, ln2_b):
    n_batch, seq, d = x.shape
    ctx_len = ctx.shape[1]
    depth = w_in.shape[0]
    n_exp = w_router.shape[2]
    assert ctx_len == ROW_TILE and seq % ROW_TILE == 0 and ROW_TILE % GRID_W == 0
    assert n_batch % N_STREAMS == 0
    alpha = float((2 * depth) ** 0.25)

    rows = ((n_batch + 1 + 7) // 8) * 8
    cc = jnp.zeros((rows, d), F32).at[:n_batch].set(c).at[n_batch].set(c_ctx)
    mods_all = _ada_table(cc, w_ada, b_ada)[:, :n_batch + 1].reshape(depth, n_batch + 1, 1, 6 * d)

    pool_w_, mlw = 256, 512
    col = lambda a, lo, hi: a[..., lo:hi]
    g_lo = pool_w_ + 4 * mlw
    parts = ((pool_w_, pool_w_ + 2 * mlw), (pool_w_ + 2 * mlw, pool_w_ + 3 * mlw), (pool_w_ + 3 * mlw, g_lo),
             (g_lo + _N_GATE_COLS, g_lo + _N_GATE_COLS + 512), (0, pool_w_),
             (g_lo + _N_GATE_COLS + 512, w_in.shape[2]))
    w_main = jnp.concatenate([col(w_in, lo, hi) for lo, hi in parts], axis=-1).astype(BF16)
    b_main = jnp.concatenate([col(b_in, lo, hi) for lo, hi in parts], axis=-1)[:, None, :]
    gw, gb = col(w_in, g_lo, g_lo + _N_GATE_COLS), col(b_in, g_lo, g_lo + _N_GATE_COLS)
    nh = N_HEADS
    pick = lambda a, k: jnp.concatenate([a[..., k * nh:(k + 1) * nh], a[..., (k + 2) * nh:(k + 3) * nh]], axis=-1)
    w_gate_t = jnp.swapaxes(jnp.concatenate([pick(gw, 0), pick(gw, 1)], axis=-1), 1, 2).astype(BF16)
    b_gate_t = jnp.concatenate([pick(gb, 0), pick(gb, 1)], axis=-1)[:, :, None]

    pm, cnt = _pool_constants(ctx_len)
    eye = jnp.eye(len(POOL_WINDOWS), dtype=F32)
    pool_bd = jnp.einsum('lgcd,gh->lgchd', pool_w, eye).reshape(depth, 256, 256).astype(BF16)
    sgu_bias_map = jnp.repeat(jnp.swapaxes(sgu_b, 1, 2), sgu_ln_g.shape[1] // sgu_b.shape[1], axis=2)
    w_router_p = jnp.pad(w_router, ((0, 0), (0, 0), (0, LANES - n_exp))).astype(BF16)
    b_router_p = jnp.pad(b_router, ((0, 0), (0, LANES - n_exp)), constant_values=NEG_BIG)[:, None, :]
    r1 = lambda a: a[:, None, :]
    b_gu4 = b_gate_up[:, :, None, :]
    b_d4 = b_down[:, :, None, :]

    nb = n_batch // N_STREAMS
    n_lat = nb * seq
    t_all = n_lat + nb * ctx_len
    streams, stream_mods = [], []
    for h in range(N_STREAMS):
        sl = slice(h * nb, (h + 1) * nb)
        streams.append(jnp.concatenate([x[sl].reshape(n_lat, d), ctx[sl].reshape(nb * ctx_len, d)], axis=0))
        stream_mods.append(jnp.concatenate([mods_all[:, sl], mods_all[:, n_batch:n_batch + 1]], axis=1))
    out = None
    for i in range(depth):
        last = i == depth - 1
        for h in range(N_STREAMS):
            stream = streams[h]
            mods = stream_mods[h][i]
            zqk, zv, zo, zuv, zp, zmg, g, gt = _inproj(
                stream, mods, w_main[i], b_main[i], w_gate_t[i], b_gate_t[i],
                qk_conv_w[i], r1(qk_conv_b)[i], n_lat=n_lat, seq=seq)
            hf, hb = _mlstm(zqk, zv, g, gt, n_batch=nb, seq=seq, ctx_len=ctx_len)
            n_rows = n_lat if last else t_all
            consts = (pm, cnt, pool_bd[i], r1(pool_scale)[i], r1(sgu_ln_g)[i], r1(sgu_ln_b)[i],
                      sgu_w[i].astype(BF16), sgu_bias_map[i], r1(mlstm_norm_g)[i],
                      w_br_pool[i].astype(BF16), w_br_mlstm[i].astype(BF16), w_br_sgu[i].astype(BF16),
                      w_out[i].astype(BF16), r1(ln1_g)[i], r1(ln1_b)[i], w_router_p[i], b_router_p[i])
            x1, h2, route, counts = _mix(stream, zp, zuv, zmg, zo, hf, hb, mods, consts,
                                         n_rows=n_rows, n_lat=n_lat, seq=seq, alpha=alpha)

            cnt_e = counts[0, :n_exp].astype(jnp.int32)
            padded = ((cnt_e + EXPERT_TILE - 1) // EXPERT_TILE) * EXPERT_TILE
            ends = jnp.cumsum(padded)
            starts = ends - padded
            e_idx = route[:, 0:TOP_K].astype(jnp.int32)
            rank = route[:, 2 * TOP_K:3 * TOP_K].astype(jnp.int32)
            pos_t = (starts[e_idx] + rank).T
            p_rows = n_rows * TOP_K + n_exp * EXPERT_TILE
            n_tiles = p_rows // EXPERT_TILE
            tile_start = jnp.arange(n_tiles, dtype=jnp.int32) * EXPERT_TILE
            tile_expert = jnp.minimum(
                jnp.sum((ends[None, :] <= tile_start[:, None]).astype(jnp.int32), axis=1), n_exp - 1)
            n_used = (ends[-1] // EXPERT_TILE).astype(jnp.int32).reshape(1)
            xs_a = _scatter_rows(h2, pos_t, p_rows, 0)
            xs_b = _scatter_rows(h2, pos_t, p_rows, 1)
            ys_a, ys_b = _experts(tile_expert, n_used, xs_a, xs_b, w_gate_up, b_gu4, w_down, b_d4, layer=i)
            pos_flat = pos_t.reshape(1, TOP_K * n_rows)
            yg_a = _gather_rows(ys_a, pos_flat).reshape(TOP_K, n_rows, -1)
            yg_b = _gather_rows(ys_b, pos_flat).reshape(TOP_K, n_rows, -1)
            if last:
                out = _combine(x1, yg_a, yg_b, route, mods, r1(ln2_g)[i], r1(ln2_b)[i], seq=seq, alpha=alpha,
                               dest=out, dest_rows=n_batch * seq, tile_offset=h * n_lat // ROW_TILE)
            else:
                streams[h] = _combine(x1, yg_a, yg_b, route, mods, r1(ln2_g)[i], r1(ln2_b)[i],
                                      seq=seq, alpha=alpha)
    return out.reshape(n_batch, seq, d)
```

```python
import functools

import numpy as np
import jax
import jax.numpy as jnp
from jax import lax
from jax.experimental import pallas as pl
from jax.experimental.pallas import tpu as pltpu
from jax.experimental.pallas import tpu_sc as plsc

F32 = jnp.float32
BF16 = jnp.bfloat16

GRID_W = 64
POOL_WINDOWS = (2, 4, 8, 16)
POOL_GROUP_DIM = 64
N_HEADS = 4
HEAD_DIM = 128
CHUNK = 128
TOP_K = 4
SWIGLU_LIMIT = 7.0
SWIGLU_ALPHA = 1.702
LN_EPS = 1e-5
HEAD_NORM_EPS = 1e-6

ROW_TILE = 256
EXPERT_TILE = 512
N_STREAMS = 2
SUBLANES = 8
LANES = 128
NEG_BIG = -1e30
VMEM_LIMIT = 56 * 1024 * 1024


def _cparams(*sem):
    return pltpu.CompilerParams(dimension_semantics=sem, vmem_limit_bytes=VMEM_LIMIT)


def _ln_plain(x):
    mu = jnp.mean(x, axis=-1, keepdims=True)
    xc = x - mu
    var = jnp.mean(xc * xc, axis=-1, keepdims=True)
    return xc * lax.rsqrt(var + LN_EPS)


def _sigmoid(x):
    return 0.5 * jnp.tanh(0.5 * x) + 0.5


def _log_sigmoid(x):
    return jnp.minimum(x, 0.0) - jnp.log1p(jnp.exp(-jnp.abs(x)))


def _pack_rows(x):
    half = x.shape[1] // 2
    bits = lax.bitcast_convert_type(x.astype(BF16).astype(F32), jnp.uint32)
    return (bits[:, half:] & jnp.uint32(0xFFFF0000)) | (bits[:, :half] >> 16)


def _unpack_rows(parts):
    low = [lax.bitcast_convert_type(w << 16, F32) for w in parts]
    high = [lax.bitcast_convert_type(w & jnp.uint32(0xFFFF0000), F32) for w in parts]
    return jnp.concatenate(low + high, axis=1)


def _ada_kernel(c_ref, w_ref, b_ref, o_ref):
    c = c_ref[...]
    s = c * _sigmoid(c)
    o_ref[...] = jnp.dot(s, w_ref[...], preferred_element_type=F32,
                         precision=lax.Precision.HIGHEST) + b_ref[...]


def _ada_table(cc, w_ada, b_ada):
    depth, d, n = w_ada.shape
    rows = cc.shape[0]
    bn = 1536
    return pl.pallas_call(
        _ada_kernel,
        grid=(depth, n // bn),
        in_specs=[
            pl.BlockSpec((rows, d), lambda l, j: (0, 0)),
            pl.BlockSpec((None, d, bn), lambda l, j: (l, 0, j)),
            pl.BlockSpec((None, 1, bn), lambda l, j: (l, 0, j)),
        ],
        out_specs=pl.BlockSpec((None, rows, bn), lambda l, j: (l, 0, j)),
        out_shape=jax.ShapeDtypeStruct((depth, rows, n), F32),
        compiler_params=_cparams("parallel", "parallel"),
    )(cc, w_ada, b_ada.reshape(depth, 1, n))


_QK_WIDTH = 2 * N_HEADS * HEAD_DIM
_Z_PARTS = (("v", 512), ("o", 512), ("uv", 512), ("pool", 256), ("mg", 3072))
_N_GATE_COLS = 4 * N_HEADS


def _inproj_kernel(x_ref, xp_ref, xn_ref, mod_ref, w_ref, b_ref, wgt_ref, bgt_ref,
                   cw_ref, cb_ref, qk_ref, v_ref, o_ref, uv_ref, p_ref, mg_ref, g_ref, gt_ref, pad_sc,
                   *, n_lat, seq):
    tm, d = x_ref.shape
    i = pl.program_id(0)
    is_lat = i * tm < n_lat
    has_prev = jnp.logical_and(is_lat, (i * tm) % seq != 0).astype(F32)
    has_next = jnp.logical_and(is_lat, ((i + 1) * tm) % seq != 0).astype(F32)

    xe = jnp.concatenate([xp_ref[...], x_ref[...], xn_ref[...]], axis=0)
    he = (_ln_plain(xe) * (1.0 + mod_ref[:, d:2 * d]) + mod_ref[:, 0:d]).astype(BF16)
    hb = he[SUBLANES:SUBLANES + tm, :]

    half = _QK_WIDTH // 2

    def qk_project(part):
        c0 = part * half
        z = jnp.dot(he, w_ref[:, c0:c0 + half], preferred_element_type=F32) + b_ref[:, c0:c0 + half]
        pad_sc[part] = z
        pad_sc[part, SUBLANES - 1:SUBLANES, :] = pad_sc[part, SUBLANES - 1:SUBLANES, :] * has_prev
        pad_sc[part, SUBLANES + tm:SUBLANES + tm + 1, :] = (
            pad_sc[part, SUBLANES + tm:SUBLANES + tm + 1, :] * has_next)

    def qk_conv(part):
        c0 = part * half
        y = (cb_ref[:, c0:c0 + half]
             + pad_sc[part, SUBLANES - 1:SUBLANES - 1 + tm, :] * cw_ref[0:1, c0:c0 + half]
             + pad_sc[part, SUBLANES:SUBLANES + tm, :] * cw_ref[1:2, c0:c0 + half]
             + pad_sc[part, SUBLANES + 1:SUBLANES + 1 + tm, :] * cw_ref[2:3, c0:c0 + half])
        act = y * _sigmoid(y)
        if part == 1:
            act = act * (HEAD_DIM ** -0.5)
        qk_ref[:, c0:c0 + half] = act.astype(qk_ref.dtype)

    def plain(ref, off, c0, cw):
        z = jnp.dot(hb, w_ref[:, off + c0:off + c0 + cw], preferred_element_type=F32)
        ref[:, c0:c0 + cw] = (z + b_ref[:, off + c0:off + c0 + cw]).astype(ref.dtype)

    mxu_tasks = []
    off = _QK_WIDTH
    for ref, (_, width) in zip((v_ref, o_ref, uv_ref, p_ref, mg_ref), _Z_PARTS):
        for c0 in range(0, width, 512):
            mxu_tasks.append(functools.partial(plain, ref, off, c0, min(512, width - c0)))
        off += width
    vpu_tasks = {2: functools.partial(_gate_prep, hb, wgt_ref, bgt_ref, g_ref, gt_ref),
                 4: functools.partial(qk_conv, 0), 6: functools.partial(qk_conv, 1)}
    qk_project(0)
    qk_project(1)
    for n, task in enumerate(mxu_tasks):
        if n in vpu_tasks:
            vpu_tasks[n]()
        task()


def _gate_prep(hb, wgt_ref, bgt_ref, g_ref, gt_ref):
    tm = hb.shape[0]
    nd = 2 * N_HEADS
    ggt = lax.dot_general(wgt_ref[...], hb, (((1,), (1,)), ((), ())),
                          preferred_element_type=F32) + bgt_ref[...]
    git, gft = ggt[:nd, :], ggt[nd:, :]
    fwd = lax.broadcasted_iota(jnp.int32, (nd, tm), 0) < N_HEADS
    pos = lax.broadcasted_iota(jnp.int32, (nd, tm), 1) % CHUNK
    lsg = _log_sigmoid(gft)
    bcum = jnp.where(fwd, _chunk_scan(lsg, pos, 1, False, False), _chunk_scan(lsg, pos, 1, True, False))
    c = git - bcum
    cmax = jnp.where(fwd, _chunk_scan(c, pos, 1, False, True), _chunk_scan(c, pos, 1, True, True))
    gt_ref[...] = c
    rows = jnp.concatenate([bcum, c, bcum + cmax, cmax, jnp.zeros((LANES - 4 * nd, tm), F32)], axis=0)
    g_ref[...] = rows.T


def _chunk_scan(a, pos, axis, reverse, use_max):
    n = a.shape[axis]
    ident = -jnp.inf if use_max else 0.0
    sh = 1
    while sh < CHUNK:
        if reverse:
            moved = jnp.where(pos < CHUNK - sh, pltpu.roll(a, n - sh, axis=axis), ident)
        else:
            moved = jnp.where(pos >= sh, pltpu.roll(a, sh, axis=axis), ident)
        a = jnp.maximum(a, moved) if use_max else a + moved
        sh *= 2
    return a


def _inproj(x, mods, w, b, wgt, bgt, conv_w, conv_b, *, n_lat, seq):
    t, d = x.shape
    tm = ROW_TILE
    n_batch = mods.shape[0] - 1
    per_tile = tm // SUBLANES

    def mod_idx(i):
        return (jnp.minimum((i * tm) // seq, n_batch), 0, 0)

    const = lambda i: (0, 0)
    row = lambda i: (i, 0)
    parts = (("qk", _QK_WIDTH),) + _Z_PARTS
    out_shapes = [jax.ShapeDtypeStruct((t, width), BF16) for _, width in parts]
    out_specs = [pl.BlockSpec((tm, width), row) for _, width in parts]
    out_shapes += [jax.ShapeDtypeStruct((t, LANES), F32), jax.ShapeDtypeStruct((2 * N_HEADS, t), F32)]
    out_specs += [pl.BlockSpec((tm, LANES), row), pl.BlockSpec((2 * N_HEADS, tm), lambda i: (0, i))]
    return pl.pallas_call(
        functools.partial(_inproj_kernel, n_lat=n_lat, seq=seq),
        grid=(t // tm,),
        in_specs=[
            pl.BlockSpec((tm, d), row),
            pl.BlockSpec((SUBLANES, d), lambda i: (jnp.maximum(i * per_tile - 1, 0), 0)),
            pl.BlockSpec((SUBLANES, d), lambda i: (jnp.minimum((i + 1) * per_tile, t // SUBLANES - 1), 0)),
            pl.BlockSpec((None, 1, mods.shape[2]), mod_idx),
            pl.BlockSpec(w.shape, const),
            pl.BlockSpec(b.shape, const),
            pl.BlockSpec(wgt.shape, const),
            pl.BlockSpec(bgt.shape, const),
            pl.BlockSpec(conv_w.shape, const),
            pl.BlockSpec(conv_b.shape, const),
        ],
        out_specs=out_specs,
        out_shape=out_shapes,
        scratch_shapes=[pltpu.VMEM((2, tm + 2 * SUBLANES, _QK_WIDTH // 2), F32)],
        compiler_params=_cparams("parallel"),
    )(x, x, x, mods, w, b, wgt, bgt, conv_w, conv_b)


def _mlstm_row_block(j, b, direction, *, n_batch, seq, ctx_len):
    ncc = ctx_len // CHUNK
    ncl = seq // CHUNK
    is_ctx = j < ncc
    if direction == 0:
        c = jnp.where(is_ctx, j, j - ncc)
    else:
        c = jnp.where(is_ctx, ncc - 1 - j, ncl - 1 - (j - ncc))
    base = jnp.where(is_ctx, (n_batch * seq + b * ctx_len) // CHUNK, (b * seq) // CHUNK)
    return base + c


def _mlstm_kernel(q_f, k_f, v_f, g_f, gt_f, q_b, k_b, v_b, g_b, gt_b, hf_ref, hb_ref, s_sc, m_sc):
    j = pl.program_id(1)

    @pl.when(j == 0)
    def _():
        s_sc[...] = jnp.zeros_like(s_sc)
        m_sc[...] = jnp.zeros_like(m_sc)

    ri = lax.broadcasted_iota(jnp.int32, (CHUNK, CHUNK), 0)
    ci = lax.broadcasted_iota(jnp.int32, (CHUNK, CHUNK), 1)
    ones_tile = jnp.ones((CHUNK, HEAD_DIM), BF16)
    dirs = ((q_f, k_f, v_f, g_f, gt_f, hf_ref), (q_b, k_b, v_b, g_b, gt_b, hb_ref))
    combos = [(d, hh) for d in range(2) for hh in range(N_HEADS)]
    heads = lambda ref, hh: ref[:, hh * HEAD_DIM:(hh + 1) * HEAD_DIM]

    nd = 2 * N_HEADS
    tri = ((ci <= ri), (ci >= ri))
    g = (g_f[...], g_b[...])
    gt = (gt_f[...], gt_b[...])

    qk_s, v1, state, m_prev = {}, {}, {}, {}
    for d, hh in combos:
        q_ref, k_ref, v_ref = dirs[d][:3]
        qk_s[d, hh] = lax.dot_general(heads(q_ref, hh), heads(k_ref, hh), (((1,), (1,)), ((), ())),
                                      preferred_element_type=F32)
        v1[d, hh] = jnp.concatenate([heads(v_ref, hh), ones_tile], axis=1)
        state[d, hh] = s_sc[d, hh]
        m_prev[d, hh] = m_sc[d * N_HEADS + hh][0:1, 0:1]

    q_state = {}
    for d, hh in combos:
        q_state[d, hh] = jnp.dot(heads(dirs[d][0], hh), state[d, hh].astype(BF16), preferred_element_type=F32)

    m_t, decay, scores = {}, {}, {}
    for d, hh in combos:
        col = d * N_HEADS + hh
        bcum = g[d][:, col:col + 1]
        inter = bcum + m_prev[d, hh]
        m_t[d, hh] = jnp.maximum(inter, g[d][:, 2 * nd + col:2 * nd + col + 1])
        decay[d, hh] = jnp.exp(inter - m_t[d, hh])
        dlog = jnp.where(tri[d], (bcum - m_t[d, hh]) + gt[d][col:col + 1, :], -jnp.inf)
        scores[d, hh] = (qk_s[d, hh] * jnp.exp(dlog)).astype(BF16)

    for d, hh in combos:
        numden = decay[d, hh] * q_state[d, hh] + jnp.dot(scores[d, hh], v1[d, hh], preferred_element_type=F32)
        num = numden[:, :HEAD_DIM]
        den = numden[:, HEAD_DIM:]
        out = num / jnp.maximum(jnp.abs(den), jnp.exp(-m_t[d, hh]))
        out_ref = dirs[d][5]
        out_ref[:, hh * HEAD_DIM:(hh + 1) * HEAD_DIM] = out.astype(out_ref.dtype)

    kw, a, m_new = {}, {}, {}
    for d, hh in combos:
        col = d * N_HEADS + hh
        last = CHUNK - 1 if d == 0 else 0
        total = g[d][last:last + 1, col:col + 1]
        m_new[d, hh] = total + jnp.maximum(m_prev[d, hh], g[d][last:last + 1, 3 * nd + col:3 * nd + col + 1])
        a[d, hh] = jnp.exp(total + m_prev[d, hh] - m_new[d, hh])
        wk = jnp.exp(total + g[d][:, nd + col:nd + col + 1] - m_new[d, hh])
        kw[d, hh] = (heads(dirs[d][1], hh).astype(F32) * wk).astype(BF16)

    for d, hh in combos:
        upd = lax.dot_general(kw[d, hh], v1[d, hh], (((0,), (0,)), ((), ())), preferred_element_type=F32)
        s_sc[d, hh] = a[d, hh] * state[d, hh] + upd
        m_sc[d * N_HEADS + hh] = jnp.broadcast_to(m_new[d, hh], m_sc.shape[1:])


def _mlstm(zqk, zv, g, gt, *, n_batch, seq, ctx_len):
    t = zqk.shape[0]
    nsteps = (seq + ctx_len) // CHUNK
    width = N_HEADS * HEAD_DIM
    idx = functools.partial(_mlstm_row_block, n_batch=n_batch, seq=seq, ctx_len=ctx_len)

    def specs(d):
        rb = lambda b, j: idx(j, b, d)
        return [
            pl.BlockSpec((CHUNK, width), lambda b, j: (rb(b, j), 0)),
            pl.BlockSpec((CHUNK, width), lambda b, j: (rb(b, j), 1)),
            pl.BlockSpec((CHUNK, width), lambda b, j: (rb(b, j), 0)),
            pl.BlockSpec((CHUNK, LANES), lambda b, j: (rb(b, j), 0)),
            pl.BlockSpec((2 * N_HEADS, CHUNK), lambda b, j: (0, rb(b, j))),
        ]

    out_spec = lambda d: pl.BlockSpec((CHUNK, width), lambda b, j: (idx(j, b, d), 0))
    return pl.pallas_call(
        _mlstm_kernel,
        grid=(n_batch, nsteps),
        in_specs=specs(0) + specs(1),
        out_specs=[out_spec(0), out_spec(1)],
        out_shape=[jax.ShapeDtypeStruct((t, width), BF16)] * 2,
        scratch_shapes=[
            pltpu.VMEM((2, N_HEADS, HEAD_DIM, 2 * HEAD_DIM), F32),
            pltpu.VMEM((2 * N_HEADS, 8, LANES), F32),
        ],
        compiler_params=_cparams("parallel", "arbitrary"),
    )(zqk, zqk, zv, g, gt, zqk, zqk, zv, g, gt)


def _pool_constants(ctx_len):
    tm = ROW_TILE
    masks = np.zeros((2, len(POOL_WINDOWS), tm, tm), np.float32)
    cnts = np.zeros((2, tm, len(POOL_WINDOWS) * POOL_GROUP_DIM), np.float32)
    for kind, length in enumerate((GRID_W, ctx_len)):
        for gi, w in enumerate(POOL_WINDOWS):
            for t in range(tm):
                base = (t // length) * length
                tl = t - base
                lo = min(max(tl - w // 2, 0), length)
                hi = min(max(tl + w // 2, 0), length)
                masks[kind, gi, t, base + lo:base + hi] = 1.0
                cnts[kind, t, gi * POOL_GROUP_DIM:(gi + 1) * POOL_GROUP_DIM] = hi - lo
    return jnp.asarray(masks, BF16), jnp.asarray(cnts, F32)


def _mix_kernel(x_ref, zp_ref, zuv_ref, zmg_ref, zo_ref, hf_ref, hb_ref, mod_ref,
                pm_ref, cnt_ref, pw_ref, ps_ref, sg_ref, sb_ref, sw_ref, sbm_ref, ng_ref,
                wbp_ref, wbm_ref, wbs_ref, wo_ref, l1g_ref, l1b_ref, wr_ref, br_ref,
                x1_ref, h2_ref, route_ref, cnt_out_ref, cnt_sc, *, alpha):
    tm, d = x_ref.shape
    i = pl.program_id(0)

    @pl.when(i == 0)
    def _():
        cnt_sc[...] = jnp.zeros_like(cnt_sc)

    lane256 = lax.broadcasted_iota(jnp.int32, (1, 4 * POOL_GROUP_DIM), 1) // POOL_GROUP_DIM

    xp = zp_ref[...]
    xp_f = xp.astype(F32)
    pooled = jnp.zeros(xp_f.shape, F32)
    for gi in range(len(POOL_WINDOWS)):
        s = jnp.dot(pm_ref[gi], xp, preferred_element_type=F32)
        pooled = jnp.where(lane256 == gi, s, pooled)
    diff = pooled / cnt_ref[...] - xp_f
    pool_o = jnp.dot(diff.astype(BF16), pw_ref[...], preferred_element_type=F32) * ps_ref[...]

    uv = zuv_ref[...].astype(F32)
    uv = 0.5 * uv * (1.0 + lax.erf(uv * (2.0 ** -0.5)))
    half = uv.shape[1] // 2
    u = uv[:, :half]
    vn = (_ln_plain(uv[:, half:]) * sg_ref[...] + sb_ref[...]).astype(BF16)
    mixed_chunks = []
    for c0 in range(0, tm, CHUNK):
        vc = vn[c0:c0 + CHUNK, :]
        mixed = jnp.zeros((CHUNK, half), F32)
        for gi in range(sw_ref.shape[0]):
            s = jnp.dot(sw_ref[gi], vc, preferred_element_type=F32)
            mixed = jnp.where(lane256 == gi, s, mixed)
        mixed_chunks.append(mixed + sbm_ref[...])
    sgu_o = u * jnp.concatenate(mixed_chunks, axis=0)

    hsum = hf_ref[...].astype(F32) + hb_ref[...].astype(F32)
    heads = []
    for hh in range(N_HEADS):
        hv = hsum[:, hh * HEAD_DIM:(hh + 1) * HEAD_DIM]
        mu = jnp.mean(hv, axis=-1, keepdims=True)
        hc = hv - mu
        var = jnp.mean(hc * hc, axis=-1, keepdims=True)
        heads.append(hc * lax.rsqrt(var + HEAD_NORM_EPS))
    hn = jnp.concatenate(heads, axis=1)
    twice_gate = lambda zh: jnp.tanh(zh.astype(F32)) + 1.0
    mlstm_o = hn * ng_ref[...] * twice_gate(zo_ref[...])

    y = twice_gate(zmg_ref[:, 0:d]) * jnp.dot(pool_o.astype(BF16), wbp_ref[...], preferred_element_type=F32)
    y += twice_gate(zmg_ref[:, d:2 * d]) * jnp.dot(mlstm_o.astype(BF16), wbm_ref[...],
                                                   preferred_element_type=F32)
    y += twice_gate(zmg_ref[:, 2 * d:3 * d]) * jnp.dot(sgu_o.astype(BF16), wbs_ref[...],
                                                       preferred_element_type=F32)
    y2 = jnp.dot(y.astype(BF16), wo_ref[...], preferred_element_type=F32)

    g1 = mod_ref[:, 2 * d:3 * d]
    x1 = _ln_plain(alpha * x_ref[...] + g1 * y2) * l1g_ref[...] + l1b_ref[...]
    x1_ref[...] = x1
    h2 = _ln_plain(x1) * (1.0 + mod_ref[:, 4 * d:5 * d]) + mod_ref[:, 3 * d:4 * d]
    h2b = h2.astype(BF16)
    h2_ref[...] = _pack_rows(h2b)

    logits = jnp.dot(h2b, wr_ref[...], preferred_element_type=F32) + br_ref[...]
    lane = lax.broadcasted_iota(jnp.int32, logits.shape, 1)
    work = logits
    vals, idxs, hots = [], [], []
    for _ in range(TOP_K):
        mval = jnp.max(work, axis=-1, keepdims=True)
        idx = jnp.min(jnp.where(work == mval, lane, LANES), axis=-1, keepdims=True)
        hot = lane == idx
        vals.append(mval)
        idxs.append(idx)
        hots.append(hot)
        work = jnp.where(hot, -jnp.inf, work)
    exps = [jnp.exp(v - vals[0]) for v in vals]
    denom = exps[0] + exps[1] + exps[2] + exps[3]
    sel = (hots[0] | hots[1] | hots[2] | hots[3])
    r_i = lax.broadcasted_iota(jnp.int32, (tm, tm), 0)
    c_i = lax.broadcasted_iota(jnp.int32, (tm, tm), 1)
    before = (c_i < r_i).astype(BF16)
    sel_b = sel.astype(F32).astype(BF16)
    rank = jnp.dot(before, sel_b, preferred_element_type=F32) + cnt_sc[0:1, :]
    route = jnp.zeros(logits.shape, F32)
    for kk in range(TOP_K):
        rk = jnp.sum(jnp.where(hots[kk], rank, 0.0), axis=-1, keepdims=True)
        route = jnp.where(lane == kk, idxs[kk].astype(F32), route)
        route = jnp.where(lane == TOP_K + kk, exps[kk] / denom, route)
        route = jnp.where(lane == 2 * TOP_K + kk, rk, route)
    route_ref[...] = route
    new_cnt = cnt_sc[0:1, :] + jnp.sum(sel.astype(F32), axis=0, keepdims=True)
    cnt_sc[...] = jnp.broadcast_to(new_cnt, cnt_sc.shape)
    cnt_out_ref[...] = jnp.broadcast_to(new_cnt, cnt_out_ref.shape)


def _mix(x, zp, zuv, zmg, zo, hf, hb, mods, consts, *, n_rows, n_lat, seq, alpha):
    t, d = x.shape
    tm = ROW_TILE
    n_batch = mods.shape[0] - 1
    row = lambda i: (i, 0)
    const2 = lambda i: (0, 0)
    const3 = lambda i: (0, 0, 0)
    kind = lambda i: jnp.where(i * tm >= n_lat, 1, 0)

    def mod_idx(i):
        return (jnp.minimum((i * tm) // seq, n_batch), 0, 0)

    (pm, cnt, pw, ps, sg, sb, sw, sbm, ng, wbp, wbm, wbs, wo, l1g, l1b, wr, br) = consts
    in_specs = [
        pl.BlockSpec((tm, d), row),
        pl.BlockSpec((tm, zp.shape[1]), row),
        pl.BlockSpec((tm, zuv.shape[1]), row),
        pl.BlockSpec((tm, zmg.shape[1]), row),
        pl.BlockSpec((tm, zo.shape[1]), row),
        pl.BlockSpec((tm, hf.shape[1]), row),
        pl.BlockSpec((tm, hb.shape[1]), row),
        pl.BlockSpec((None, 1, mods.shape[2]), mod_idx),
        pl.BlockSpec((None,) + pm.shape[1:], lambda i: (kind(i), 0, 0, 0)),
        pl.BlockSpec((None,) + cnt.shape[1:], lambda i: (kind(i), 0, 0)),
        pl.BlockSpec(pw.shape, const2),
        pl.BlockSpec(ps.shape, const2),
        pl.BlockSpec(sg.shape, const2),
        pl.BlockSpec(sb.shape, const2),
        pl.BlockSpec(sw.shape, const3),
        pl.BlockSpec(sbm.shape, const2),
        pl.BlockSpec(ng.shape, const2),
        pl.BlockSpec(wbp.shape, const2),
        pl.BlockSpec(wbm.shape, const2),
        pl.BlockSpec(wbs.shape, const2),
        pl.BlockSpec(wo.shape, const2),
        pl.BlockSpec(l1g.shape, const2),
        pl.BlockSpec(l1b.shape, const2),
        pl.BlockSpec(wr.shape, const2),
        pl.BlockSpec(br.shape, const2),
    ]
    return pl.pallas_call(
        functools.partial(_mix_kernel, alpha=alpha),
        grid=(n_rows // tm,),
        in_specs=in_specs,
        out_specs=[
            pl.BlockSpec((tm, d), row),
            pl.BlockSpec((tm, d // 2), row),
            pl.BlockSpec((tm, LANES), row),
            pl.BlockSpec((8, LANES), const2),
        ],
        out_shape=[
            jax.ShapeDtypeStruct((n_rows, d), F32),
            jax.ShapeDtypeStruct((n_rows, d // 2), jnp.uint32),
            jax.ShapeDtypeStruct((n_rows, LANES), F32),
            jax.ShapeDtypeStruct((8, LANES), F32),
        ],
        scratch_shapes=[pltpu.VMEM((8, LANES), F32)],
        compiler_params=_cparams("arbitrary"),
    )(x, zp, zuv, zmg, zo, hf, hb, mods, pm, cnt, pw, ps, sg, sb, sw, sbm, ng,
      wbp, wbm, wbs, wo, l1g, l1b, wr, br)


SCATTER_WINDOW = 128
SCATTER_PARTS = 2


def _scatter_rows(x, idx, n_out, part):
    n = x.shape[0]
    w = x.shape[1] // SCATTER_PARTS
    n_idx = idx.shape[0]
    mesh = plsc.VectorSubcoreMesh(core_axis_name="core", subcore_axis_name="subcore")

    @pl.kernel(out_type=jax.ShapeDtypeStruct((n_out, w), x.dtype), mesh=mesh, scratch_types=[])
    def scatter_kernel(x_hbm, i_hbm, o_hbm):
        def body(x_vmem, i_vmem):
            for k in range(n_idx):
                pltpu.sync_copy(x_vmem, o_hbm.at[i_vmem.at[k]])

        pltpu.emit_pipeline(
            body,
            grid=(n // SCATTER_WINDOW,),
            in_specs=[pl.BlockSpec((SCATTER_WINDOW, w), lambda i: (i, part)),
                      pl.BlockSpec((n_idx, SCATTER_WINDOW), lambda i: (0, i))],
            out_specs=[],
            core_axis_name=("core", "subcore"),
            dimension_semantics=(pltpu.PARALLEL,),
        )(x_hbm, i_hbm)

    return scatter_kernel(x, idx)


def _gather_rows(x, idx):
    w = x.shape[1]
    n_idx = idx.shape[1]
    mesh = plsc.VectorSubcoreMesh(core_axis_name="core", subcore_axis_name="subcore")

    @pl.kernel(out_type=jax.ShapeDtypeStruct((n_idx, w), x.dtype), mesh=mesh, scratch_types=[])
    def gather_kernel(x_hbm, i_hbm, o_hbm):
        def body(i_vmem, o_vmem):
            pltpu.sync_copy(x_hbm.at[i_vmem.at[0]], o_vmem)

        pltpu.emit_pipeline(
            body,
            grid=(n_idx // SCATTER_WINDOW,),
            in_specs=[pl.BlockSpec((1, SCATTER_WINDOW), lambda i: (0, i))],
            out_specs=[pl.BlockSpec((SCATTER_WINDOW, w), lambda i: (i, 0))],
            core_axis_name=("core", "subcore"),
            dimension_semantics=(pltpu.PARALLEL,),
        )(i_hbm, o_hbm)

    return gather_kernel(x, idx)


def _expert_kernel(te_ref, nt_ref, xa_ref, xb_ref, wgu_ref, bgu_ref, wd_ref, bd_ref, ya_ref, yb_ref,
                   wgu_sc, wd_sc):
    i = pl.program_id(0)
    f = wd_ref.shape[0]
    used = i < nt_ref[0]

    @pl.when(jnp.logical_and(used, jnp.logical_or(i == 0, te_ref[i] != te_ref[jnp.maximum(i - 1, 0)])))
    def _():
        for c0 in range(0, 2 * f, 512):
            wgu_sc[:, c0:c0 + 512] = wgu_ref[:, c0:c0 + 512].astype(BF16)
        for c0 in range(0, f, 256):
            wd_sc[c0:c0 + 256, :] = wd_ref[c0:c0 + 256, :].astype(BF16)

    @pl.when(used)
    def _():
        x = _unpack_rows([xa_ref[...], xb_ref[...]]).astype(BF16)
        acc = jnp.zeros((x.shape[0], wd_ref.shape[1]), F32)
        for c0 in range(0, f, 512):
            gate = jnp.dot(x, wgu_sc[:, c0:c0 + 512], preferred_element_type=F32) + bgu_ref[:, c0:c0 + 512]
            up = (jnp.dot(x, wgu_sc[:, f + c0:f + c0 + 512], preferred_element_type=F32)
                  + bgu_ref[:, f + c0:f + c0 + 512])
            gate = jnp.minimum(gate, SWIGLU_LIMIT)
            up = jnp.clip(up, -SWIGLU_LIMIT, SWIGLU_LIMIT)
            glu = gate * _sigmoid(SWIGLU_ALPHA * gate)
            act = ((up + 1.0) * glu).astype(BF16)
            acc += jnp.dot(act, wd_sc[c0:c0 + 512, :], preferred_element_type=F32)
        words = _pack_rows(acc + bd_ref[...])
        ya_ref[...] = words[:, :ya_ref.shape[1]]
        yb_ref[...] = words[:, ya_ref.shape[1]:]

    @pl.when(jnp.logical_not(used))
    def _():
        ya_ref[...] = jnp.zeros_like(ya_ref)
        yb_ref[...] = jnp.zeros_like(yb_ref)


def _experts(tile_expert, n_tiles_used, xa, xb, wgu, bgu, wd, bd, *, layer):
    p = xa.shape[0]
    d = 2 * SCATTER_PARTS * xa.shape[1]
    tm = EXPERT_TILE
    f2 = wgu.shape[3]
    grid_spec = pltpu.PrefetchScalarGridSpec(
        num_scalar_prefetch=2,
        grid=(p // tm,),
        in_specs=[
            pl.BlockSpec((tm, xa.shape[1]), lambda i, te, nt: (i, 0)),
            pl.BlockSpec((tm, xb.shape[1]), lambda i, te, nt: (i, 0)),
            pl.BlockSpec((None, None, d, f2), lambda i, te, nt: (layer, te[i], 0, 0)),
            pl.BlockSpec((None, None, 1, f2), lambda i, te, nt: (layer, te[i], 0, 0)),
            pl.BlockSpec((None, None, f2 // 2, d), lambda i, te, nt: (layer, te[i], 0, 0)),
            pl.BlockSpec((None, None, 1, d), lambda i, te, nt: (layer, te[i], 0, 0)),
        ],
        out_specs=[pl.BlockSpec((tm, xa.shape[1]), lambda i, te, nt: (i, 0)),
                   pl.BlockSpec((tm, xb.shape[1]), lambda i, te, nt: (i, 0))],
        scratch_shapes=[pltpu.VMEM((d, f2), BF16), pltpu.VMEM((f2 // 2, d), BF16)],
    )
    return pl.pallas_call(
        _expert_kernel,
        grid_spec=grid_spec,
        out_shape=[jax.ShapeDtypeStruct(xa.shape, xa.dtype), jax.ShapeDtypeStruct(xb.shape, xb.dtype)],
        compiler_params=_cparams("arbitrary"),
    )(tile_expert, n_tiles_used, xa, xb, wgu, bgu, wd, bd)


def _combine_kernel(x1_ref, ya_ref, yb_ref, route_ref, mod_ref, g_ref, b_ref, *rest, alpha):
    x2_ref = rest[-1]
    d = x1_ref.shape[1]
    route = route_ref[...]
    f = jnp.zeros(x1_ref.shape, F32)
    for kk in range(TOP_K):
        f += route[:, TOP_K + kk:TOP_K + kk + 1] * _unpack_rows([ya_ref[kk], yb_ref[kk]])
    g2 = mod_ref[:, 5 * d:6 * d]
    x2_ref[...] = _ln_plain(alpha * x1_ref[...] + g2 * f) * g_ref[...] + b_ref[...]


def _combine(x1, yg_a, yg_b, route, mods, g, b, *, seq, alpha, dest=None, dest_rows=None, tile_offset=0):
    t, d = x1.shape
    tm = ROW_TILE
    n_batch = mods.shape[0] - 1
    row = lambda i: (i, 0)

    def mod_idx(i):
        return (jnp.minimum((i * tm) // seq, n_batch), 0, 0)

    in_specs = [
        pl.BlockSpec((tm, d), row),
        pl.BlockSpec((TOP_K, tm, yg_a.shape[2]), lambda i: (0, i, 0)),
        pl.BlockSpec((TOP_K, tm, yg_b.shape[2]), lambda i: (0, i, 0)),
        pl.BlockSpec((tm, LANES), row),
        pl.BlockSpec((None, 1, mods.shape[2]), mod_idx),
        pl.BlockSpec(g.shape, lambda i: (0, 0)),
        pl.BlockSpec(b.shape, lambda i: (0, 0)),
    ]
    args = [x1, yg_a, yg_b, route, mods, g, b]
    aliases = {}
    if dest is not None:
        in_specs.append(pl.BlockSpec(memory_space=pl.ANY))
        args.append(dest)
        aliases = {len(args) - 1: 0}
    return pl.pallas_call(
        functools.partial(_combine_kernel, alpha=alpha),
        grid=(t // tm,),
        in_specs=in_specs,
        out_specs=pl.BlockSpec((tm, d), lambda i: (i + tile_offset, 0)),
        out_shape=jax.ShapeDtypeStruct((t if dest_rows is None else dest_rows, d), F32),
        input_output_aliases=aliases,
        compiler_params=_cparams("parallel"),
    )(*args)


def kernel(x, c, ctx, c_ctx, w_ada, b_ada, w_in, b_in, pool_w, pool_scale, qk_conv_w, qk_conv_b,
           mlstm_norm_g, sgu_ln_g, sgu_ln_b, sgu_w, sgu_b, w_br_pool, w_br_mlstm, w_br_sgu, w_out,
           ln1_g, ln1_b, w_router, b_router, w_gate_up, b_gate_up, w_down, b_down, ln2_g, ln2_b):
    n_batch, seq, d = x.shape
    ctx_len = ctx.shape[1]
    depth = w_in.shape[0]
    n_exp = w_router.shape[2]
    assert ctx_len == ROW_TILE and seq % ROW_TILE == 0 and ROW_TILE % GRID_W == 0
    assert n_batch % N_STREAMS == 0
    alpha = float((2 * depth) ** 0.25)

    rows = ((n_batch + 1 + 7) // 8) * 8
    cc = jnp.zeros((rows, d), F32).at[:n_batch].set(c).at[n_batch].set(c_ctx)
    mods_all = _ada_table(cc, w_ada, b_ada)[:, :n_batch + 1].reshape(depth, n_batch + 1, 1, 6 * d)

    pool_w_, mlw = 256, 512
    col = lambda a, lo, hi: a[..., lo:hi]
    g_lo = pool_w_ + 4 * mlw
    parts = ((pool_w_, pool_w_ + 2 * mlw), (pool_w_ + 2 * mlw, pool_w_ + 3 * mlw), (pool_w_ + 3 * mlw, g_lo),
             (g_lo + _N_GATE_COLS, g_lo + _N_GATE_COLS + 512), (0, pool_w_),
             (g_lo + _N_GATE_COLS + 512, w_in.shape[2]))
    part_scale = (1.0, 1.0, 0.5, 1.0, 1.0, 0.5)
    w_main = jnp.concatenate([col(w_in, lo, hi) * sc for (lo, hi), sc in zip(parts, part_scale)],
                             axis=-1).astype(BF16)
    b_main = jnp.concatenate([col(b_in, lo, hi) * sc for (lo, hi), sc in zip(parts, part_scale)],
                             axis=-1)[:, None, :]
    gw, gb = col(w_in, g_lo, g_lo + _N_GATE_COLS), col(b_in, g_lo, g_lo + _N_GATE_COLS)
    nh = N_HEADS
    pick = lambda a, k: jnp.concatenate([a[..., k * nh:(k + 1) * nh], a[..., (k + 2) * nh:(k + 3) * nh]], axis=-1)
    w_gate_t = jnp.swapaxes(jnp.concatenate([pick(gw, 0), pick(gw, 1)], axis=-1), 1, 2).astype(BF16)
    b_gate_t = jnp.concatenate([pick(gb, 0), pick(gb, 1)], axis=-1)[:, :, None]

    pm, cnt = _pool_constants(ctx_len)
    eye = jnp.eye(len(POOL_WINDOWS), dtype=F32)
    pool_bd = jnp.einsum('lgcd,gh->lgchd', pool_w, eye).reshape(depth, 256, 256).astype(BF16)
    sgu_bias_map = jnp.repeat(jnp.swapaxes(sgu_b, 1, 2), sgu_ln_g.shape[1] // sgu_b.shape[1], axis=2)
    w_router_p = jnp.pad(w_router, ((0, 0), (0, 0), (0, LANES - n_exp))).astype(BF16)
    b_router_p = jnp.pad(b_router, ((0, 0), (0, LANES - n_exp)), constant_values=NEG_BIG)[:, None, :]
    r1 = lambda a: a[:, None, :]
    b_gu4 = b_gate_up[:, :, None, :]
    b_d4 = b_down[:, :, None, :]

    nb = n_batch // N_STREAMS
    n_lat = nb * seq
    t_all = n_lat + nb * ctx_len
    streams, stream_mods = [], []
    for h in range(N_STREAMS):
        sl = slice(h * nb, (h + 1) * nb)
        streams.append(jnp.concatenate([x[sl].reshape(n_lat, d), ctx[sl].reshape(nb * ctx_len, d)], axis=0))
        stream_mods.append(jnp.concatenate([mods_all[:, sl], mods_all[:, n_batch:n_batch + 1]], axis=1))
    out = None
    for i in range(depth):
        last = i == depth - 1
        for h in range(N_STREAMS):
            stream = streams[h]
            mods = stream_mods[h][i]
            zqk, zv, zo, zuv, zp, zmg, g, gt = _inproj(
                stream, mods, w_main[i], b_main[i], w_gate_t[i], b_gate_t[i],
                qk_conv_w[i], r1(qk_conv_b)[i], n_lat=n_lat, seq=seq)
            hf, hb = _mlstm(zqk, zv, g, gt, n_batch=nb, seq=seq, ctx_len=ctx_len)
            n_rows = n_lat if last else t_all
            consts = (pm, cnt, pool_bd[i], r1(pool_scale)[i], r1(sgu_ln_g)[i], r1(sgu_ln_b)[i],
                      sgu_w[i].astype(BF16), sgu_bias_map[i], r1(0.5 * mlstm_norm_g)[i],
                      w_br_pool[i].astype(BF16), w_br_mlstm[i].astype(BF16), w_br_sgu[i].astype(BF16),
                      (0.5 * w_out[i]).astype(BF16), r1(ln1_g)[i], r1(ln1_b)[i], w_router_p[i], b_router_p[i])
            x1, h2, route, counts = _mix(stream, zp, zuv, zmg, zo, hf, hb, mods, consts,
                                         n_rows=n_rows, n_lat=n_lat, seq=seq, alpha=alpha)

            cnt_e = counts[0, :n_exp].astype(jnp.int32)
            padded = ((cnt_e + EXPERT_TILE - 1) // EXPERT_TILE) * EXPERT_TILE
            ends = jnp.cumsum(padded)
            starts = ends - padded
            e_idx = route[:, 0:TOP_K].astype(jnp.int32)
            rank = route[:, 2 * TOP_K:3 * TOP_K].astype(jnp.int32)
            pos_t = (starts[e_idx] + rank).T
            p_rows = n_rows * TOP_K + n_exp * EXPERT_TILE
            n_tiles = p_rows // EXPERT_TILE
            tile_start = jnp.arange(n_tiles, dtype=jnp.int32) * EXPERT_TILE
            tile_expert = jnp.minimum(
                jnp.sum((ends[None, :] <= tile_start[:, None]).astype(jnp.int32), axis=1), n_exp - 1)
            n_used = (ends[-1] // EXPERT_TILE).astype(jnp.int32).reshape(1)
            xs_a = _scatter_rows(h2, pos_t, p_rows, 0)
            xs_b = _scatter_rows(h2, pos_t, p_rows, 1)
            ys_a, ys_b = _experts(tile_expert, n_used, xs_a, xs_b, w_gate_up, b_gu4, w_down, b_d4, layer=i)
            pos_flat = pos_t.reshape(1, TOP_K * n_rows)
            yg_a = _gather_rows(ys_a, pos_flat).reshape(TOP_K, n_rows, -1)
            yg_b = _gather_rows(ys_b, pos_flat).reshape(TOP_K, n_rows, -1)
            if last:
                out = _combine(x1, yg_a, yg_b, route, mods, r1(ln2_g)[i], r1(ln2_b)[i], seq=seq, alpha=alpha,
                               dest=out, dest_rows=n_batch * seq, tile_offset=h * n_lat // ROW_TILE)
            else:
                streams[h] = _combine(x1, yg_a, yg_b, route, mods, r1(ln2_g)[i], r1(ln2_b)[i],
                                      seq=seq, alpha=alpha)
    return out.reshape(n_batch, seq, d)
```

```python
import functools

import numpy as np
import jax
import jax.numpy as jnp
from jax import lax
from jax.experimental import pallas as pl
from jax.experimental.pallas import tpu as pltpu
from jax.experimental.pallas import tpu_sc as plsc

F32 = jnp.float32
BF16 = jnp.bfloat16

GRID_W = 64
POOL_WINDOWS = (2, 4, 8, 16)
POOL_GROUP_DIM = 64
N_HEADS = 4
HEAD_DIM = 128
CHUNK = 128
TOP_K = 4
SWIGLU_LIMIT = 7.0
SWIGLU_ALPHA = 1.702
LN_EPS = 1e-5
HEAD_NORM_EPS = 1e-6

ROW_TILE = 256
INPROJ_TILE = 512
COMBINE_TILE = 512
SEAM_ROWS = 16
EXPERT_TILE = 512
N_STREAMS = 2
SUBLANES = 8
LANES = 128
NEG_BIG = -1e30
VMEM_LIMIT = 56 * 1024 * 1024


def _cparams(*sem):
    return pltpu.CompilerParams(dimension_semantics=sem, vmem_limit_bytes=VMEM_LIMIT)


def _ln_plain(x):
    mu = jnp.mean(x, axis=-1, keepdims=True)
    xc = x - mu
    var = jnp.mean(xc * xc, axis=-1, keepdims=True)
    return xc * lax.rsqrt(var + LN_EPS)


def _sigmoid(x):
    return 0.5 * jnp.tanh(0.5 * x) + 0.5


def _log_sigmoid(x):
    return jnp.minimum(x, 0.0) - jnp.log1p(jnp.exp(-jnp.abs(x)))


def _pack_rows(x):
    half = x.shape[1] // 2
    bits = lax.bitcast_convert_type(x.astype(BF16).astype(F32), jnp.uint32)
    return (bits[:, half:] & jnp.uint32(0xFFFF0000)) | (bits[:, :half] >> 16)


def _unpack_rows(parts):
    low = [lax.bitcast_convert_type(w << 16, F32) for w in parts]
    high = [lax.bitcast_convert_type(w & jnp.uint32(0xFFFF0000), F32) for w in parts]
    return jnp.concatenate(low + high, axis=1)


def _ada_kernel(c_ref, w_ref, b_ref, o_ref):
    c = c_ref[...]
    s = c * _sigmoid(c)
    o_ref[...] = jnp.dot(s, w_ref[...], preferred_element_type=F32,
                         precision=lax.Precision.HIGHEST) + b_ref[...]


def _ada_table(cc, w_ada, b_ada):
    depth, d, n = w_ada.shape
    rows = cc.shape[0]
    bn = 1536
    return pl.pallas_call(
        _ada_kernel,
        grid=(depth, n // bn),
        in_specs=[
            pl.BlockSpec((rows, d), lambda l, j: (0, 0)),
            pl.BlockSpec((None, d, bn), lambda l, j: (l, 0, j)),
            pl.BlockSpec((None, 1, bn), lambda l, j: (l, 0, j)),
        ],
        out_specs=pl.BlockSpec((None, rows, bn), lambda l, j: (l, 0, j)),
        out_shape=jax.ShapeDtypeStruct((depth, rows, n), F32),
        compiler_params=_cparams("parallel", "parallel"),
    )(cc, w_ada, b_ada.reshape(depth, 1, n))


_QK_WIDTH = 2 * N_HEADS * HEAD_DIM
_Z_PARTS = (("v", 512), ("o", 512), ("uv", 512), ("pool", 256), ("mg", 3072))
_N_GATE_COLS = 4 * N_HEADS


def _inproj_kernel(x_ref, xp_ref, xn_ref, mod_ref, w_ref, b_ref, wgt_ref, bgt_ref,
                   cw_ref, cb_ref, qk_ref, v_ref, o_ref, uv_ref, p_ref, mg_ref, g_ref, gt_ref, pad_sc,
                   *, n_lat, seq, ctx_len):
    tm, d = x_ref.shape
    i = pl.program_id(0)
    is_lat = i * tm < n_lat
    has_prev = jnp.logical_and(is_lat, (i * tm) % seq != 0).astype(F32)
    has_next = jnp.logical_and(is_lat, ((i + 1) * tm) % seq != 0).astype(F32)

    xe = jnp.concatenate([xp_ref[...], x_ref[...], xn_ref[...]], axis=0)
    he = (_ln_plain(xe) * (1.0 + mod_ref[:, d:2 * d]) + mod_ref[:, 0:d]).astype(BF16)
    hb = he[SUBLANES:SUBLANES + tm, :]

    half = _QK_WIDTH // 2

    def qk_project(part):
        c0 = part * half
        z = jnp.dot(he, w_ref[:, c0:c0 + half], preferred_element_type=F32) + b_ref[:, c0:c0 + half]
        pad_sc[part] = z
        pad_sc[part, SUBLANES - 1:SUBLANES, :] = pad_sc[part, SUBLANES - 1:SUBLANES, :] * has_prev
        pad_sc[part, SUBLANES + tm:SUBLANES + tm + 1, :] = (
            pad_sc[part, SUBLANES + tm:SUBLANES + tm + 1, :] * has_next)

    def conv_act(part, prev, cur, nxt):
        c0 = part * half
        y = (cb_ref[:, c0:c0 + half] + prev * cw_ref[0:1, c0:c0 + half]
             + cur * cw_ref[1:2, c0:c0 + half] + nxt * cw_ref[2:3, c0:c0 + half])
        act = y * _sigmoid(y)
        if part == 1:
            act = act * (HEAD_DIM ** -0.5)
        return act.astype(qk_ref.dtype)

    def qk_conv(part):
        qk_ref[:, part * half:(part + 1) * half] = conv_act(
            part, pad_sc[part, SUBLANES - 1:SUBLANES - 1 + tm, :], pad_sc[part, SUBLANES:SUBLANES + tm, :],
            pad_sc[part, SUBLANES + 1:SUBLANES + 1 + tm, :])

    def plain(ref, off, c0, cw):
        z = jnp.dot(hb, w_ref[:, off + c0:off + c0 + cw], preferred_element_type=F32)
        ref[:, c0:c0 + cw] = (z + b_ref[:, off + c0:off + c0 + cw]).astype(ref.dtype)

    mxu_tasks = []
    off = _QK_WIDTH
    for ref, (_, width) in zip((v_ref, o_ref, uv_ref, p_ref, mg_ref), _Z_PARTS):
        for c0 in range(0, width, 512):
            mxu_tasks.append(functools.partial(plain, ref, off, c0, min(512, width - c0)))
        off += width
    vpu_tasks = {2: functools.partial(_gate_prep, hb, wgt_ref, bgt_ref, g_ref, gt_ref),
                 4: functools.partial(qk_conv, 0), 6: functools.partial(qk_conv, 1)}
    qk_project(0)
    qk_project(1)
    for n, task in enumerate(mxu_tasks):
        if n in vpu_tasks:
            vpu_tasks[n]()
        task()

    if tm > ctx_len:
        @pl.when(jnp.logical_not(is_lat))
        def _():
            r = lax.broadcasted_iota(jnp.int32, (2 * SEAM_ROWS, 1), 0)
            for part in range(2):
                for p in range(ctx_len, tm, ctx_len):
                    lo = SUBLANES + p - SEAM_ROWS
                    prev = jnp.where(r == SEAM_ROWS, 0.0, pad_sc[part, lo - 1:lo - 1 + 2 * SEAM_ROWS, :])
                    nxt = jnp.where(r == SEAM_ROWS - 1, 0.0, pad_sc[part, lo + 1:lo + 1 + 2 * SEAM_ROWS, :])
                    qk_ref[p - SEAM_ROWS:p + SEAM_ROWS, part * half:(part + 1) * half] = conv_act(
                        part, prev, pad_sc[part, lo:lo + 2 * SEAM_ROWS, :], nxt)


def _gate_prep(hb, wgt_ref, bgt_ref, g_ref, gt_ref):
    tm = hb.shape[0]
    nd = 2 * N_HEADS
    ggt = lax.dot_general(wgt_ref[...], hb, (((1,), (1,)), ((), ())),
                          preferred_element_type=F32) + bgt_ref[...]
    git, gft = ggt[:nd, :], ggt[nd:, :]
    fwd = lax.broadcasted_iota(jnp.int32, (nd, tm), 0) < N_HEADS
    pos = lax.broadcasted_iota(jnp.int32, (nd, tm), 1) % CHUNK
    lsg = _log_sigmoid(gft)
    bcum = jnp.where(fwd, _chunk_scan(lsg, pos, 1, False, False), _chunk_scan(lsg, pos, 1, True, False))
    c = git - bcum
    cmax = jnp.where(fwd, _chunk_scan(c, pos, 1, False, True), _chunk_scan(c, pos, 1, True, True))
    gt_ref[...] = c
    rows = jnp.concatenate([bcum, c, bcum + cmax, cmax, jnp.zeros((LANES - 4 * nd, tm), F32)], axis=0)
    g_ref[...] = rows.T


def _chunk_scan(a, pos, axis, reverse, use_max):
    n = a.shape[axis]
    ident = -jnp.inf if use_max else 0.0
    sh = 1
    while sh < CHUNK:
        if reverse:
            moved = jnp.where(pos < CHUNK - sh, pltpu.roll(a, n - sh, axis=axis), ident)
        else:
            moved = jnp.where(pos >= sh, pltpu.roll(a, sh, axis=axis), ident)
        a = jnp.maximum(a, moved) if use_max else a + moved
        sh *= 2
    return a


def _inproj(x, mods, w, b, wgt, bgt, conv_w, conv_b, *, n_lat, seq, ctx_len):
    t, d = x.shape
    tm = INPROJ_TILE
    assert seq % tm == 0 and n_lat % tm == 0 and (t - n_lat) % tm == 0 and tm % ctx_len == 0
    n_batch = mods.shape[0] - 1
    per_tile = tm // SUBLANES

    def mod_idx(i):
        return (jnp.minimum((i * tm) // seq, n_batch), 0, 0)

    const = lambda i: (0, 0)
    row = lambda i: (i, 0)
    parts = (("qk", _QK_WIDTH),) + _Z_PARTS
    out_shapes = [jax.ShapeDtypeStruct((t, width), BF16) for _, width in parts]
    out_specs = [pl.BlockSpec((tm, width), row) for _, width in parts]
    out_shapes += [jax.ShapeDtypeStruct((t, LANES), F32), jax.ShapeDtypeStruct((2 * N_HEADS, t), F32)]
    out_specs += [pl.BlockSpec((tm, LANES), row), pl.BlockSpec((2 * N_HEADS, tm), lambda i: (0, i))]
    return pl.pallas_call(
        functools.partial(_inproj_kernel, n_lat=n_lat, seq=seq, ctx_len=ctx_len),
        grid=(t // tm,),
        in_specs=[
            pl.BlockSpec((tm, d), row),
            pl.BlockSpec((SUBLANES, d), lambda i: (jnp.maximum(i * per_tile - 1, 0), 0)),
            pl.BlockSpec((SUBLANES, d), lambda i: (jnp.minimum((i + 1) * per_tile, t // SUBLANES - 1), 0)),
            pl.BlockSpec((None, 1, mods.shape[2]), mod_idx),
            pl.BlockSpec(w.shape, const),
            pl.BlockSpec(b.shape, const),
            pl.BlockSpec(wgt.shape, const),
            pl.BlockSpec(bgt.shape, const),
            pl.BlockSpec(conv_w.shape, const),
            pl.BlockSpec(conv_b.shape, const),
        ],
        out_specs=out_specs,
        out_shape=out_shapes,
        scratch_shapes=[pltpu.VMEM((2, tm + 2 * SUBLANES, _QK_WIDTH // 2), F32)],
        compiler_params=_cparams("parallel"),
    )(x, x, x, mods, w, b, wgt, bgt, conv_w, conv_b)


def _mlstm_row_block(j, b, direction, *, n_batch, seq, ctx_len):
    ncc = ctx_len // CHUNK
    ncl = seq // CHUNK
    is_ctx = j < ncc
    if direction == 0:
        c = jnp.where(is_ctx, j, j - ncc)
    else:
        c = jnp.where(is_ctx, ncc - 1 - j, ncl - 1 - (j - ncc))
    base = jnp.where(is_ctx, (n_batch * seq + b * ctx_len) // CHUNK, (b * seq) // CHUNK)
    return base + c


def _mlstm_kernel(q_f, k_f, v_f, g_f, gt_f, q_b, k_b, v_b, g_b, gt_b, hf_ref, hb_ref, s_sc, m_sc):
    j = pl.program_id(1)

    @pl.when(j == 0)
    def _():
        s_sc[...] = jnp.zeros_like(s_sc)
        m_sc[...] = jnp.zeros_like(m_sc)

    ri = lax.broadcasted_iota(jnp.int32, (CHUNK, CHUNK), 0)
    ci = lax.broadcasted_iota(jnp.int32, (CHUNK, CHUNK), 1)
    ones_tile = jnp.ones((CHUNK, HEAD_DIM), BF16)
    dirs = ((q_f, k_f, v_f, g_f, gt_f, hf_ref), (q_b, k_b, v_b, g_b, gt_b, hb_ref))
    combos = [(d, hh) for d in range(2) for hh in range(N_HEADS)]
    heads = lambda ref, hh: ref[:, hh * HEAD_DIM:(hh + 1) * HEAD_DIM]

    nd = 2 * N_HEADS
    tri = ((ci <= ri), (ci >= ri))
    g = (g_f[...], g_b[...])
    gt = (gt_f[...], gt_b[...])

    qk_s, v1, state, m_prev = {}, {}, {}, {}
    for d, hh in combos:
        q_ref, k_ref, v_ref = dirs[d][:3]
        qk_s[d, hh] = lax.dot_general(heads(q_ref, hh), heads(k_ref, hh), (((1,), (1,)), ((), ())),
                                      preferred_element_type=F32)
        v1[d, hh] = jnp.concatenate([heads(v_ref, hh), ones_tile], axis=1)
        state[d, hh] = s_sc[d, hh]
        m_prev[d, hh] = m_sc[d * N_HEADS + hh][0:1, 0:1]

    q_state = {}
    for d, hh in combos:
        q_state[d, hh] = jnp.dot(heads(dirs[d][0], hh), state[d, hh].astype(BF16), preferred_element_type=F32)

    m_t, decay, scores = {}, {}, {}
    for d, hh in combos:
        col = d * N_HEADS + hh
        bcum = g[d][:, col:col + 1]
        inter = bcum + m_prev[d, hh]
        m_t[d, hh] = jnp.maximum(inter, g[d][:, 2 * nd + col:2 * nd + col + 1])
        decay[d, hh] = jnp.exp(inter - m_t[d, hh])
        dlog = jnp.where(tri[d], (bcum - m_t[d, hh]) + gt[d][col:col + 1, :], -jnp.inf)
        scores[d, hh] = (qk_s[d, hh] * jnp.exp(dlog)).astype(BF16)

    for d, hh in combos:
        numden = decay[d, hh] * q_state[d, hh] + jnp.dot(scores[d, hh], v1[d, hh], preferred_element_type=F32)
        num = numden[:, :HEAD_DIM]
        den = numden[:, HEAD_DIM:]
        out = num / jnp.maximum(jnp.abs(den), jnp.exp(-m_t[d, hh]))
        out_ref = dirs[d][5]
        out_ref[:, hh * HEAD_DIM:(hh + 1) * HEAD_DIM] = out.astype(out_ref.dtype)

    kw, a, m_new = {}, {}, {}
    for d, hh in combos:
        col = d * N_HEADS + hh
        last = CHUNK - 1 if d == 0 else 0
        total = g[d][last:last + 1, col:col + 1]
        m_new[d, hh] = total + jnp.maximum(m_prev[d, hh], g[d][last:last + 1, 3 * nd + col:3 * nd + col + 1])
        a[d, hh] = jnp.exp(total + m_prev[d, hh] - m_new[d, hh])
        wk = jnp.exp(total + g[d][:, nd + col:nd + col + 1] - m_new[d, hh])
        kw[d, hh] = (heads(dirs[d][1], hh).astype(F32) * wk).astype(BF16)

    for d, hh in combos:
        upd = lax.dot_general(kw[d, hh], v1[d, hh], (((0,), (0,)), ((), ())), preferred_element_type=F32)
        s_sc[d, hh] = a[d, hh] * state[d, hh] + upd
        m_sc[d * N_HEADS + hh] = jnp.broadcast_to(m_new[d, hh], m_sc.shape[1:])


def _mlstm(zqk, zv, g, gt, *, n_batch, seq, ctx_len):
    t = zqk.shape[0]
    nsteps = (seq + ctx_len) // CHUNK
    width = N_HEADS * HEAD_DIM
    idx = functools.partial(_mlstm_row_block, n_batch=n_batch, seq=seq, ctx_len=ctx_len)

    def specs(d):
        rb = lambda b, j: idx(j, b, d)
        return [
            pl.BlockSpec((CHUNK, width), lambda b, j: (rb(b, j), 0)),
            pl.BlockSpec((CHUNK, width), lambda b, j: (rb(b, j), 1)),
            pl.BlockSpec((CHUNK, width), lambda b, j: (rb(b, j), 0)),
            pl.BlockSpec((CHUNK, LANES), lambda b, j: (rb(b, j), 0)),
            pl.BlockSpec((2 * N_HEADS, CHUNK), lambda b, j: (0, rb(b, j))),
        ]

    out_spec = lambda d: pl.BlockSpec((CHUNK, width), lambda b, j: (idx(j, b, d), 0))
    return pl.pallas_call(
        _mlstm_kernel,
        grid=(n_batch, nsteps),
        in_specs=specs(0) + specs(1),
        out_specs=[out_spec(0), out_spec(1)],
        out_shape=[jax.ShapeDtypeStruct((t, width), BF16)] * 2,
        scratch_shapes=[
            pltpu.VMEM((2, N_HEADS, HEAD_DIM, 2 * HEAD_DIM), F32),
            pltpu.VMEM((2 * N_HEADS, 8, LANES), F32),
        ],
        compiler_params=_cparams("parallel", "arbitrary"),
    )(zqk, zqk, zv, g, gt, zqk, zqk, zv, g, gt)


def _pool_constants(ctx_len):
    tm = ROW_TILE
    masks = np.zeros((2, len(POOL_WINDOWS), tm, tm), np.float32)
    cnts = np.zeros((2, tm, len(POOL_WINDOWS) * POOL_GROUP_DIM), np.float32)
    for kind, length in enumerate((GRID_W, ctx_len)):
        for gi, w in enumerate(POOL_WINDOWS):
            for t in range(tm):
                base = (t // length) * length
                tl = t - base
                lo = min(max(tl - w // 2, 0), length)
                hi = min(max(tl + w // 2, 0), length)
                masks[kind, gi, t, base + lo:base + hi] = 1.0
                cnts[kind, t, gi * POOL_GROUP_DIM:(gi + 1) * POOL_GROUP_DIM] = hi - lo
    return jnp.asarray(masks, BF16), jnp.asarray(cnts, F32)


def _mix_kernel(x_ref, zp_ref, zuv_ref, zmg_ref, zo_ref, hf_ref, hb_ref, mod_ref,
                pm_ref, cnt_ref, pw_ref, ps_ref, sg_ref, sb_ref, sw_ref, sbm_ref, ng_ref,
                wbp_ref, wbm_ref, wbs_ref, wo_ref, l1g_ref, l1b_ref, wr_ref, br_ref,
                x1_ref, h2_ref, route_ref, cnt_out_ref, cnt_sc, *, alpha):
    tm, d = x_ref.shape
    i = pl.program_id(0)

    @pl.when(i == 0)
    def _():
        cnt_sc[...] = jnp.zeros_like(cnt_sc)

    lane256 = lax.broadcasted_iota(jnp.int32, (1, 4 * POOL_GROUP_DIM), 1) // POOL_GROUP_DIM

    xp = zp_ref[...]
    xp_f = xp.astype(F32)
    pooled = jnp.zeros(xp_f.shape, F32)
    for gi in range(len(POOL_WINDOWS)):
        s = jnp.dot(pm_ref[gi], xp, preferred_element_type=F32)
        pooled = jnp.where(lane256 == gi, s, pooled)
    diff = pooled / cnt_ref[...] - xp_f
    pool_o = jnp.dot(diff.astype(BF16), pw_ref[...], preferred_element_type=F32) * ps_ref[...]

    uv = zuv_ref[...].astype(F32)
    uv = 0.5 * uv * (1.0 + lax.erf(uv * (2.0 ** -0.5)))
    half = uv.shape[1] // 2
    u = uv[:, :half]
    vn = (_ln_plain(uv[:, half:]) * sg_ref[...] + sb_ref[...]).astype(BF16)
    mixed_chunks = []
    for c0 in range(0, tm, CHUNK):
        vc = vn[c0:c0 + CHUNK, :]
        mixed = jnp.zeros((CHUNK, half), F32)
        for gi in range(sw_ref.shape[0]):
            s = jnp.dot(sw_ref[gi], vc, preferred_element_type=F32)
            mixed = jnp.where(lane256 == gi, s, mixed)
        mixed_chunks.append(mixed + sbm_ref[...])
    sgu_o = u * jnp.concatenate(mixed_chunks, axis=0)

    hsum = hf_ref[...].astype(F32) + hb_ref[...].astype(F32)
    heads = []
    for hh in range(N_HEADS):
        hv = hsum[:, hh * HEAD_DIM:(hh + 1) * HEAD_DIM]
        mu = jnp.mean(hv, axis=-1, keepdims=True)
        hc = hv - mu
        var = jnp.mean(hc * hc, axis=-1, keepdims=True)
        heads.append(hc * lax.rsqrt(var + HEAD_NORM_EPS))
    hn = jnp.concatenate(heads, axis=1)
    twice_gate = lambda zh: jnp.tanh(zh.astype(F32)) + 1.0
    mlstm_o = hn * ng_ref[...] * twice_gate(zo_ref[...])

    y = twice_gate(zmg_ref[:, 0:d]) * jnp.dot(pool_o.astype(BF16), wbp_ref[...], preferred_element_type=F32)
    y += twice_gate(zmg_ref[:, d:2 * d]) * jnp.dot(mlstm_o.astype(BF16), wbm_ref[...],
                                                   preferred_element_type=F32)
    y += twice_gate(zmg_ref[:, 2 * d:3 * d]) * jnp.dot(sgu_o.astype(BF16), wbs_ref[...],
                                                       preferred_element_type=F32)
    y2 = jnp.dot(y.astype(BF16), wo_ref[...], preferred_element_type=F32)

    g1 = mod_ref[:, 2 * d:3 * d]
    x1 = _ln_plain(alpha * x_ref[...] + g1 * y2) * l1g_ref[...] + l1b_ref[...]
    x1_ref[...] = x1
    h2 = _ln_plain(x1) * (1.0 + mod_ref[:, 4 * d:5 * d]) + mod_ref[:, 3 * d:4 * d]
    h2b = h2.astype(BF16)
    h2_ref[...] = _pack_rows(h2b)

    logits = jnp.dot(h2b, wr_ref[...], preferred_element_type=F32) + br_ref[...]
    lane = lax.broadcasted_iota(jnp.int32, logits.shape, 1)
    work = logits
    vals, idxs, hots = [], [], []
    for _ in range(TOP_K):
        mval = jnp.max(work, axis=-1, keepdims=True)
        idx = jnp.min(jnp.where(work == mval, lane, LANES), axis=-1, keepdims=True)
        hot = lane == idx
        vals.append(mval)
        idxs.append(idx)
        hots.append(hot)
        work = jnp.where(hot, -jnp.inf, work)
    exps = [jnp.exp(v - vals[0]) for v in vals]
    denom = exps[0] + exps[1] + exps[2] + exps[3]
    sel = (hots[0] | hots[1] | hots[2] | hots[3])
    r_i = lax.broadcasted_iota(jnp.int32, (tm, tm), 0)
    c_i = lax.broadcasted_iota(jnp.int32, (tm, tm), 1)
    before = (c_i < r_i).astype(BF16)
    sel_b = sel.astype(F32).astype(BF16)
    rank = jnp.dot(before, sel_b, preferred_element_type=F32) + cnt_sc[0:1, :]
    route = jnp.zeros(logits.shape, F32)
    for kk in range(TOP_K):
        rk = jnp.sum(jnp.where(hots[kk], rank, 0.0), axis=-1, keepdims=True)
        route = jnp.where(lane == kk, idxs[kk].astype(F32), route)
        route = jnp.where(lane == TOP_K + kk, exps[kk] / denom, route)
        route = jnp.where(lane == 2 * TOP_K + kk, rk, route)
    route_ref[...] = route
    new_cnt = cnt_sc[0:1, :] + jnp.sum(sel.astype(F32), axis=0, keepdims=True)
    cnt_sc[...] = jnp.broadcast_to(new_cnt, cnt_sc.shape)
    cnt_out_ref[...] = jnp.broadcast_to(new_cnt, cnt_out_ref.shape)


def _mix(x, zp, zuv, zmg, zo, hf, hb, mods, consts, *, n_rows, n_lat, seq, alpha):
    t, d = x.shape
    tm = ROW_TILE
    n_batch = mods.shape[0] - 1
    row = lambda i: (i, 0)
    const2 = lambda i: (0, 0)
    const3 = lambda i: (0, 0, 0)
    kind = lambda i: jnp.where(i * tm >= n_lat, 1, 0)

    def mod_idx(i):
        return (jnp.minimum((i * tm) // seq, n_batch), 0, 0)

    (pm, cnt, pw, ps, sg, sb, sw, sbm, ng, wbp, wbm, wbs, wo, l1g, l1b, wr, br) = consts
    in_specs = [
        pl.BlockSpec((tm, d), row),
        pl.BlockSpec((tm, zp.shape[1]), row),
        pl.BlockSpec((tm, zuv.shape[1]), row),
        pl.BlockSpec((tm, zmg.shape[1]), row),
        pl.BlockSpec((tm, zo.shape[1]), row),
        pl.BlockSpec((tm, hf.shape[1]), row),
        pl.BlockSpec((tm, hb.shape[1]), row),
        pl.BlockSpec((None, 1, mods.shape[2]), mod_idx),
        pl.BlockSpec((None,) + pm.shape[1:], lambda i: (kind(i), 0, 0, 0)),
        pl.BlockSpec((None,) + cnt.shape[1:], lambda i: (kind(i), 0, 0)),
        pl.BlockSpec(pw.shape, const2),
        pl.BlockSpec(ps.shape, const2),
        pl.BlockSpec(sg.shape, const2),
        pl.BlockSpec(sb.shape, const2),
        pl.BlockSpec(sw.shape, const3),
        pl.BlockSpec(sbm.shape, const2),
        pl.BlockSpec(ng.shape, const2),
        pl.BlockSpec(wbp.shape, const2),
        pl.BlockSpec(wbm.shape, const2),
        pl.BlockSpec(wbs.shape, const2),
        pl.BlockSpec(wo.shape, const2),
        pl.BlockSpec(l1g.shape, const2),
        pl.BlockSpec(l1b.shape, const2),
        pl.BlockSpec(wr.shape, const2),
        pl.BlockSpec(br.shape, const2),
    ]
    return pl.pallas_call(
        functools.partial(_mix_kernel, alpha=alpha),
        grid=(n_rows // tm,),
        in_specs=in_specs,
        out_specs=[
            pl.BlockSpec((tm, d), row),
            pl.BlockSpec((tm, d // 2), row),
            pl.BlockSpec((tm, LANES), row),
            pl.BlockSpec((8, LANES), const2),
        ],
        out_shape=[
            jax.ShapeDtypeStruct((n_rows, d), F32),
            jax.ShapeDtypeStruct((n_rows, d // 2), jnp.uint32),
            jax.ShapeDtypeStruct((n_rows, LANES), F32),
            jax.ShapeDtypeStruct((8, LANES), F32),
        ],
        scratch_shapes=[pltpu.VMEM((8, LANES), F32)],
        compiler_params=_cparams("arbitrary"),
    )(x, zp, zuv, zmg, zo, hf, hb, mods, pm, cnt, pw, ps, sg, sb, sw, sbm, ng,
      wbp, wbm, wbs, wo, l1g, l1b, wr, br)


SCATTER_WINDOW = 128
SCATTER_PARTS = 2


def _scatter_rows(x, idx, n_out, part):
    n = x.shape[0]
    w = x.shape[1] // SCATTER_PARTS
    n_idx = idx.shape[0]
    mesh = plsc.VectorSubcoreMesh(core_axis_name="core", subcore_axis_name="subcore")

    @pl.kernel(out_type=jax.ShapeDtypeStruct((n_out, w), x.dtype), mesh=mesh, scratch_types=[])
    def scatter_kernel(x_hbm, i_hbm, o_hbm):
        def body(x_vmem, i_vmem):
            for k in range(n_idx):
                pltpu.sync_copy(x_vmem, o_hbm.at[i_vmem.at[k]])

        pltpu.emit_pipeline(
            body,
            grid=(n // SCATTER_WINDOW,),
            in_specs=[pl.BlockSpec((SCATTER_WINDOW, w), lambda i: (i, part)),
                      pl.BlockSpec((n_idx, SCATTER_WINDOW), lambda i: (0, i))],
            out_specs=[],
            core_axis_name=("core", "subcore"),
            dimension_semantics=(pltpu.PARALLEL,),
        )(x_hbm, i_hbm)

    return scatter_kernel(x, idx)


def _gather_rows(x, idx):
    w = x.shape[1]
    n_idx = idx.shape[1]
    mesh = plsc.VectorSubcoreMesh(core_axis_name="core", subcore_axis_name="subcore")

    @pl.kernel(out_type=jax.ShapeDtypeStruct((n_idx, w), x.dtype), mesh=mesh, scratch_types=[])
    def gather_kernel(x_hbm, i_hbm, o_hbm):
        def body(i_vmem, o_vmem):
            pltpu.sync_copy(x_hbm.at[i_vmem.at[0]], o_vmem)

        pltpu.emit_pipeline(
            body,
            grid=(n_idx // SCATTER_WINDOW,),
            in_specs=[pl.BlockSpec((1, SCATTER_WINDOW), lambda i: (0, i))],
            out_specs=[pl.BlockSpec((SCATTER_WINDOW, w), lambda i: (i, 0))],
            core_axis_name=("core", "subcore"),
            dimension_semantics=(pltpu.PARALLEL,),
        )(i_hbm, o_hbm)

    return gather_kernel(x, idx)


def _expert_kernel(te_ref, nt_ref, xa_ref, xb_ref, wgu_ref, bgu_ref, wd_ref, bd_ref, ya_ref, yb_ref,
                   wgu_sc, wd_sc):
    i = pl.program_id(0)
    f = wd_ref.shape[0]
    used = i < nt_ref[0]

    @pl.when(jnp.logical_and(used, jnp.logical_or(i == 0, te_ref[i] != te_ref[jnp.maximum(i - 1, 0)])))
    def _():
        for c0 in range(0, 2 * f, 512):
            wgu_sc[:, c0:c0 + 512] = wgu_ref[:, c0:c0 + 512].astype(BF16)
        for c0 in range(0, f, 256):
            wd_sc[c0:c0 + 256, :] = wd_ref[c0:c0 + 256, :].astype(BF16)

    @pl.when(used)
    def _():
        x = _unpack_rows([xa_ref[...], xb_ref[...]]).astype(BF16)
        acc = jnp.zeros((x.shape[0], wd_ref.shape[1]), F32)
        for c0 in range(0, f, 512):
            gate = jnp.dot(x, wgu_sc[:, c0:c0 + 512], preferred_element_type=F32) + bgu_ref[:, c0:c0 + 512]
            up = (jnp.dot(x, wgu_sc[:, f + c0:f + c0 + 512], preferred_element_type=F32)
                  + bgu_ref[:, f + c0:f + c0 + 512])
            gate = jnp.minimum(gate, SWIGLU_LIMIT)
            up = jnp.clip(up, -SWIGLU_LIMIT, SWIGLU_LIMIT)
            glu = gate * _sigmoid(SWIGLU_ALPHA * gate)
            act = ((up + 1.0) * glu).astype(BF16)
            acc += jnp.dot(act, wd_sc[c0:c0 + 512, :], preferred_element_type=F32)
        words = _pack_rows(acc + bd_ref[...])
        ya_ref[...] = words[:, :ya_ref.shape[1]]
        yb_ref[...] = words[:, ya_ref.shape[1]:]

    @pl.when(jnp.logical_not(used))
    def _():
        ya_ref[...] = jnp.zeros_like(ya_ref)
        yb_ref[...] = jnp.zeros_like(yb_ref)


def _experts(tile_expert, n_tiles_used, xa, xb, wgu, bgu, wd, bd, *, layer):
    p = xa.shape[0]
    d = 2 * SCATTER_PARTS * xa.shape[1]
    tm = EXPERT_TILE
    f2 = wgu.shape[3]
    grid_spec = pltpu.PrefetchScalarGridSpec(
        num_scalar_prefetch=2,
        grid=(p // tm,),
        in_specs=[
            pl.BlockSpec((tm, xa.shape[1]), lambda i, te, nt: (i, 0)),
            pl.BlockSpec((tm, xb.shape[1]), lambda i, te, nt: (i, 0)),
            pl.BlockSpec((None, None, d, f2), lambda i, te, nt: (layer, te[i], 0, 0)),
            pl.BlockSpec((None, None, 1, f2), lambda i, te, nt: (layer, te[i], 0, 0)),
            pl.BlockSpec((None, None, f2 // 2, d), lambda i, te, nt: (layer, te[i], 0, 0)),
            pl.BlockSpec((None, None, 1, d), lambda i, te, nt: (layer, te[i], 0, 0)),
        ],
        out_specs=[pl.BlockSpec((tm, xa.shape[1]), lambda i, te, nt: (i, 0)),
                   pl.BlockSpec((tm, xb.shape[1]), lambda i, te, nt: (i, 0))],
        scratch_shapes=[pltpu.VMEM((d, f2), BF16), pltpu.VMEM((f2 // 2, d), BF16)],
    )
    return pl.pallas_call(
        _expert_kernel,
        grid_spec=grid_spec,
        out_shape=[jax.ShapeDtypeStruct(xa.shape, xa.dtype), jax.ShapeDtypeStruct(xb.shape, xb.dtype)],
        compiler_params=_cparams("arbitrary"),
    )(tile_expert, n_tiles_used, xa, xb, wgu, bgu, wd, bd)


def _combine_kernel(x1_ref, ya_ref, yb_ref, route_ref, mod_ref, g_ref, b_ref, *rest, alpha):
    x2_ref = rest[-1]
    d = x1_ref.shape[1]
    route = route_ref[...]
    f = jnp.zeros(x1_ref.shape, F32)
    for kk in range(TOP_K):
        f += route[:, TOP_K + kk:TOP_K + kk + 1] * _unpack_rows([ya_ref[kk], yb_ref[kk]])
    g2 = mod_ref[:, 5 * d:6 * d]
    x2_ref[...] = _ln_plain(alpha * x1_ref[...] + g2 * f) * g_ref[...] + b_ref[...]


def _combine(x1, yg_a, yg_b, route, mods, g, b, *, seq, alpha, dest=None, dest_rows=None, tile_offset=0):
    t, d = x1.shape
    tm = COMBINE_TILE
    n_batch = mods.shape[0] - 1
    row = lambda i: (i, 0)

    def mod_idx(i):
        return (jnp.minimum((i * tm) // seq, n_batch), 0, 0)

    in_specs = [
        pl.BlockSpec((tm, d), row),
        pl.BlockSpec((TOP_K, tm, yg_a.shape[2]), lambda i: (0, i, 0)),
        pl.BlockSpec((TOP_K, tm, yg_b.shape[2]), lambda i: (0, i, 0)),
        pl.BlockSpec((tm, LANES), row),
        pl.BlockSpec((None, 1, mods.shape[2]), mod_idx),
        pl.BlockSpec(g.shape, lambda i: (0, 0)),
        pl.BlockSpec(b.shape, lambda i: (0, 0)),
    ]
    args = [x1, yg_a, yg_b, route, mods, g, b]
    aliases = {}
    if dest is not None:
        in_specs.append(pl.BlockSpec(memory_space=pl.ANY))
        args.append(dest)
        aliases = {len(args) - 1: 0}
    return pl.pallas_call(
        functools.partial(_combine_kernel, alpha=alpha),
        grid=(t // tm,),
        in_specs=in_specs,
        out_specs=pl.BlockSpec((tm, d), lambda i: (i + tile_offset, 0)),
        out_shape=jax.ShapeDtypeStruct((t if dest_rows is None else dest_rows, d), F32),
        input_output_aliases=aliases,
        compiler_params=_cparams("parallel"),
    )(*args)


def kernel(x, c, ctx, c_ctx, w_ada, b_ada, w_in, b_in, pool_w, pool_scale, qk_conv_w, qk_conv_b,
           mlstm_norm_g, sgu_ln_g, sgu_ln_b, sgu_w, sgu_b, w_br_pool, w_br_mlstm, w_br_sgu, w_out,
           ln1_g, ln1_b, w_router, b_router, w_gate_up, b_gate_up, w_down, b_down, ln2_g, ln2_b):
    n_batch, seq, d = x.shape
    ctx_len = ctx.shape[1]
    depth = w_in.shape[0]
    n_exp = w_router.shape[2]
    assert ctx_len == ROW_TILE and seq % ROW_TILE == 0 and ROW_TILE % GRID_W == 0
    assert n_batch % N_STREAMS == 0
    alpha = float((2 * depth) ** 0.25)

    rows = ((n_batch + 1 + 7) // 8) * 8
    cc = jnp.zeros((rows, d), F32).at[:n_batch].set(c).at[n_batch].set(c_ctx)
    mods_all = _ada_table(cc, w_ada, b_ada)[:, :n_batch + 1].reshape(depth, n_batch + 1, 1, 6 * d)

    pool_w_, mlw = 256, 512
    col = lambda a, lo, hi: a[..., lo:hi]
    g_lo = pool_w_ + 4 * mlw
    parts = ((pool_w_, pool_w_ + 2 * mlw), (pool_w_ + 2 * mlw, pool_w_ + 3 * mlw), (pool_w_ + 3 * mlw, g_lo),
             (g_lo + _N_GATE_COLS, g_lo + _N_GATE_COLS + 512), (0, pool_w_),
             (g_lo + _N_GATE_COLS + 512, w_in.shape[2]))
    part_scale = (1.0, 1.0, 0.5, 1.0, 1.0, 0.5)
    w_main = jnp.concatenate([col(w_in, lo, hi) * sc for (lo, hi), sc in zip(parts, part_scale)],
                             axis=-1).astype(BF16)
    b_main = jnp.concatenate([col(b_in, lo, hi) * sc for (lo, hi), sc in zip(parts, part_scale)],
                             axis=-1)[:, None, :]
    gw, gb = col(w_in, g_lo, g_lo + _N_GATE_COLS), col(b_in, g_lo, g_lo + _N_GATE_COLS)
    nh = N_HEADS
    pick = lambda a, k: jnp.concatenate([a[..., k * nh:(k + 1) * nh], a[..., (k + 2) * nh:(k + 3) * nh]], axis=-1)
    w_gate_t = jnp.swapaxes(jnp.concatenate([pick(gw, 0), pick(gw, 1)], axis=-1), 1, 2).astype(BF16)
    b_gate_t = jnp.concatenate([pick(gb, 0), pick(gb, 1)], axis=-1)[:, :, None]

    pm, cnt = _pool_constants(ctx_len)
    eye = jnp.eye(len(POOL_WINDOWS), dtype=F32)
    pool_bd = jnp.einsum('lgcd,gh->lgchd', pool_w, eye).reshape(depth, 256, 256).astype(BF16)
    sgu_bias_map = jnp.repeat(jnp.swapaxes(sgu_b, 1, 2), sgu_ln_g.shape[1] // sgu_b.shape[1], axis=2)
    w_router_p = jnp.pad(w_router, ((0, 0), (0, 0), (0, LANES - n_exp))).astype(BF16)
    b_router_p = jnp.pad(b_router, ((0, 0), (0, LANES - n_exp)), constant_values=NEG_BIG)[:, None, :]
    r1 = lambda a: a[:, None, :]
    b_gu4 = b_gate_up[:, :, None, :]
    b_d4 = b_down[:, :, None, :]

    nb = n_batch // N_STREAMS
    n_lat = nb * seq
    t_all = n_lat + nb * ctx_len
    streams, stream_mods = [], []
    for h in range(N_STREAMS):
        sl = slice(h * nb, (h + 1) * nb)
        streams.append(jnp.concatenate([x[sl].reshape(n_lat, d), ctx[sl].reshape(nb * ctx_len, d)], axis=0))
        stream_mods.append(jnp.concatenate([mods_all[:, sl], mods_all[:, n_batch:n_batch + 1]], axis=1))
    out = None
    for i in range(depth):
        last = i == depth - 1
        for h in range(N_STREAMS):
            stream = streams[h]
            mods = stream_mods[h][i]
            zqk, zv, zo, zuv, zp, zmg, g, gt = _inproj(
                stream, mods, w_main[i], b_main[i], w_gate_t[i], b_gate_t[i],
                qk_conv_w[i], r1(qk_conv_b)[i], n_lat=n_lat, seq=seq, ctx_len=ctx_len)
            hf, hb = _mlstm(zqk, zv, g, gt, n_batch=nb, seq=seq, ctx_len=ctx_len)
            n_rows = n_lat if last else t_all
            consts = (pm, cnt, pool_bd[i], r1(pool_scale)[i], r1(sgu_ln_g)[i], r1(sgu_ln_b)[i],
                      sgu_w[i].astype(BF16), sgu_bias_map[i], r1(0.5 * mlstm_norm_g)[i],
                      w_br_pool[i].astype(BF16), w_br_mlstm[i].astype(BF16), w_br_sgu[i].astype(BF16),
                      (0.5 * w_out[i]).astype(BF16), r1(ln1_g)[i], r1(ln1_b)[i], w_router_p[i], b_router_p[i])
            x1, h2, route, counts = _mix(stream, zp, zuv, zmg, zo, hf, hb, mods, consts,
                                         n_rows=n_rows, n_lat=n_lat, seq=seq, alpha=alpha)

            cnt_e = counts[0, :n_exp].astype(jnp.int32)
            padded = ((cnt_e + EXPERT_TILE - 1) // EXPERT_TILE) * EXPERT_TILE
            ends = jnp.cumsum(padded)
            starts = ends - padded
            e_idx = route[:, 0:TOP_K].astype(jnp.int32)
            rank = route[:, 2 * TOP_K:3 * TOP_K].astype(jnp.int32)
            pos_t = (starts[e_idx] + rank).T
            p_rows = n_rows * TOP_K + n_exp * EXPERT_TILE
            n_tiles = p_rows // EXPERT_TILE
            tile_start = jnp.arange(n_tiles, dtype=jnp.int32) * EXPERT_TILE
            tile_expert = jnp.minimum(
                jnp.sum((ends[None, :] <= tile_start[:, None]).astype(jnp.int32), axis=1), n_exp - 1)
            n_used = (ends[-1] // EXPERT_TILE).astype(jnp.int32).reshape(1)
            xs_a = _scatter_rows(h2, pos_t, p_rows, 0)
            xs_b = _scatter_rows(h2, pos_t, p_rows, 1)
            ys_a, ys_b = _experts(tile_expert, n_used, xs_a, xs_b, w_gate_up, b_gu4, w_down, b_d4, layer=i)
            pos_flat = pos_t.reshape(1, TOP_K * n_rows)
            yg_a = _gather_rows(ys_a, pos_flat).reshape(TOP_K, n_rows, -1)
            yg_b = _gather_rows(ys_b, pos_flat).reshape(TOP_K, n_rows, -1)
            if last:
                out = _combine(x1, yg_a, yg_b, route, mods, r1(ln2_g)[i], r1(ln2_b)[i], seq=seq, alpha=alpha,
                               dest=out, dest_rows=n_batch * seq, tile_offset=h * n_lat // COMBINE_TILE)
            else:
                streams[h] = _combine(x1, yg_a, yg_b, route, mods, r1(ln2_g)[i], r1(ln2_b)[i],
                                      seq=seq, alpha=alpha)
    return out.reshape(n_batch, seq, d)
```

```python
import functools

import numpy as np
import jax
import jax.numpy as jnp
from jax import lax
from jax.experimental import pallas as pl
from jax.experimental.pallas import tpu as pltpu
from jax.experimental.pallas import tpu_sc as plsc

F32 = jnp.float32
BF16 = jnp.bfloat16

GRID_W = 64
POOL_WINDOWS = (2, 4, 8, 16)
POOL_GROUP_DIM = 64
N_HEADS = 4
HEAD_DIM = 128
CHUNK = 128
TOP_K = 4
SWIGLU_LIMIT = 7.0
SWIGLU_ALPHA = 1.702
LN_EPS = 1e-5
HEAD_NORM_EPS = 1e-6

ROW_TILE = 256
INPROJ_TILE = 512
COMBINE_TILE = 512
SEAM_ROWS = 16
ROUTE_ROWS = 16
EXPERT_TILE = 512
N_STREAMS = 2
SUBLANES = 8
LANES = 128
VMEM_LIMIT = 56 * 1024 * 1024


def _cparams(*sem):
    return pltpu.CompilerParams(dimension_semantics=sem, vmem_limit_bytes=VMEM_LIMIT)


def _ln_plain(x):
    mu = jnp.mean(x, axis=-1, keepdims=True)
    xc = x - mu
    var = jnp.mean(xc * xc, axis=-1, keepdims=True)
    return xc * lax.rsqrt(var + LN_EPS)


def _sigmoid(x):
    return 0.5 * jnp.tanh(0.5 * x) + 0.5


def _log_sigmoid(x):
    return jnp.minimum(x, 0.0) - jnp.log1p(jnp.exp(-jnp.abs(x)))


def _pack_rows(x):
    half = x.shape[1] // 2
    bits = lax.bitcast_convert_type(x.astype(BF16).astype(F32), jnp.uint32)
    return (bits[:, half:] & jnp.uint32(0xFFFF0000)) | (bits[:, :half] >> 16)


def _unpack_rows(parts):
    low = [lax.bitcast_convert_type(w << 16, F32) for w in parts]
    high = [lax.bitcast_convert_type(w & jnp.uint32(0xFFFF0000), F32) for w in parts]
    return jnp.concatenate(low + high, axis=1)


def _ada_kernel(c_ref, w_ref, b_ref, o_ref):
    c = c_ref[...]
    s = c * _sigmoid(c)
    o_ref[...] = jnp.dot(s, w_ref[...], preferred_element_type=F32,
                         precision=lax.Precision.HIGHEST) + b_ref[...]


def _ada_table(cc, w_ada, b_ada):
    depth, d, n = w_ada.shape
    rows = cc.shape[0]
    bn = 1536
    return pl.pallas_call(
        _ada_kernel,
        grid=(depth, n // bn),
        in_specs=[
            pl.BlockSpec((rows, d), lambda l, j: (0, 0)),
            pl.BlockSpec((None, d, bn), lambda l, j: (l, 0, j)),
            pl.BlockSpec((None, 1, bn), lambda l, j: (l, 0, j)),
        ],
        out_specs=pl.BlockSpec((None, rows, bn), lambda l, j: (l, 0, j)),
        out_shape=jax.ShapeDtypeStruct((depth, rows, n), F32),
        compiler_params=_cparams("parallel", "parallel"),
    )(cc, w_ada, b_ada.reshape(depth, 1, n))


_QK_WIDTH = 2 * N_HEADS * HEAD_DIM
_Z_PARTS = (("v", 512), ("o", 512), ("uv", 512), ("pool", 256), ("mg", 3072))
_N_GATE_COLS = 4 * N_HEADS


def _inproj_kernel(x_ref, xp_ref, xn_ref, mod_ref, w_ref, b_ref, wgt_ref, bgt_ref,
                   cw_ref, cb_ref, qk_ref, v_ref, o_ref, uv_ref, p_ref, mg_ref, g_ref, gt_ref, pad_sc,
                   *, n_lat, seq, ctx_len):
    tm, d = x_ref.shape
    i = pl.program_id(0)
    is_lat = i * tm < n_lat
    has_prev = jnp.logical_and(is_lat, (i * tm) % seq != 0).astype(F32)
    has_next = jnp.logical_and(is_lat, ((i + 1) * tm) % seq != 0).astype(F32)

    xe = jnp.concatenate([xp_ref[...], x_ref[...], xn_ref[...]], axis=0)
    he = (_ln_plain(xe) * (1.0 + mod_ref[:, d:2 * d]) + mod_ref[:, 0:d]).astype(BF16)
    hb = he[SUBLANES:SUBLANES + tm, :]

    half = _QK_WIDTH // 2

    def qk_project(part):
        c0 = part * half
        z = jnp.dot(he, w_ref[:, c0:c0 + half], preferred_element_type=F32) + b_ref[:, c0:c0 + half]
        pad_sc[part] = z
        pad_sc[part, SUBLANES - 1:SUBLANES, :] = pad_sc[part, SUBLANES - 1:SUBLANES, :] * has_prev
        pad_sc[part, SUBLANES + tm:SUBLANES + tm + 1, :] = (
            pad_sc[part, SUBLANES + tm:SUBLANES + tm + 1, :] * has_next)

    def conv_act(part, prev, cur, nxt):
        c0 = part * half
        y = (cb_ref[:, c0:c0 + half] + prev * cw_ref[0:1, c0:c0 + half]
             + cur * cw_ref[1:2, c0:c0 + half] + nxt * cw_ref[2:3, c0:c0 + half])
        act = y * _sigmoid(y)
        if part == 1:
            act = act * (HEAD_DIM ** -0.5)
        return act.astype(qk_ref.dtype)

    def qk_conv(part):
        qk_ref[:, part * half:(part + 1) * half] = conv_act(
            part, pad_sc[part, SUBLANES - 1:SUBLANES - 1 + tm, :], pad_sc[part, SUBLANES:SUBLANES + tm, :],
            pad_sc[part, SUBLANES + 1:SUBLANES + 1 + tm, :])

    def plain(ref, off, c0, cw):
        z = jnp.dot(hb, w_ref[:, off + c0:off + c0 + cw], preferred_element_type=F32)
        ref[:, c0:c0 + cw] = (z + b_ref[:, off + c0:off + c0 + cw]).astype(ref.dtype)

    mxu_tasks = []
    off = _QK_WIDTH
    for ref, (_, width) in zip((v_ref, o_ref, uv_ref, p_ref, mg_ref), _Z_PARTS):
        for c0 in range(0, width, 512):
            mxu_tasks.append(functools.partial(plain, ref, off, c0, min(512, width - c0)))
        off += width
    vpu_tasks = {2: functools.partial(_gate_prep, hb, wgt_ref, bgt_ref, g_ref, gt_ref),
                 4: functools.partial(qk_conv, 0), 6: functools.partial(qk_conv, 1)}
    qk_project(0)
    qk_project(1)
    for n, task in enumerate(mxu_tasks):
        if n in vpu_tasks:
            vpu_tasks[n]()
        task()

    if tm > ctx_len:
        @pl.when(jnp.logical_not(is_lat))
        def _():
            r = lax.broadcasted_iota(jnp.int32, (2 * SEAM_ROWS, 1), 0)
            for part in range(2):
                for p in range(ctx_len, tm, ctx_len):
                    lo = SUBLANES + p - SEAM_ROWS
                    prev = jnp.where(r == SEAM_ROWS, 0.0, pad_sc[part, lo - 1:lo - 1 + 2 * SEAM_ROWS, :])
                    nxt = jnp.where(r == SEAM_ROWS - 1, 0.0, pad_sc[part, lo + 1:lo + 1 + 2 * SEAM_ROWS, :])
                    qk_ref[p - SEAM_ROWS:p + SEAM_ROWS, part * half:(part + 1) * half] = conv_act(
                        part, prev, pad_sc[part, lo:lo + 2 * SEAM_ROWS, :], nxt)


def _gate_prep(hb, wgt_ref, bgt_ref, g_ref, gt_ref):
    tm = hb.shape[0]
    nd = 2 * N_HEADS
    ggt = lax.dot_general(wgt_ref[...], hb, (((1,), (1,)), ((), ())),
                          preferred_element_type=F32) + bgt_ref[...]
    git, gft = ggt[:nd, :], ggt[nd:, :]
    fwd = lax.broadcasted_iota(jnp.int32, (nd, tm), 0) < N_HEADS
    pos = lax.broadcasted_iota(jnp.int32, (nd, tm), 1) % CHUNK
    lsg = _log_sigmoid(gft)
    bcum = jnp.where(fwd, _chunk_scan(lsg, pos, 1, False, False), _chunk_scan(lsg, pos, 1, True, False))
    c = git - bcum
    cmax = jnp.where(fwd, _chunk_scan(c, pos, 1, False, True), _chunk_scan(c, pos, 1, True, True))
    gt_ref[...] = c
    rows = jnp.concatenate([bcum, c, bcum + cmax, cmax, jnp.zeros((LANES - 4 * nd, tm), F32)], axis=0)
    g_ref[...] = rows.T


def _chunk_scan(a, pos, axis, reverse, use_max):
    n = a.shape[axis]
    ident = -jnp.inf if use_max else 0.0
    sh = 1
    while sh < CHUNK:
        if reverse:
            moved = jnp.where(pos < CHUNK - sh, pltpu.roll(a, n - sh, axis=axis), ident)
        else:
            moved = jnp.where(pos >= sh, pltpu.roll(a, sh, axis=axis), ident)
        a = jnp.maximum(a, moved) if use_max else a + moved
        sh *= 2
    return a


def _inproj(x, mods, w, b, wgt, bgt, conv_w, conv_b, *, n_lat, seq, ctx_len):
    t, d = x.shape
    tm = INPROJ_TILE
    assert seq % tm == 0 and n_lat % tm == 0 and (t - n_lat) % tm == 0 and tm % ctx_len == 0
    n_batch = mods.shape[0] - 1
    per_tile = tm // SUBLANES

    def mod_idx(i):
        return (jnp.minimum((i * tm) // seq, n_batch), 0, 0)

    const = lambda i: (0, 0)
    row = lambda i: (i, 0)
    parts = (("qk", _QK_WIDTH),) + _Z_PARTS
    out_shapes = [jax.ShapeDtypeStruct((t, width), BF16) for _, width in parts]
    out_specs = [pl.BlockSpec((tm, width), row) for _, width in parts]
    out_shapes += [jax.ShapeDtypeStruct((t, LANES), F32), jax.ShapeDtypeStruct((2 * N_HEADS, t), F32)]
    out_specs += [pl.BlockSpec((tm, LANES), row), pl.BlockSpec((2 * N_HEADS, tm), lambda i: (0, i))]
    return pl.pallas_call(
        functools.partial(_inproj_kernel, n_lat=n_lat, seq=seq, ctx_len=ctx_len),
        grid=(t // tm,),
        in_specs=[
            pl.BlockSpec((tm, d), row),
            pl.BlockSpec((SUBLANES, d), lambda i: (jnp.maximum(i * per_tile - 1, 0), 0)),
            pl.BlockSpec((SUBLANES, d), lambda i: (jnp.minimum((i + 1) * per_tile, t // SUBLANES - 1), 0)),
            pl.BlockSpec((None, 1, mods.shape[2]), mod_idx),
            pl.BlockSpec(w.shape, const),
            pl.BlockSpec(b.shape, const),
            pl.BlockSpec(wgt.shape, const),
            pl.BlockSpec(bgt.shape, const),
            pl.BlockSpec(conv_w.shape, const),
            pl.BlockSpec(conv_b.shape, const),
        ],
        out_specs=out_specs,
        out_shape=out_shapes,
        scratch_shapes=[pltpu.VMEM((2, tm + 2 * SUBLANES, _QK_WIDTH // 2), F32)],
        compiler_params=_cparams("parallel"),
    )(x, x, x, mods, w, b, wgt, bgt, conv_w, conv_b)


def _mlstm_row_block(j, b, direction, *, n_batch, seq, ctx_len):
    ncc = ctx_len // CHUNK
    ncl = seq // CHUNK
    is_ctx = j < ncc
    if direction == 0:
        c = jnp.where(is_ctx, j, j - ncc)
    else:
        c = jnp.where(is_ctx, ncc - 1 - j, ncl - 1 - (j - ncc))
    base = jnp.where(is_ctx, (n_batch * seq + b * ctx_len) // CHUNK, (b * seq) // CHUNK)
    return base + c


def _mlstm_kernel(q_f, k_f, v_f, g_f, gt_f, q_b, k_b, v_b, g_b, gt_b, hf_ref, hb_ref, s_sc, m_sc):
    j = pl.program_id(1)

    @pl.when(j == 0)
    def _():
        s_sc[...] = jnp.zeros_like(s_sc)
        m_sc[...] = jnp.zeros_like(m_sc)

    ri = lax.broadcasted_iota(jnp.int32, (CHUNK, CHUNK), 0)
    ci = lax.broadcasted_iota(jnp.int32, (CHUNK, CHUNK), 1)
    ones_tile = jnp.ones((CHUNK, HEAD_DIM), BF16)
    dirs = ((q_f, k_f, v_f, g_f, gt_f, hf_ref), (q_b, k_b, v_b, g_b, gt_b, hb_ref))
    combos = [(d, hh) for d in range(2) for hh in range(N_HEADS)]
    heads = lambda ref, hh: ref[:, hh * HEAD_DIM:(hh + 1) * HEAD_DIM]

    nd = 2 * N_HEADS
    tri = ((ci <= ri), (ci >= ri))
    g = (g_f[...], g_b[...])
    gt = (gt_f[...], gt_b[...])

    qk_s, v1, state, m_prev = {}, {}, {}, {}
    for d, hh in combos:
        q_ref, k_ref, v_ref = dirs[d][:3]
        qk_s[d, hh] = lax.dot_general(heads(q_ref, hh), heads(k_ref, hh), (((1,), (1,)), ((), ())),
                                      preferred_element_type=F32)
        v1[d, hh] = jnp.concatenate([heads(v_ref, hh), ones_tile], axis=1)
        state[d, hh] = s_sc[d, hh]
        m_prev[d, hh] = m_sc[d * N_HEADS + hh][0:1, 0:1]

    q_state = {}
    for d, hh in combos:
        q_state[d, hh] = jnp.dot(heads(dirs[d][0], hh), state[d, hh].astype(BF16), preferred_element_type=F32)

    m_t, decay, scores = {}, {}, {}
    for d, hh in combos:
        col = d * N_HEADS + hh
        bcum = g[d][:, col:col + 1]
        inter = bcum + m_prev[d, hh]
        m_t[d, hh] = jnp.maximum(inter, g[d][:, 2 * nd + col:2 * nd + col + 1])
        decay[d, hh] = jnp.exp(inter - m_t[d, hh])
        dlog = jnp.where(tri[d], (bcum - m_t[d, hh]) + gt[d][col:col + 1, :], -jnp.inf)
        scores[d, hh] = (qk_s[d, hh] * jnp.exp(dlog)).astype(BF16)

    for d, hh in combos:
        numden = decay[d, hh] * q_state[d, hh] + jnp.dot(scores[d, hh], v1[d, hh], preferred_element_type=F32)
        num = numden[:, :HEAD_DIM]
        den = numden[:, HEAD_DIM:]
        out = num / jnp.maximum(jnp.abs(den), jnp.exp(-m_t[d, hh]))
        out_ref = dirs[d][5]
        out_ref[:, hh * HEAD_DIM:(hh + 1) * HEAD_DIM] = out.astype(out_ref.dtype)

    kw, a, m_new = {}, {}, {}
    for d, hh in combos:
        col = d * N_HEADS + hh
        last = CHUNK - 1 if d == 0 else 0
        total = g[d][last:last + 1, col:col + 1]
        m_new[d, hh] = total + jnp.maximum(m_prev[d, hh], g[d][last:last + 1, 3 * nd + col:3 * nd + col + 1])
        a[d, hh] = jnp.exp(total + m_prev[d, hh] - m_new[d, hh])
        wk = jnp.exp(total + g[d][:, nd + col:nd + col + 1] - m_new[d, hh])
        kw[d, hh] = (heads(dirs[d][1], hh).astype(F32) * wk).astype(BF16)

    for d, hh in combos:
        upd = lax.dot_general(kw[d, hh], v1[d, hh], (((0,), (0,)), ((), ())), preferred_element_type=F32)
        s_sc[d, hh] = a[d, hh] * state[d, hh] + upd
        m_sc[d * N_HEADS + hh] = jnp.broadcast_to(m_new[d, hh], m_sc.shape[1:])


def _mlstm(zqk, zv, g, gt, *, n_batch, seq, ctx_len):
    t = zqk.shape[0]
    nsteps = (seq + ctx_len) // CHUNK
    width = N_HEADS * HEAD_DIM
    idx = functools.partial(_mlstm_row_block, n_batch=n_batch, seq=seq, ctx_len=ctx_len)

    def specs(d):
        rb = lambda b, j: idx(j, b, d)
        return [
            pl.BlockSpec((CHUNK, width), lambda b, j: (rb(b, j), 0)),
            pl.BlockSpec((CHUNK, width), lambda b, j: (rb(b, j), 1)),
            pl.BlockSpec((CHUNK, width), lambda b, j: (rb(b, j), 0)),
            pl.BlockSpec((CHUNK, LANES), lambda b, j: (rb(b, j), 0)),
            pl.BlockSpec((2 * N_HEADS, CHUNK), lambda b, j: (0, rb(b, j))),
        ]

    out_spec = lambda d: pl.BlockSpec((CHUNK, width), lambda b, j: (idx(j, b, d), 0))
    return pl.pallas_call(
        _mlstm_kernel,
        grid=(n_batch, nsteps),
        in_specs=specs(0) + specs(1),
        out_specs=[out_spec(0), out_spec(1)],
        out_shape=[jax.ShapeDtypeStruct((t, width), BF16)] * 2,
        scratch_shapes=[
            pltpu.VMEM((2, N_HEADS, HEAD_DIM, 2 * HEAD_DIM), F32),
            pltpu.VMEM((2 * N_HEADS, 8, LANES), F32),
        ],
        compiler_params=_cparams("parallel", "arbitrary"),
    )(zqk, zqk, zv, g, gt, zqk, zqk, zv, g, gt)


def _pool_constants(ctx_len):
    tm = ROW_TILE
    masks = np.zeros((2, len(POOL_WINDOWS), tm, tm), np.float32)
    cnts = np.zeros((2, tm, len(POOL_WINDOWS) * POOL_GROUP_DIM), np.float32)
    for kind, length in enumerate((GRID_W, ctx_len)):
        for gi, w in enumerate(POOL_WINDOWS):
            for t in range(tm):
                base = (t // length) * length
                tl = t - base
                lo = min(max(tl - w // 2, 0), length)
                hi = min(max(tl + w // 2, 0), length)
                masks[kind, gi, t, base + lo:base + hi] = 1.0
                cnts[kind, t, gi * POOL_GROUP_DIM:(gi + 1) * POOL_GROUP_DIM] = hi - lo
    return jnp.asarray(masks, BF16), jnp.asarray(cnts, F32)


def _mix_kernel(x_ref, zp_ref, zuv_ref, zmg_ref, zo_ref, hf_ref, hb_ref, mod_ref,
                pm_ref, cnt_ref, pw_ref, ps_ref, sg_ref, sb_ref, sw_ref, sbm_ref, ng_ref,
                wbp_ref, wbm_ref, wbs_ref, wo_ref, l1g_ref, l1b_ref, wr_ref, br_ref,
                x1_ref, h2_ref, route_ref, cnt_out_ref, cnt_sc, *, alpha):
    tm, d = x_ref.shape
    i = pl.program_id(0)

    @pl.when(i == 0)
    def _():
        cnt_sc[...] = jnp.zeros_like(cnt_sc)

    lane256 = lax.broadcasted_iota(jnp.int32, (1, 4 * POOL_GROUP_DIM), 1) // POOL_GROUP_DIM

    xp = zp_ref[...]
    xp_f = xp.astype(F32)
    pooled = jnp.zeros(xp_f.shape, F32)
    for gi in range(len(POOL_WINDOWS)):
        s = jnp.dot(pm_ref[gi], xp, preferred_element_type=F32)
        pooled = jnp.where(lane256 == gi, s, pooled)
    diff = pooled / cnt_ref[...] - xp_f
    pool_o = jnp.dot(diff.astype(BF16), pw_ref[...], preferred_element_type=F32) * ps_ref[...]

    uv = zuv_ref[...].astype(F32)
    uv = 0.5 * uv * (1.0 + lax.erf(uv * (2.0 ** -0.5)))
    half = uv.shape[1] // 2
    u = uv[:, :half]
    vn = (_ln_plain(uv[:, half:]) * sg_ref[...] + sb_ref[...]).astype(BF16)
    mixed_chunks = []
    for c0 in range(0, tm, CHUNK):
        vc = vn[c0:c0 + CHUNK, :]
        mixed = jnp.zeros((CHUNK, half), F32)
        for gi in range(sw_ref.shape[0]):
            s = jnp.dot(sw_ref[gi], vc, preferred_element_type=F32)
            mixed = jnp.where(lane256 == gi, s, mixed)
        mixed_chunks.append(mixed + sbm_ref[...])
    sgu_o = u * jnp.concatenate(mixed_chunks, axis=0)

    hsum = hf_ref[...].astype(F32) + hb_ref[...].astype(F32)
    heads = []
    for hh in range(N_HEADS):
        hv = hsum[:, hh * HEAD_DIM:(hh + 1) * HEAD_DIM]
        mu = jnp.mean(hv, axis=-1, keepdims=True)
        hc = hv - mu
        var = jnp.mean(hc * hc, axis=-1, keepdims=True)
        heads.append(hc * lax.rsqrt(var + HEAD_NORM_EPS))
    hn = jnp.concatenate(heads, axis=1)
    twice_gate = lambda zh: jnp.tanh(zh.astype(F32)) + 1.0
    mlstm_o = hn * ng_ref[...] * twice_gate(zo_ref[...])

    y = twice_gate(zmg_ref[:, 0:d]) * jnp.dot(pool_o.astype(BF16), wbp_ref[...], preferred_element_type=F32)
    y += twice_gate(zmg_ref[:, d:2 * d]) * jnp.dot(mlstm_o.astype(BF16), wbm_ref[...],
                                                   preferred_element_type=F32)
    y += twice_gate(zmg_ref[:, 2 * d:3 * d]) * jnp.dot(sgu_o.astype(BF16), wbs_ref[...],
                                                       preferred_element_type=F32)
    y2 = jnp.dot(y.astype(BF16), wo_ref[...], preferred_element_type=F32)

    g1 = mod_ref[:, 2 * d:3 * d]
    x1 = _ln_plain(alpha * x_ref[...] + g1 * y2) * l1g_ref[...] + l1b_ref[...]
    x1_ref[...] = x1
    h2 = _ln_plain(x1) * (1.0 + mod_ref[:, 4 * d:5 * d]) + mod_ref[:, 3 * d:4 * d]
    h2b = h2.astype(BF16)
    h2_ref[...] = _pack_rows(h2b)

    logits = lax.dot_general(wr_ref[...], h2b, (((1,), (1,)), ((), ())),
                             preferred_element_type=F32) + br_ref[...]
    n_exp = logits.shape[0]
    expert = lax.broadcasted_iota(jnp.int32, logits.shape, 0)
    work = logits
    vals, idxs, hots = [], [], []
    for _ in range(TOP_K):
        mval = jnp.max(work, axis=0, keepdims=True)
        idx = jnp.min(jnp.where(work == mval, expert, n_exp), axis=0, keepdims=True)
        hot = expert == idx
        vals.append(mval)
        idxs.append(idx)
        hots.append(hot)
        work = jnp.where(hot, -jnp.inf, work)
    exps = [jnp.exp(v - vals[0]) for v in vals]
    denom = exps[0] + exps[1] + exps[2] + exps[3]
    sel = (hots[0] | hots[1] | hots[2] | hots[3]).astype(F32)
    r_i = lax.broadcasted_iota(jnp.int32, (tm, tm), 0)
    c_i = lax.broadcasted_iota(jnp.int32, (tm, tm), 1)
    earlier = (r_i < c_i).astype(BF16)
    rank = jnp.dot(sel.astype(BF16), earlier, preferred_element_type=F32) + cnt_sc[:, 0:1]
    rows = ([i_.astype(F32) for i_ in idxs] + [e / denom for e in exps]
            + [jnp.sum(jnp.where(h, rank, 0.0), axis=0, keepdims=True) for h in hots]
            + [jnp.zeros((route_ref.shape[0] - 3 * TOP_K, tm), F32)])
    route_ref[...] = jnp.concatenate(rows, axis=0)
    new_cnt = cnt_sc[:, 0:1] + jnp.sum(sel, axis=1, keepdims=True)
    cnt_sc[...] = jnp.broadcast_to(new_cnt, cnt_sc.shape)
    cnt_out_ref[...] = jnp.broadcast_to(new_cnt, cnt_out_ref.shape)


def _mix(x, zp, zuv, zmg, zo, hf, hb, mods, consts, *, n_rows, n_lat, seq, alpha):
    t, d = x.shape
    tm = ROW_TILE
    n_batch = mods.shape[0] - 1
    row = lambda i: (i, 0)
    const2 = lambda i: (0, 0)
    const3 = lambda i: (0, 0, 0)
    kind = lambda i: jnp.where(i * tm >= n_lat, 1, 0)

    def mod_idx(i):
        return (jnp.minimum((i * tm) // seq, n_batch), 0, 0)

    (pm, cnt, pw, ps, sg, sb, sw, sbm, ng, wbp, wbm, wbs, wo, l1g, l1b, wr, br) = consts
    in_specs = [
        pl.BlockSpec((tm, d), row),
        pl.BlockSpec((tm, zp.shape[1]), row),
        pl.BlockSpec((tm, zuv.shape[1]), row),
        pl.BlockSpec((tm, zmg.shape[1]), row),
        pl.BlockSpec((tm, zo.shape[1]), row),
        pl.BlockSpec((tm, hf.shape[1]), row),
        pl.BlockSpec((tm, hb.shape[1]), row),
        pl.BlockSpec((None, 1, mods.shape[2]), mod_idx),
        pl.BlockSpec((None,) + pm.shape[1:], lambda i: (kind(i), 0, 0, 0)),
        pl.BlockSpec((None,) + cnt.shape[1:], lambda i: (kind(i), 0, 0)),
        pl.BlockSpec(pw.shape, const2),
        pl.BlockSpec(ps.shape, const2),
        pl.BlockSpec(sg.shape, const2),
        pl.BlockSpec(sb.shape, const2),
        pl.BlockSpec(sw.shape, const3),
        pl.BlockSpec(sbm.shape, const2),
        pl.BlockSpec(ng.shape, const2),
        pl.BlockSpec(wbp.shape, const2),
        pl.BlockSpec(wbm.shape, const2),
        pl.BlockSpec(wbs.shape, const2),
        pl.BlockSpec(wo.shape, const2),
        pl.BlockSpec(l1g.shape, const2),
        pl.BlockSpec(l1b.shape, const2),
        pl.BlockSpec(wr.shape, const2),
        pl.BlockSpec(br.shape, const2),
    ]
    return pl.pallas_call(
        functools.partial(_mix_kernel, alpha=alpha),
        grid=(n_rows // tm,),
        in_specs=in_specs,
        out_specs=[
            pl.BlockSpec((tm, d), row),
            pl.BlockSpec((tm, d // 2), row),
            pl.BlockSpec((ROUTE_ROWS, tm), lambda i: (0, i)),
            pl.BlockSpec((wr.shape[0], LANES), const2),
        ],
        out_shape=[
            jax.ShapeDtypeStruct((n_rows, d), F32),
            jax.ShapeDtypeStruct((n_rows, d // 2), jnp.uint32),
            jax.ShapeDtypeStruct((ROUTE_ROWS, n_rows), F32),
            jax.ShapeDtypeStruct((wr.shape[0], LANES), F32),
        ],
        scratch_shapes=[pltpu.VMEM((wr.shape[0], LANES), F32)],
        compiler_params=_cparams("arbitrary"),
    )(x, zp, zuv, zmg, zo, hf, hb, mods, pm, cnt, pw, ps, sg, sb, sw, sbm, ng,
      wbp, wbm, wbs, wo, l1g, l1b, wr, br)


SCATTER_WINDOW = 128
SCATTER_PARTS = 2


def _scatter_rows(x, idx, n_out, part):
    n = x.shape[0]
    w = x.shape[1] // SCATTER_PARTS
    n_idx = idx.shape[0]
    mesh = plsc.VectorSubcoreMesh(core_axis_name="core", subcore_axis_name="subcore")

    @pl.kernel(out_type=jax.ShapeDtypeStruct((n_out, w), x.dtype), mesh=mesh, scratch_types=[])
    def scatter_kernel(x_hbm, i_hbm, o_hbm):
        def body(x_vmem, i_vmem):
            for k in range(n_idx):
                pltpu.sync_copy(x_vmem, o_hbm.at[i_vmem.at[k]])

        pltpu.emit_pipeline(
            body,
            grid=(n // SCATTER_WINDOW,),
            in_specs=[pl.BlockSpec((SCATTER_WINDOW, w), lambda i: (i, part)),
                      pl.BlockSpec((n_idx, SCATTER_WINDOW), lambda i: (0, i))],
            out_specs=[],
            core_axis_name=("core", "subcore"),
            dimension_semantics=(pltpu.PARALLEL,),
        )(x_hbm, i_hbm)

    return scatter_kernel(x, idx)


def _gather_rows(x, idx):
    w = x.shape[1]
    n_idx = idx.shape[1]
    mesh = plsc.VectorSubcoreMesh(core_axis_name="core", subcore_axis_name="subcore")

    @pl.kernel(out_type=jax.ShapeDtypeStruct((n_idx, w), x.dtype), mesh=mesh, scratch_types=[])
    def gather_kernel(x_hbm, i_hbm, o_hbm):
        def body(i_vmem, o_vmem):
            pltpu.sync_copy(x_hbm.at[i_vmem.at[0]], o_vmem)

        pltpu.emit_pipeline(
            body,
            grid=(n_idx // SCATTER_WINDOW,),
            in_specs=[pl.BlockSpec((1, SCATTER_WINDOW), lambda i: (0, i))],
            out_specs=[pl.BlockSpec((SCATTER_WINDOW, w), lambda i: (i, 0))],
            core_axis_name=("core", "subcore"),
            dimension_semantics=(pltpu.PARALLEL,),
        )(i_hbm, o_hbm)

    return gather_kernel(x, idx)


def _expert_kernel(te_ref, nt_ref, xa_ref, xb_ref, wgu_ref, bgu_ref, wd_ref, bd_ref, ya_ref, yb_ref,
                   wgu_sc, wd_sc):
    i = pl.program_id(0)
    f = wd_ref.shape[0]
    used = i < nt_ref[0]

    @pl.when(jnp.logical_and(used, jnp.logical_or(i == 0, te_ref[i] != te_ref[jnp.maximum(i - 1, 0)])))
    def _():
        for c0 in range(0, 2 * f, 512):
            wgu_sc[:, c0:c0 + 512] = wgu_ref[:, c0:c0 + 512].astype(BF16)
        for c0 in range(0, f, 256):
            wd_sc[c0:c0 + 256, :] = wd_ref[c0:c0 + 256, :].astype(BF16)

    @pl.when(used)
    def _():
        x = _unpack_rows([xa_ref[...], xb_ref[...]]).astype(BF16)
        acc = jnp.zeros((x.shape[0], wd_ref.shape[1]), F32)
        for c0 in range(0, f, 512):
            gate = jnp.dot(x, wgu_sc[:, c0:c0 + 512], preferred_element_type=F32) + bgu_ref[:, c0:c0 + 512]
            up = (jnp.dot(x, wgu_sc[:, f + c0:f + c0 + 512], preferred_element_type=F32)
                  + bgu_ref[:, f + c0:f + c0 + 512])
            gate = jnp.minimum(gate, SWIGLU_LIMIT)
            up = jnp.clip(up, -SWIGLU_LIMIT, SWIGLU_LIMIT)
            glu = gate * _sigmoid(SWIGLU_ALPHA * gate)
            act = ((up + 1.0) * glu).astype(BF16)
            acc += jnp.dot(act, wd_sc[c0:c0 + 512, :], preferred_element_type=F32)
        words = _pack_rows(acc + bd_ref[...])
        ya_ref[...] = words[:, :ya_ref.shape[1]]
        yb_ref[...] = words[:, ya_ref.shape[1]:]

    @pl.when(jnp.logical_not(used))
    def _():
        ya_ref[...] = jnp.zeros_like(ya_ref)
        yb_ref[...] = jnp.zeros_like(yb_ref)


def _experts(tile_expert, n_tiles_used, xa, xb, wgu, bgu, wd, bd, *, layer):
    p = xa.shape[0]
    d = 2 * SCATTER_PARTS * xa.shape[1]
    tm = EXPERT_TILE
    f2 = wgu.shape[3]
    grid_spec = pltpu.PrefetchScalarGridSpec(
        num_scalar_prefetch=2,
        grid=(p // tm,),
        in_specs=[
            pl.BlockSpec((tm, xa.shape[1]), lambda i, te, nt: (i, 0)),
            pl.BlockSpec((tm, xb.shape[1]), lambda i, te, nt: (i, 0)),
            pl.BlockSpec((None, None, d, f2), lambda i, te, nt: (layer, te[i], 0, 0)),
            pl.BlockSpec((None, None, 1, f2), lambda i, te, nt: (layer, te[i], 0, 0)),
            pl.BlockSpec((None, None, f2 // 2, d), lambda i, te, nt: (layer, te[i], 0, 0)),
            pl.BlockSpec((None, None, 1, d), lambda i, te, nt: (layer, te[i], 0, 0)),
        ],
        out_specs=[pl.BlockSpec((tm, xa.shape[1]), lambda i, te, nt: (i, 0)),
                   pl.BlockSpec((tm, xb.shape[1]), lambda i, te, nt: (i, 0))],
        scratch_shapes=[pltpu.VMEM((d, f2), BF16), pltpu.VMEM((f2 // 2, d), BF16)],
    )
    return pl.pallas_call(
        _expert_kernel,
        grid_spec=grid_spec,
        out_shape=[jax.ShapeDtypeStruct(xa.shape, xa.dtype), jax.ShapeDtypeStruct(xb.shape, xb.dtype)],
        compiler_params=_cparams("arbitrary"),
    )(tile_expert, n_tiles_used, xa, xb, wgu, bgu, wd, bd)


def _combine_kernel(x1_ref, ya_ref, yb_ref, route_ref, mod_ref, g_ref, b_ref, *rest, alpha):
    x2_ref = rest[-1]
    d = x1_ref.shape[1]
    tm = x1_ref.shape[0]
    route = jnp.concatenate([route_ref[...], jnp.zeros((LANES - ROUTE_ROWS, tm), F32)], axis=0).T
    f = jnp.zeros(x1_ref.shape, F32)
    for kk in range(TOP_K):
        f += route[:, TOP_K + kk:TOP_K + kk + 1] * _unpack_rows([ya_ref[kk], yb_ref[kk]])
    g2 = mod_ref[:, 5 * d:6 * d]
    x2_ref[...] = _ln_plain(alpha * x1_ref[...] + g2 * f) * g_ref[...] + b_ref[...]


def _combine(x1, yg_a, yg_b, route, mods, g, b, *, seq, alpha, dest=None, dest_rows=None, tile_offset=0):
    t, d = x1.shape
    tm = COMBINE_TILE
    n_batch = mods.shape[0] - 1
    row = lambda i: (i, 0)

    def mod_idx(i):
        return (jnp.minimum((i * tm) // seq, n_batch), 0, 0)

    in_specs = [
        pl.BlockSpec((tm, d), row),
        pl.BlockSpec((TOP_K, tm, yg_a.shape[2]), lambda i: (0, i, 0)),
        pl.BlockSpec((TOP_K, tm, yg_b.shape[2]), lambda i: (0, i, 0)),
        pl.BlockSpec((ROUTE_ROWS, tm), lambda i: (0, i)),
        pl.BlockSpec((None, 1, mods.shape[2]), mod_idx),
        pl.BlockSpec(g.shape, lambda i: (0, 0)),
        pl.BlockSpec(b.shape, lambda i: (0, 0)),
    ]
    args = [x1, yg_a, yg_b, route, mods, g, b]
    aliases = {}
    if dest is not None:
        in_specs.append(pl.BlockSpec(memory_space=pl.ANY))
        args.append(dest)
        aliases = {len(args) - 1: 0}
    return pl.pallas_call(
        functools.partial(_combine_kernel, alpha=alpha),
        grid=(t // tm,),
        in_specs=in_specs,
        out_specs=pl.BlockSpec((tm, d), lambda i: (i + tile_offset, 0)),
        out_shape=jax.ShapeDtypeStruct((t if dest_rows is None else dest_rows, d), F32),
        input_output_aliases=aliases,
        compiler_params=_cparams("parallel"),
    )(*args)


def kernel(x, c, ctx, c_ctx, w_ada, b_ada, w_in, b_in, pool_w, pool_scale, qk_conv_w, qk_conv_b,
           mlstm_norm_g, sgu_ln_g, sgu_ln_b, sgu_w, sgu_b, w_br_pool, w_br_mlstm, w_br_sgu, w_out,
           ln1_g, ln1_b, w_router, b_router, w_gate_up, b_gate_up, w_down, b_down, ln2_g, ln2_b):
    n_batch, seq, d = x.shape
    ctx_len = ctx.shape[1]
    depth = w_in.shape[0]
    n_exp = w_router.shape[2]
    assert ctx_len == ROW_TILE and seq % ROW_TILE == 0 and ROW_TILE % GRID_W == 0
    assert n_batch % N_STREAMS == 0
    alpha = float((2 * depth) ** 0.25)

    rows = ((n_batch + 1 + 7) // 8) * 8
    cc = jnp.zeros((rows, d), F32).at[:n_batch].set(c).at[n_batch].set(c_ctx)
    mods_all = _ada_table(cc, w_ada, b_ada)[:, :n_batch + 1].reshape(depth, n_batch + 1, 1, 6 * d)

    pool_w_, mlw = 256, 512
    col = lambda a, lo, hi: a[..., lo:hi]
    g_lo = pool_w_ + 4 * mlw
    parts = ((pool_w_, pool_w_ + 2 * mlw), (pool_w_ + 2 * mlw, pool_w_ + 3 * mlw), (pool_w_ + 3 * mlw, g_lo),
             (g_lo + _N_GATE_COLS, g_lo + _N_GATE_COLS + 512), (0, pool_w_),
             (g_lo + _N_GATE_COLS + 512, w_in.shape[2]))
    part_scale = (1.0, 1.0, 0.5, 1.0, 1.0, 0.5)
    w_main = jnp.concatenate([col(w_in, lo, hi) * sc for (lo, hi), sc in zip(parts, part_scale)],
                             axis=-1).astype(BF16)
    b_main = jnp.concatenate([col(b_in, lo, hi) * sc for (lo, hi), sc in zip(parts, part_scale)],
                             axis=-1)[:, None, :]
    gw, gb = col(w_in, g_lo, g_lo + _N_GATE_COLS), col(b_in, g_lo, g_lo + _N_GATE_COLS)
    nh = N_HEADS
    pick = lambda a, k: jnp.concatenate([a[..., k * nh:(k + 1) * nh], a[..., (k + 2) * nh:(k + 3) * nh]], axis=-1)
    w_gate_t = jnp.swapaxes(jnp.concatenate([pick(gw, 0), pick(gw, 1)], axis=-1), 1, 2).astype(BF16)
    b_gate_t = jnp.concatenate([pick(gb, 0), pick(gb, 1)], axis=-1)[:, :, None]

    pm, cnt = _pool_constants(ctx_len)
    eye = jnp.eye(len(POOL_WINDOWS), dtype=F32)
    pool_bd = jnp.einsum('lgcd,gh->lgchd', pool_w, eye).reshape(depth, 256, 256).astype(BF16)
    sgu_bias_map = jnp.repeat(jnp.swapaxes(sgu_b, 1, 2), sgu_ln_g.shape[1] // sgu_b.shape[1], axis=2)
    w_router_t = jnp.swapaxes(w_router, 1, 2).astype(BF16)
    b_router_t = b_router[:, :, None]
    r1 = lambda a: a[:, None, :]
    b_gu4 = b_gate_up[:, :, None, :]
    b_d4 = b_down[:, :, None, :]

    nb = n_batch // N_STREAMS
    n_lat = nb * seq
    t_all = n_lat + nb * ctx_len
    streams, stream_mods = [], []
    for h in range(N_STREAMS):
        sl = slice(h * nb, (h + 1) * nb)
        streams.append(jnp.concatenate([x[sl].reshape(n_lat, d), ctx[sl].reshape(nb * ctx_len, d)], axis=0))
        stream_mods.append(jnp.concatenate([mods_all[:, sl], mods_all[:, n_batch:n_batch + 1]], axis=1))
    out = None
    for i in range(depth):
        last = i == depth - 1
        for h in range(N_STREAMS):
            stream = streams[h]
            mods = stream_mods[h][i]
            zqk, zv, zo, zuv, zp, zmg, g, gt = _inproj(
                stream, mods, w_main[i], b_main[i], w_gate_t[i], b_gate_t[i],
                qk_conv_w[i], r1(qk_conv_b)[i], n_lat=n_lat, seq=seq, ctx_len=ctx_len)
            hf, hb = _mlstm(zqk, zv, g, gt, n_batch=nb, seq=seq, ctx_len=ctx_len)
            n_rows = n_lat if last else t_all
            consts = (pm, cnt, pool_bd[i], r1(pool_scale)[i], r1(sgu_ln_g)[i], r1(sgu_ln_b)[i],
                      sgu_w[i].astype(BF16), sgu_bias_map[i], r1(0.5 * mlstm_norm_g)[i],
                      w_br_pool[i].astype(BF16), w_br_mlstm[i].astype(BF16), w_br_sgu[i].astype(BF16),
                      (0.5 * w_out[i]).astype(BF16), r1(ln1_g)[i], r1(ln1_b)[i], w_router_t[i], b_router_t[i])
            x1, h2, route, counts = _mix(stream, zp, zuv, zmg, zo, hf, hb, mods, consts,
                                         n_rows=n_rows, n_lat=n_lat, seq=seq, alpha=alpha)

            cnt_e = counts[:, 0].astype(jnp.int32)
            padded = ((cnt_e + EXPERT_TILE - 1) // EXPERT_TILE) * EXPERT_TILE
            ends = jnp.cumsum(padded)
            starts = ends - padded
            e_idx = route[0:TOP_K].astype(jnp.int32)
            rank = route[2 * TOP_K:3 * TOP_K].astype(jnp.int32)
            pos_t = starts[e_idx] + rank
            p_rows = n_rows * TOP_K + n_exp * EXPERT_TILE
            n_tiles = p_rows // EXPERT_TILE
            tile_start = jnp.arange(n_tiles, dtype=jnp.int32) * EXPERT_TILE
            tile_expert = jnp.minimum(
                jnp.sum((ends[None, :] <= tile_start[:, None]).astype(jnp.int32), axis=1), n_exp - 1)
            n_used = (ends[-1] // EXPERT_TILE).astype(jnp.int32).reshape(1)
            xs_a = _scatter_rows(h2, pos_t, p_rows, 0)
            xs_b = _scatter_rows(h2, pos_t, p_rows, 1)
            ys_a, ys_b = _experts(tile_expert, n_used, xs_a, xs_b, w_gate_up, b_gu4, w_down, b_d4, layer=i)
            pos_flat = pos_t.reshape(1, TOP_K * n_rows)
            yg_a = _gather_rows(ys_a, pos_flat).reshape(TOP_K, n_rows, -1)
            yg_b = _gather_rows(ys_b, pos_flat).reshape(TOP_K, n_rows, -1)
            if last:
                out = _combine(x1, yg_a, yg_b, route, mods, r1(ln2_g)[i], r1(ln2_b)[i], seq=seq, alpha=alpha,
                               dest=out, dest_rows=n_batch * seq, tile_offset=h * n_lat // COMBINE_TILE)
            else:
                streams[h] = _combine(x1, yg_a, yg_b, route, mods, r1(ln2_g)[i], r1(ln2_b)[i],
                                      seq=seq, alpha=alpha)
    return out.reshape(n_batch, seq, d)
```

```python
import functools

import numpy as np
import jax
import jax.numpy as jnp
from jax import lax
from jax.experimental import pallas as pl
from jax.experimental.pallas import tpu as pltpu
from jax.experimental.pallas import tpu_sc as plsc

F32 = jnp.float32
BF16 = jnp.bfloat16

GRID_W = 64
POOL_WINDOWS = (2, 4, 8, 16)
POOL_GROUP_DIM = 64
N_HEADS = 4
HEAD_DIM = 128
CHUNK = 128
TOP_K = 4
SWIGLU_LIMIT = 7.0
SWIGLU_ALPHA = 1.702
LN_EPS = 1e-5
HEAD_NORM_EPS = 1e-6

ROW_TILE = 256
INPROJ_TILE = 512
COMBINE_TILE = 512
SEAM_ROWS = 16
ROUTE_ROWS = 16
EXPERT_TILE = 512
N_STREAMS = 2
SUBLANES = 8
LANES = 128
VMEM_LIMIT = 56 * 1024 * 1024


def _cparams(*sem):
    return pltpu.CompilerParams(dimension_semantics=sem, vmem_limit_bytes=VMEM_LIMIT)


def _ln_plain(x):
    mu = jnp.mean(x, axis=-1, keepdims=True)
    xc = x - mu
    var = jnp.mean(xc * xc, axis=-1, keepdims=True)
    return xc * lax.rsqrt(var + LN_EPS)


def _sigmoid(x):
    return 0.5 * jnp.tanh(0.5 * x) + 0.5


def _log_sigmoid(x):
    return jnp.minimum(x, 0.0) - jnp.log1p(jnp.exp(-jnp.abs(x)))


def _pack_rows(x):
    half = x.shape[1] // 2
    bits = lax.bitcast_convert_type(x.astype(BF16).astype(F32), jnp.uint32)
    return (bits[:, half:] & jnp.uint32(0xFFFF0000)) | (bits[:, :half] >> 16)


def _unpack_rows(parts):
    low = [lax.bitcast_convert_type(w << 16, F32) for w in parts]
    high = [lax.bitcast_convert_type(w & jnp.uint32(0xFFFF0000), F32) for w in parts]
    return jnp.concatenate(low + high, axis=1)


def _ada_kernel(c_ref, w_ref, b_ref, o_ref):
    c = c_ref[...]
    s = c * _sigmoid(c)
    o_ref[...] = jnp.dot(s, w_ref[...], preferred_element_type=F32,
                         precision=lax.Precision.HIGHEST) + b_ref[...]


def _ada_table(cc, w_ada, b_ada):
    depth, d, n = w_ada.shape
    rows = cc.shape[0]
    bn = 1536
    return pl.pallas_call(
        _ada_kernel,
        grid=(depth, n // bn),
        in_specs=[
            pl.BlockSpec((rows, d), lambda l, j: (0, 0)),
            pl.BlockSpec((None, d, bn), lambda l, j: (l, 0, j)),
            pl.BlockSpec((None, 1, bn), lambda l, j: (l, 0, j)),
        ],
        out_specs=pl.BlockSpec((None, rows, bn), lambda l, j: (l, 0, j)),
        out_shape=jax.ShapeDtypeStruct((depth, rows, n), F32),
        compiler_params=_cparams("parallel", "parallel"),
    )(cc, w_ada, b_ada.reshape(depth, 1, n))


_QK_WIDTH = 2 * N_HEADS * HEAD_DIM
_Z_PARTS = (("v", 512), ("o", 512), ("uv", 512), ("pool", 256), ("mg", 3072))
_N_GATE_COLS = 4 * N_HEADS


def _inproj_kernel(x_ref, xp_ref, xn_ref, mod_ref, w_ref, b_ref, wgt_ref, bgt_ref,
                   cw_ref, cb_ref, qk_ref, v_ref, o_ref, uv_ref, p_ref, mg_ref, g_ref, gt_ref, pad_sc,
                   *, n_lat, seq, ctx_len):
    tm, d = x_ref.shape
    i = pl.program_id(0)
    is_lat = i * tm < n_lat
    has_prev = jnp.logical_and(is_lat, (i * tm) % seq != 0).astype(F32)
    has_next = jnp.logical_and(is_lat, ((i + 1) * tm) % seq != 0).astype(F32)

    xe = jnp.concatenate([xp_ref[...], x_ref[...], xn_ref[...]], axis=0)
    he = (_ln_plain(xe) * (1.0 + mod_ref[:, d:2 * d]) + mod_ref[:, 0:d]).astype(BF16)
    hb = he[SUBLANES:SUBLANES + tm, :]

    half = _QK_WIDTH // 2

    def qk_project(part):
        c0 = part * half
        z = jnp.dot(he, w_ref[:, c0:c0 + half], preferred_element_type=F32) + b_ref[:, c0:c0 + half]
        pad_sc[part] = z
        pad_sc[part, SUBLANES - 1:SUBLANES, :] = pad_sc[part, SUBLANES - 1:SUBLANES, :] * has_prev
        pad_sc[part, SUBLANES + tm:SUBLANES + tm + 1, :] = (
            pad_sc[part, SUBLANES + tm:SUBLANES + tm + 1, :] * has_next)

    def conv_act(part, prev, cur, nxt):
        c0 = part * half
        y = (cb_ref[:, c0:c0 + half] + prev * cw_ref[0:1, c0:c0 + half]
             + cur * cw_ref[1:2, c0:c0 + half] + nxt * cw_ref[2:3, c0:c0 + half])
        act = y * _sigmoid(y)
        if part == 1:
            act = act * (HEAD_DIM ** -0.5)
        return act.astype(qk_ref.dtype)

    def qk_conv(part):
        qk_ref[:, part * half:(part + 1) * half] = conv_act(
            part, pad_sc[part, SUBLANES - 1:SUBLANES - 1 + tm, :], pad_sc[part, SUBLANES:SUBLANES + tm, :],
            pad_sc[part, SUBLANES + 1:SUBLANES + 1 + tm, :])

    def plain(ref, off, c0, cw):
        z = jnp.dot(hb, w_ref[:, off + c0:off + c0 + cw], preferred_element_type=F32)
        ref[:, c0:c0 + cw] = (z + b_ref[:, off + c0:off + c0 + cw]).astype(ref.dtype)

    mxu_tasks = []
    off = _QK_WIDTH
    for ref, (_, width) in zip((v_ref, o_ref, uv_ref, p_ref, mg_ref), _Z_PARTS):
        for c0 in range(0, width, 512):
            mxu_tasks.append(functools.partial(plain, ref, off, c0, min(512, width - c0)))
        off += width
    vpu_tasks = {2: functools.partial(_gate_prep, hb, wgt_ref, bgt_ref, g_ref, gt_ref),
                 4: functools.partial(qk_conv, 0), 6: functools.partial(qk_conv, 1)}
    qk_project(0)
    qk_project(1)
    for n, task in enumerate(mxu_tasks):
        if n in vpu_tasks:
            vpu_tasks[n]()
        task()

    if tm > ctx_len:
        @pl.when(jnp.logical_not(is_lat))
        def _():
            r = lax.broadcasted_iota(jnp.int32, (2 * SEAM_ROWS, 1), 0)
            for part in range(2):
                for p in range(ctx_len, tm, ctx_len):
                    lo = SUBLANES + p - SEAM_ROWS
                    prev = jnp.where(r == SEAM_ROWS, 0.0, pad_sc[part, lo - 1:lo - 1 + 2 * SEAM_ROWS, :])
                    nxt = jnp.where(r == SEAM_ROWS - 1, 0.0, pad_sc[part, lo + 1:lo + 1 + 2 * SEAM_ROWS, :])
                    qk_ref[p - SEAM_ROWS:p + SEAM_ROWS, part * half:(part + 1) * half] = conv_act(
                        part, prev, pad_sc[part, lo:lo + 2 * SEAM_ROWS, :], nxt)


def _gate_prep(hb, wgt_ref, bgt_ref, g_ref, gt_ref):
    tm = hb.shape[0]
    nd = 2 * N_HEADS
    ggt = lax.dot_general(wgt_ref[...], hb, (((1,), (1,)), ((), ())),
                          preferred_element_type=F32) + bgt_ref[...]
    git, gft = ggt[:nd, :], ggt[nd:, :]
    fwd = lax.broadcasted_iota(jnp.int32, (nd, tm), 0) < N_HEADS
    pos = lax.broadcasted_iota(jnp.int32, (nd, tm), 1) % CHUNK
    lsg = _log_sigmoid(gft)
    bcum = jnp.where(fwd, _chunk_scan(lsg, pos, 1, False, False), _chunk_scan(lsg, pos, 1, True, False))
    c = git - bcum
    cmax = jnp.where(fwd, _chunk_scan(c, pos, 1, False, True), _chunk_scan(c, pos, 1, True, True))
    gt_ref[...] = c
    rows = jnp.concatenate([bcum, c, bcum + cmax, cmax, jnp.zeros((LANES - 4 * nd, tm), F32)], axis=0)
    g_ref[...] = rows.T


def _chunk_scan(a, pos, axis, reverse, use_max):
    n = a.shape[axis]
    ident = -jnp.inf if use_max else 0.0
    sh = 1
    while sh < CHUNK:
        if reverse:
            moved = jnp.where(pos < CHUNK - sh, pltpu.roll(a, n - sh, axis=axis), ident)
        else:
            moved = jnp.where(pos >= sh, pltpu.roll(a, sh, axis=axis), ident)
        a = jnp.maximum(a, moved) if use_max else a + moved
        sh *= 2
    return a


def _inproj(x, mods, w, b, wgt, bgt, conv_w, conv_b, *, n_lat, seq, ctx_len):
    t, d = x.shape
    tm = INPROJ_TILE
    assert seq % tm == 0 and n_lat % tm == 0 and (t - n_lat) % tm == 0 and tm % ctx_len == 0
    n_batch = mods.shape[0] - 1
    per_tile = tm // SUBLANES

    def mod_idx(i):
        return (jnp.minimum((i * tm) // seq, n_batch), 0, 0)

    const = lambda i: (0, 0)
    row = lambda i: (i, 0)
    parts = (("qk", _QK_WIDTH),) + _Z_PARTS
    out_shapes = [jax.ShapeDtypeStruct((t, width), BF16) for _, width in parts]
    out_specs = [pl.BlockSpec((tm, width), row) for _, width in parts]
    out_shapes += [jax.ShapeDtypeStruct((t, LANES), F32), jax.ShapeDtypeStruct((2 * N_HEADS, t), F32)]
    out_specs += [pl.BlockSpec((tm, LANES), row), pl.BlockSpec((2 * N_HEADS, tm), lambda i: (0, i))]
    return pl.pallas_call(
        functools.partial(_inproj_kernel, n_lat=n_lat, seq=seq, ctx_len=ctx_len),
        grid=(t // tm,),
        in_specs=[
            pl.BlockSpec((tm, d), row),
            pl.BlockSpec((SUBLANES, d), lambda i: (jnp.maximum(i * per_tile - 1, 0), 0)),
            pl.BlockSpec((SUBLANES, d), lambda i: (jnp.minimum((i + 1) * per_tile, t // SUBLANES - 1), 0)),
            pl.BlockSpec((None, 1, mods.shape[2]), mod_idx),
            pl.BlockSpec(w.shape, const),
            pl.BlockSpec(b.shape, const),
            pl.BlockSpec(wgt.shape, const),
            pl.BlockSpec(bgt.shape, const),
            pl.BlockSpec(conv_w.shape, const),
            pl.BlockSpec(conv_b.shape, const),
        ],
        out_specs=out_specs,
        out_shape=out_shapes,
        scratch_shapes=[pltpu.VMEM((2, tm + 2 * SUBLANES, _QK_WIDTH // 2), F32)],
        compiler_params=_cparams("parallel"),
    )(x, x, x, mods, w, b, wgt, bgt, conv_w, conv_b)


def _mlstm_row_block(j, b, direction, *, n_batch, seq, ctx_len):
    ncc = ctx_len // CHUNK
    ncl = seq // CHUNK
    is_ctx = j < ncc
    if direction == 0:
        c = jnp.where(is_ctx, j, j - ncc)
    else:
        c = jnp.where(is_ctx, ncc - 1 - j, ncl - 1 - (j - ncc))
    base = jnp.where(is_ctx, (n_batch * seq + b * ctx_len) // CHUNK, (b * seq) // CHUNK)
    return base + c


def _mlstm_kernel(q_f, k_f, v_f, g_f, gt_f, q_b, k_b, v_b, g_b, gt_b, hf_ref, hb_ref, s_sc, m_sc):
    j = pl.program_id(1)

    @pl.when(j == 0)
    def _():
        s_sc[...] = jnp.zeros_like(s_sc)
        m_sc[...] = jnp.zeros_like(m_sc)

    ri = lax.broadcasted_iota(jnp.int32, (CHUNK, CHUNK), 0)
    ci = lax.broadcasted_iota(jnp.int32, (CHUNK, CHUNK), 1)
    ones_tile = jnp.ones((CHUNK, HEAD_DIM), BF16)
    dirs = ((q_f, k_f, v_f, g_f, gt_f, hf_ref), (q_b, k_b, v_b, g_b, gt_b, hb_ref))
    combos = [(d, hh) for d in range(2) for hh in range(N_HEADS)]
    heads = lambda ref, hh: ref[:, hh * HEAD_DIM:(hh + 1) * HEAD_DIM]

    nd = 2 * N_HEADS
    tri = ((ci <= ri), (ci >= ri))
    g = (g_f[...], g_b[...])
    gt = (gt_f[...], gt_b[...])

    qk_s, v1, state, m_prev = {}, {}, {}, {}
    for d, hh in combos:
        q_ref, k_ref, v_ref = dirs[d][:3]
        qk_s[d, hh] = lax.dot_general(heads(q_ref, hh), heads(k_ref, hh), (((1,), (1,)), ((), ())),
                                      preferred_element_type=F32)
        v1[d, hh] = jnp.concatenate([heads(v_ref, hh), ones_tile], axis=1)
        state[d, hh] = s_sc[d, hh]
        m_prev[d, hh] = m_sc[d * N_HEADS + hh][0:1, 0:1]

    q_state = {}
    for d, hh in combos:
        q_state[d, hh] = jnp.dot(heads(dirs[d][0], hh), state[d, hh].astype(BF16), preferred_element_type=F32)

    m_t, decay, scores = {}, {}, {}
    for d, hh in combos:
        col = d * N_HEADS + hh
        bcum = g[d][:, col:col + 1]
        inter = bcum + m_prev[d, hh]
        m_t[d, hh] = jnp.maximum(inter, g[d][:, 2 * nd + col:2 * nd + col + 1])
        decay[d, hh] = jnp.exp(inter - m_t[d, hh])
        dlog = jnp.where(tri[d], (bcum - m_t[d, hh]) + gt[d][col:col + 1, :], -jnp.inf)
        scores[d, hh] = (qk_s[d, hh] * jnp.exp(dlog)).astype(BF16)

    for d, hh in combos:
        numden = decay[d, hh] * q_state[d, hh] + jnp.dot(scores[d, hh], v1[d, hh], preferred_element_type=F32)
        num = numden[:, :HEAD_DIM]
        den = numden[:, HEAD_DIM:]
        out = num / jnp.maximum(jnp.abs(den), jnp.exp(-m_t[d, hh]))
        out_ref = dirs[d][5]
        out_ref[:, hh * HEAD_DIM:(hh + 1) * HEAD_DIM] = out.astype(out_ref.dtype)

    kw, a, m_new = {}, {}, {}
    for d, hh in combos:
        col = d * N_HEADS + hh
        last = CHUNK - 1 if d == 0 else 0
        total = g[d][last:last + 1, col:col + 1]
        m_new[d, hh] = total + jnp.maximum(m_prev[d, hh], g[d][last:last + 1, 3 * nd + col:3 * nd + col + 1])
        a[d, hh] = jnp.exp(total + m_prev[d, hh] - m_new[d, hh])
        wk = jnp.exp(total + g[d][:, nd + col:nd + col + 1] - m_new[d, hh])
        kw[d, hh] = (heads(dirs[d][1], hh).astype(F32) * wk).astype(BF16)

    for d, hh in combos:
        upd = lax.dot_general(kw[d, hh], v1[d, hh], (((0,), (0,)), ((), ())), preferred_element_type=F32)
        s_sc[d, hh] = a[d, hh] * state[d, hh] + upd
        m_sc[d * N_HEADS + hh] = jnp.broadcast_to(m_new[d, hh], m_sc.shape[1:])


def _mlstm(zqk, zv, g, gt, *, n_batch, seq, ctx_len):
    t = zqk.shape[0]
    nsteps = (seq + ctx_len) // CHUNK
    width = N_HEADS * HEAD_DIM
    idx = functools.partial(_mlstm_row_block, n_batch=n_batch, seq=seq, ctx_len=ctx_len)

    def specs(d):
        rb = lambda b, j: idx(j, b, d)
        return [
            pl.BlockSpec((CHUNK, width), lambda b, j: (rb(b, j), 0)),
            pl.BlockSpec((CHUNK, width), lambda b, j: (rb(b, j), 1)),
            pl.BlockSpec((CHUNK, width), lambda b, j: (rb(b, j), 0)),
            pl.BlockSpec((CHUNK, LANES), lambda b, j: (rb(b, j), 0)),
            pl.BlockSpec((2 * N_HEADS, CHUNK), lambda b, j: (0, rb(b, j))),
        ]

    out_spec = lambda d: pl.BlockSpec((CHUNK, width), lambda b, j: (idx(j, b, d), 0))
    return pl.pallas_call(
        _mlstm_kernel,
        grid=(n_batch, nsteps),
        in_specs=specs(0) + specs(1),
        out_specs=[out_spec(0), out_spec(1)],
        out_shape=[jax.ShapeDtypeStruct((t, width), BF16)] * 2,
        scratch_shapes=[
            pltpu.VMEM((2, N_HEADS, HEAD_DIM, 2 * HEAD_DIM), F32),
            pltpu.VMEM((2 * N_HEADS, 8, LANES), F32),
        ],
        compiler_params=_cparams("parallel", "arbitrary"),
    )(zqk, zqk, zv, g, gt, zqk, zqk, zv, g, gt)


def _pool_constants(ctx_len):
    tm = ROW_TILE
    masks = np.zeros((2, len(POOL_WINDOWS), tm, tm), np.float32)
    cnts = np.zeros((2, tm, len(POOL_WINDOWS) * POOL_GROUP_DIM), np.float32)
    for kind, length in enumerate((GRID_W, ctx_len)):
        for gi, w in enumerate(POOL_WINDOWS):
            for t in range(tm):
                base = (t // length) * length
                tl = t - base
                lo = min(max(tl - w // 2, 0), length)
                hi = min(max(tl + w // 2, 0), length)
                masks[kind, gi, t, base + lo:base + hi] = 1.0
                cnts[kind, t, gi * POOL_GROUP_DIM:(gi + 1) * POOL_GROUP_DIM] = hi - lo
    return jnp.asarray(masks, BF16), jnp.asarray(cnts, F32)


def _mix_kernel(x_ref, zp_ref, zuv_ref, zmg_ref, zo_ref, hf_ref, hb_ref, mod_ref,
                pm_ref, cnt_ref, pw_ref, ps_ref, sg_ref, sb_ref, sw_ref, sbm_ref, ng_ref,
                wbp_ref, wbm_ref, wbs_ref, wo_ref, l1g_ref, l1b_ref, wr_ref, br_ref,
                x1_ref, h2_ref, route_ref, cnt_out_ref, cnt_sc, *, alpha):
    tm, d = x_ref.shape
    i = pl.program_id(0)

    @pl.when(i == 0)
    def _():
        cnt_sc[...] = jnp.zeros_like(cnt_sc)

    lane256 = lax.broadcasted_iota(jnp.int32, (1, 4 * POOL_GROUP_DIM), 1) // POOL_GROUP_DIM

    xp = zp_ref[...]
    xp_f = xp.astype(F32)
    pooled = jnp.zeros(xp_f.shape, F32)
    for gi in range(len(POOL_WINDOWS)):
        s = jnp.dot(pm_ref[gi], xp, preferred_element_type=F32)
        pooled = jnp.where(lane256 == gi, s, pooled)
    diff = pooled / cnt_ref[...] - xp_f
    pool_o = jnp.dot(diff.astype(BF16), pw_ref[...], preferred_element_type=F32) * ps_ref[...]

    uv = zuv_ref[...].astype(F32)
    uv = 0.5 * uv * (1.0 + lax.erf(uv * (2.0 ** -0.5)))
    half = uv.shape[1] // 2
    u = uv[:, :half]
    vn = (_ln_plain(uv[:, half:]) * sg_ref[...] + sb_ref[...]).astype(BF16)
    mixed_chunks = []
    for c0 in range(0, tm, CHUNK):
        vc = vn[c0:c0 + CHUNK, :]
        mixed = jnp.zeros((CHUNK, half), F32)
        for gi in range(sw_ref.shape[0]):
            s = jnp.dot(sw_ref[gi], vc, preferred_element_type=F32)
            mixed = jnp.where(lane256 == gi, s, mixed)
        mixed_chunks.append(mixed + sbm_ref[...])
    sgu_o = u * jnp.concatenate(mixed_chunks, axis=0)

    hsum = hf_ref[...].astype(F32) + hb_ref[...].astype(F32)
    heads = []
    for hh in range(N_HEADS):
        hv = hsum[:, hh * HEAD_DIM:(hh + 1) * HEAD_DIM]
        mu = jnp.mean(hv, axis=-1, keepdims=True)
        hc = hv - mu
        var = jnp.mean(hc * hc, axis=-1, keepdims=True)
        heads.append(hc * lax.rsqrt(var + HEAD_NORM_EPS))
    hn = jnp.concatenate(heads, axis=1)
    twice_gate = lambda zh: jnp.tanh(zh.astype(F32)) + 1.0
    mlstm_o = hn * ng_ref[...] * twice_gate(zo_ref[...])

    y = twice_gate(zmg_ref[:, 0:d]) * jnp.dot(pool_o.astype(BF16), wbp_ref[...], preferred_element_type=F32)
    y += twice_gate(zmg_ref[:, d:2 * d]) * jnp.dot(mlstm_o.astype(BF16), wbm_ref[...],
                                                   preferred_element_type=F32)
    y += twice_gate(zmg_ref[:, 2 * d:3 * d]) * jnp.dot(sgu_o.astype(BF16), wbs_ref[...],
                                                       preferred_element_type=F32)
    y2 = jnp.dot(y.astype(BF16), wo_ref[...], preferred_element_type=F32)

    g1 = mod_ref[:, 2 * d:3 * d]
    x1 = _ln_plain(alpha * x_ref[...] + g1 * y2) * l1g_ref[...] + l1b_ref[...]
    x1_ref[...] = x1
    h2 = _ln_plain(x1) * (1.0 + mod_ref[:, 4 * d:5 * d]) + mod_ref[:, 3 * d:4 * d]
    h2b = h2.astype(BF16)
    h2_ref[...] = _pack_rows(h2b)

    logits = lax.dot_general(wr_ref[...], h2b, (((1,), (1,)), ((), ())),
                             preferred_element_type=F32) + br_ref[...]
    n_exp = logits.shape[0]
    expert = lax.broadcasted_iota(jnp.int32, logits.shape, 0)
    work = logits
    vals, idxs, hots = [], [], []
    for _ in range(TOP_K):
        mval = jnp.max(work, axis=0, keepdims=True)
        idx = jnp.min(jnp.where(work == mval, expert, n_exp), axis=0, keepdims=True)
        hot = expert == idx
        vals.append(mval)
        idxs.append(idx)
        hots.append(hot)
        work = jnp.where(hot, -jnp.inf, work)
    exps = [jnp.exp(v - vals[0]) for v in vals]
    denom = exps[0] + exps[1] + exps[2] + exps[3]
    sel = (hots[0] | hots[1] | hots[2] | hots[3]).astype(F32)
    r_i = lax.broadcasted_iota(jnp.int32, (tm, tm), 0)
    c_i = lax.broadcasted_iota(jnp.int32, (tm, tm), 1)
    earlier = (r_i < c_i).astype(BF16)
    rank = jnp.dot(sel.astype(BF16), earlier, preferred_element_type=F32) + cnt_sc[:, 0:1]
    rows = ([i_.astype(F32) for i_ in idxs] + [e / denom for e in exps]
            + [jnp.sum(jnp.where(h, rank, 0.0), axis=0, keepdims=True) for h in hots]
            + [jnp.zeros((route_ref.shape[0] - 3 * TOP_K, tm), F32)])
    route_ref[...] = jnp.concatenate(rows, axis=0)
    new_cnt = cnt_sc[:, 0:1] + jnp.sum(sel, axis=1, keepdims=True)
    cnt_sc[...] = jnp.broadcast_to(new_cnt, cnt_sc.shape)
    cnt_out_ref[...] = jnp.broadcast_to(new_cnt, cnt_out_ref.shape)


def _mix(x, zp, zuv, zmg, zo, hf, hb, mods, consts, *, n_rows, n_lat, seq, alpha):
    t, d = x.shape
    tm = ROW_TILE
    n_batch = mods.shape[0] - 1
    row = lambda i: (i, 0)
    const2 = lambda i: (0, 0)
    const3 = lambda i: (0, 0, 0)
    kind = lambda i: jnp.where(i * tm >= n_lat, 1, 0)

    def mod_idx(i):
        return (jnp.minimum((i * tm) // seq, n_batch), 0, 0)

    (pm, cnt, pw, ps, sg, sb, sw, sbm, ng, wbp, wbm, wbs, wo, l1g, l1b, wr, br) = consts
    in_specs = [
        pl.BlockSpec((tm, d), row),
        pl.BlockSpec((tm, zp.shape[1]), row),
        pl.BlockSpec((tm, zuv.shape[1]), row),
        pl.BlockSpec((tm, zmg.shape[1]), row),
        pl.BlockSpec((tm, zo.shape[1]), row),
        pl.BlockSpec((tm, hf.shape[1]), row),
        pl.BlockSpec((tm, hb.shape[1]), row),
        pl.BlockSpec((None, 1, mods.shape[2]), mod_idx),
        pl.BlockSpec((None,) + pm.shape[1:], lambda i: (kind(i), 0, 0, 0)),
        pl.BlockSpec((None,) + cnt.shape[1:], lambda i: (kind(i), 0, 0)),
        pl.BlockSpec(pw.shape, const2),
        pl.BlockSpec(ps.shape, const2),
        pl.BlockSpec(sg.shape, const2),
        pl.BlockSpec(sb.shape, const2),
        pl.BlockSpec(sw.shape, const3),
        pl.BlockSpec(sbm.shape, const2),
        pl.BlockSpec(ng.shape, const2),
        pl.BlockSpec(wbp.shape, const2),
        pl.BlockSpec(wbm.shape, const2),
        pl.BlockSpec(wbs.shape, const2),
        pl.BlockSpec(wo.shape, const2),
        pl.BlockSpec(l1g.shape, const2),
        pl.BlockSpec(l1b.shape, const2),
        pl.BlockSpec(wr.shape, const2),
        pl.BlockSpec(br.shape, const2),
    ]
    return pl.pallas_call(
        functools.partial(_mix_kernel, alpha=alpha),
        grid=(n_rows // tm,),
        in_specs=in_specs,
        out_specs=[
            pl.BlockSpec((tm, d), row),
            pl.BlockSpec((tm, d // 2), row),
            pl.BlockSpec((ROUTE_ROWS, tm), lambda i: (0, i)),
            pl.BlockSpec((wr.shape[0], LANES), const2),
        ],
        out_shape=[
            jax.ShapeDtypeStruct((n_rows, d), F32),
            jax.ShapeDtypeStruct((n_rows, d // 2), jnp.uint32),
            jax.ShapeDtypeStruct((ROUTE_ROWS, n_rows), F32),
            jax.ShapeDtypeStruct((wr.shape[0], LANES), F32),
        ],
        scratch_shapes=[pltpu.VMEM((wr.shape[0], LANES), F32)],
        compiler_params=_cparams("arbitrary"),
    )(x, zp, zuv, zmg, zo, hf, hb, mods, pm, cnt, pw, ps, sg, sb, sw, sbm, ng,
      wbp, wbm, wbs, wo, l1g, l1b, wr, br)


SCATTER_WINDOW = 128
SCATTER_PARTS = 2


def _scatter_rows(x, idx, n_out, part):
    n = x.shape[0]
    w = x.shape[1] // SCATTER_PARTS
    n_idx = idx.shape[0]
    mesh = plsc.VectorSubcoreMesh(core_axis_name="core", subcore_axis_name="subcore")

    @pl.kernel(out_type=jax.ShapeDtypeStruct((n_out, w), x.dtype), mesh=mesh, scratch_types=[])
    def scatter_kernel(x_hbm, i_hbm, o_hbm):
        def body(x_vmem, i_vmem):
            for k in range(n_idx):
                pltpu.sync_copy(x_vmem, o_hbm.at[i_vmem.at[k]])

        pltpu.emit_pipeline(
            body,
            grid=(n // SCATTER_WINDOW,),
            in_specs=[pl.BlockSpec((SCATTER_WINDOW, w), lambda i: (i, part)),
                      pl.BlockSpec((n_idx, SCATTER_WINDOW), lambda i: (0, i))],
            out_specs=[],
            core_axis_name=("core", "subcore"),
            dimension_semantics=(pltpu.PARALLEL,),
        )(x_hbm, i_hbm)

    return scatter_kernel(x, idx)


def _gather_rows(x, idx):
    w = x.shape[1]
    n_idx = idx.shape[1]
    mesh = plsc.VectorSubcoreMesh(core_axis_name="core", subcore_axis_name="subcore")

    @pl.kernel(out_type=jax.ShapeDtypeStruct((n_idx, w), x.dtype), mesh=mesh, scratch_types=[])
    def gather_kernel(x_hbm, i_hbm, o_hbm):
        def body(i_vmem, o_vmem):
            pltpu.sync_copy(x_hbm.at[i_vmem.at[0]], o_vmem)

        pltpu.emit_pipeline(
            body,
            grid=(n_idx // SCATTER_WINDOW,),
            in_specs=[pl.BlockSpec((1, SCATTER_WINDOW), lambda i: (0, i))],
            out_specs=[pl.BlockSpec((SCATTER_WINDOW, w), lambda i: (i, 0))],
            core_axis_name=("core", "subcore"),
            dimension_semantics=(pltpu.PARALLEL,),
        )(i_hbm, o_hbm)

    return gather_kernel(x, idx)


def _expert_kernel(te_ref, nt_ref, xa_ref, xb_ref, wgu_ref, bgu_ref, wd_ref, bd_ref, ya_ref, yb_ref,
                   wgu_sc, wd_sc):
    i = pl.program_id(0)
    f = wd_ref.shape[0]
    used = i < nt_ref[0]

    @pl.when(jnp.logical_and(used, jnp.logical_or(i == 0, te_ref[i] != te_ref[jnp.maximum(i - 1, 0)])))
    def _():
        for c0 in range(0, 2 * f, 512):
            wgu_sc[:, c0:c0 + 512] = wgu_ref[:, c0:c0 + 512].astype(BF16)
        for c0 in range(0, f, 256):
            wd_sc[c0:c0 + 256, :] = wd_ref[c0:c0 + 256, :].astype(BF16)

    @pl.when(used)
    def _():
        x = _unpack_rows([xa_ref[...], xb_ref[...]]).astype(BF16)
        acc = jnp.zeros((x.shape[0], wd_ref.shape[1]), F32)
        for c0 in range(0, f, 512):
            gate = jnp.dot(x, wgu_sc[:, c0:c0 + 512], preferred_element_type=F32) + bgu_ref[:, c0:c0 + 512]
            up = (jnp.dot(x, wgu_sc[:, f + c0:f + c0 + 512], preferred_element_type=F32)
                  + bgu_ref[:, f + c0:f + c0 + 512])
            gate = jnp.minimum(gate, SWIGLU_LIMIT)
            up = jnp.clip(up, -SWIGLU_LIMIT, SWIGLU_LIMIT)
            glu = gate * _sigmoid(SWIGLU_ALPHA * gate)
            act = ((up + 1.0) * glu).astype(BF16)
            acc += jnp.dot(act, wd_sc[c0:c0 + 512, :], preferred_element_type=F32)
        words = _pack_rows(acc + bd_ref[...])
        ya_ref[...] = words[:, :ya_ref.shape[1]]
        yb_ref[...] = words[:, ya_ref.shape[1]:]

    @pl.when(jnp.logical_not(used))
    def _():
        ya_ref[...] = jnp.zeros_like(ya_ref)
        yb_ref[...] = jnp.zeros_like(yb_ref)


def _experts(tile_expert, n_tiles_used, xa, xb, wgu, bgu, wd, bd, *, layer):
    p = xa.shape[0]
    d = 2 * SCATTER_PARTS * xa.shape[1]
    tm = EXPERT_TILE
    f2 = wgu.shape[3]
    grid_spec = pltpu.PrefetchScalarGridSpec(
        num_scalar_prefetch=2,
        grid=(p // tm,),
        in_specs=[
            pl.BlockSpec((tm, xa.shape[1]), lambda i, te, nt: (i, 0)),
            pl.BlockSpec((tm, xb.shape[1]), lambda i, te, nt: (i, 0)),
            pl.BlockSpec((None, None, d, f2), lambda i, te, nt: (layer, te[i], 0, 0)),
            pl.BlockSpec((None, None, 1, f2), lambda i, te, nt: (layer, te[i], 0, 0)),
            pl.BlockSpec((None, None, f2 // 2, d), lambda i, te, nt: (layer, te[i], 0, 0)),
            pl.BlockSpec((None, None, 1, d), lambda i, te, nt: (layer, te[i], 0, 0)),
        ],
        out_specs=[pl.BlockSpec((tm, xa.shape[1]), lambda i, te, nt: (i, 0)),
                   pl.BlockSpec((tm, xb.shape[1]), lambda i, te, nt: (i, 0))],
        scratch_shapes=[pltpu.VMEM((d, f2), BF16), pltpu.VMEM((f2 // 2, d), BF16)],
    )
    return pl.pallas_call(
        _expert_kernel,
        grid_spec=grid_spec,
        out_shape=[jax.ShapeDtypeStruct(xa.shape, xa.dtype), jax.ShapeDtypeStruct(xb.shape, xb.dtype)],
        compiler_params=_cparams("arbitrary"),
    )(tile_expert, n_tiles_used, xa, xb, wgu, bgu, wd, bd)


def _combine_kernel(x1_ref, ya_ref, yb_ref, route_ref, mod_ref, g_ref, b_ref, *rest, alpha):
    x2_ref = rest[-1]
    d = x1_ref.shape[1]
    tm = x1_ref.shape[0]
    route = jnp.concatenate([route_ref[...], jnp.zeros((LANES - ROUTE_ROWS, tm), F32)], axis=0).T
    f = jnp.zeros(x1_ref.shape, F32)
    for kk in range(TOP_K):
        f += route[:, TOP_K + kk:TOP_K + kk + 1] * _unpack_rows([ya_ref[kk], yb_ref[kk]])
    g2 = mod_ref[:, 5 * d:6 * d]
    x2_ref[...] = _ln_plain(alpha * x1_ref[...] + g2 * f) * g_ref[...] + b_ref[...]


def _combine(x1, yg_a, yg_b, route, mods, g, b, *, seq, alpha, dest=None, dest_rows=None, tile_offset=0):
    t, d = x1.shape
    tm = COMBINE_TILE
    n_batch = mods.shape[0] - 1
    row = lambda i: (i, 0)

    def mod_idx(i):
        return (jnp.minimum((i * tm) // seq, n_batch), 0, 0)

    in_specs = [
        pl.BlockSpec((tm, d), row),
        pl.BlockSpec((TOP_K, tm, yg_a.shape[2]), lambda i: (0, i, 0)),
        pl.BlockSpec((TOP_K, tm, yg_b.shape[2]), lambda i: (0, i, 0)),
        pl.BlockSpec((ROUTE_ROWS, tm), lambda i: (0, i)),
        pl.BlockSpec((None, 1, mods.shape[2]), mod_idx),
        pl.BlockSpec(g.shape, lambda i: (0, 0)),
        pl.BlockSpec(b.shape, lambda i: (0, 0)),
    ]
    args = [x1, yg_a, yg_b, route, mods, g, b]
    aliases = {}
    if dest is not None:
        in_specs.append(pl.BlockSpec(memory_space=pl.ANY))
        args.append(dest)
        aliases = {len(args) - 1: 0}
    return pl.pallas_call(
        functools.partial(_combine_kernel, alpha=alpha),
        grid=(t // tm,),
        in_specs=in_specs,
        out_specs=pl.BlockSpec((tm, d), lambda i: (i + tile_offset, 0)),
        out_shape=jax.ShapeDtypeStruct((t if dest_rows is None else dest_rows, d), F32),
        input_output_aliases=aliases,
        compiler_params=_cparams("parallel"),
    )(*args)


def kernel(x, c, ctx, c_ctx, w_ada, b_ada, w_in, b_in, pool_w, pool_scale, qk_conv_w, qk_conv_b,
           mlstm_norm_g, sgu_ln_g, sgu_ln_b, sgu_w, sgu_b, w_br_pool, w_br_mlstm, w_br_sgu, w_out,
           ln1_g, ln1_b, w_router, b_router, w_gate_up, b_gate_up, w_down, b_down, ln2_g, ln2_b):
    n_batch, seq, d = x.shape
    ctx_len = ctx.shape[1]
    depth = w_in.shape[0]
    n_exp = w_router.shape[2]
    assert ctx_len == ROW_TILE and seq % ROW_TILE == 0 and ROW_TILE % GRID_W == 0
    assert n_batch % N_STREAMS == 0
    alpha = float((2 * depth) ** 0.25)

    rows = ((n_batch + 1 + 7) // 8) * 8
    cc = jnp.zeros((rows, d), F32).at[:n_batch].set(c).at[n_batch].set(c_ctx)
    mods_all = _ada_table(cc, w_ada, b_ada)[:, :n_batch + 1].reshape(depth, n_batch + 1, 1, 6 * d)

    pool_w_, mlw = 256, 512
    col = lambda a, lo, hi: a[..., lo:hi]
    g_lo = pool_w_ + 4 * mlw
    parts = ((pool_w_, pool_w_ + 2 * mlw), (pool_w_ + 2 * mlw, pool_w_ + 3 * mlw), (pool_w_ + 3 * mlw, g_lo),
             (g_lo + _N_GATE_COLS, g_lo + _N_GATE_COLS + 512), (0, pool_w_),
             (g_lo + _N_GATE_COLS + 512, w_in.shape[2]))
    part_scale = (1.0, 1.0, 0.5, 1.0, 1.0, 0.5)
    w_main = jnp.concatenate([col(w_in, lo, hi) * sc for (lo, hi), sc in zip(parts, part_scale)],
                             axis=-1).astype(BF16)
    b_main = jnp.concatenate([col(b_in, lo, hi) * sc for (lo, hi), sc in zip(parts, part_scale)],
                             axis=-1)[:, None, :]
    gw, gb = col(w_in, g_lo, g_lo + _N_GATE_COLS), col(b_in, g_lo, g_lo + _N_GATE_COLS)
    nh = N_HEADS
    pick = lambda a, k: jnp.concatenate([a[..., k * nh:(k + 1) * nh], a[..., (k + 2) * nh:(k + 3) * nh]], axis=-1)
    w_gate_t = jnp.swapaxes(jnp.concatenate([pick(gw, 0), pick(gw, 1)], axis=-1), 1, 2).astype(BF16)
    b_gate_t = jnp.concatenate([pick(gb, 0), pick(gb, 1)], axis=-1)[:, :, None]

    pm, cnt = _pool_constants(ctx_len)
    eye = jnp.eye(len(POOL_WINDOWS), dtype=F32)
    pool_bd = jnp.einsum('lgcd,gh->lgchd', pool_w, eye).reshape(depth, 256, 256).astype(BF16)
    sgu_bias_map = jnp.repeat(jnp.swapaxes(sgu_b, 1, 2), sgu_ln_g.shape[1] // sgu_b.shape[1], axis=2)
    w_router_t = jnp.swapaxes(w_router, 1, 2).astype(BF16)
    b_router_t = b_router[:, :, None]
    r1 = lambda a: a[:, None, :]
    b_gu4 = b_gate_up[:, :, None, :]
    b_d4 = b_down[:, :, None, :]

    nb = n_batch // N_STREAMS
    n_lat = nb * seq
    t_all = n_lat + nb * ctx_len
    streams, stream_mods = [], []
    for h in range(N_STREAMS):
        sl = slice(h * nb, (h + 1) * nb)
        streams.append(jnp.concatenate([x[sl].reshape(n_lat, d), ctx[sl].reshape(nb * ctx_len, d)], axis=0))
        stream_mods.append(jnp.concatenate([mods_all[:, sl], mods_all[:, n_batch:n_batch + 1]], axis=1))
    out = None
    for i in range(depth):
        last = i == depth - 1
        for h in range(N_STREAMS):
            stream = streams[h]
            mods = stream_mods[h][i]
            zqk, zv, zo, zuv, zp, zmg, g, gt = _inproj(
                stream, mods, w_main[i], b_main[i], w_gate_t[i], b_gate_t[i],
                qk_conv_w[i], r1(qk_conv_b)[i], n_lat=n_lat, seq=seq, ctx_len=ctx_len)
            hf, hb = _mlstm(zqk, zv, g, gt, n_batch=nb, seq=seq, ctx_len=ctx_len)
            n_rows = n_lat if last else t_all
            consts = (pm, cnt, pool_bd[i], r1(pool_scale)[i], r1(sgu_ln_g)[i], r1(sgu_ln_b)[i],
                      sgu_w[i].astype(BF16), sgu_bias_map[i], r1(0.5 * mlstm_norm_g)[i],
                      w_br_pool[i].astype(BF16), w_br_mlstm[i].astype(BF16), w_br_sgu[i].astype(BF16),
                      (0.5 * w_out[i]).astype(BF16), r1(ln1_g)[i], r1(ln1_b)[i], w_router_t[i], b_router_t[i])
            x1, h2, route, counts = _mix(stream, zp, zuv, zmg, zo, hf, hb, mods, consts,
                                         n_rows=n_rows, n_lat=n_lat, seq=seq, alpha=alpha)

            cnt_e = counts[:, 0].astype(jnp.int32)
            padded = ((cnt_e + EXPERT_TILE - 1) // EXPERT_TILE) * EXPERT_TILE
            ends = jnp.cumsum(padded)
            starts = ends - padded
            e_idx = route[0:TOP_K].astype(jnp.int32)
            rank = route[2 * TOP_K:3 * TOP_K].astype(jnp.int32)
            start_of = jnp.sum(jnp.where(e_idx[..., None] == jnp.arange(n_exp, dtype=jnp.int32), starts, 0),
                               axis=-1)
            pos_t = start_of + rank
            p_rows = n_rows * TOP_K + n_exp * EXPERT_TILE
            n_tiles = p_rows // EXPERT_TILE
            tile_start = jnp.arange(n_tiles, dtype=jnp.int32) * EXPERT_TILE
            tile_expert = jnp.minimum(
                jnp.sum((ends[None, :] <= tile_start[:, None]).astype(jnp.int32), axis=1), n_exp - 1)
            n_used = (ends[-1] // EXPERT_TILE).astype(jnp.int32).reshape(1)
            xs_a = _scatter_rows(h2, pos_t, p_rows, 0)
            xs_b = _scatter_rows(h2, pos_t, p_rows, 1)
            ys_a, ys_b = _experts(tile_expert, n_used, xs_a, xs_b, w_gate_up, b_gu4, w_down, b_d4, layer=i)
            pos_flat = pos_t.reshape(1, TOP_K * n_rows)
            yg_a = _gather_rows(ys_a, pos_flat).reshape(TOP_K, n_rows, -1)
            yg_b = _gather_rows(ys_b, pos_flat).reshape(TOP_K, n_rows, -1)
            if last:
                out = _combine(x1, yg_a, yg_b, route, mods, r1(ln2_g)[i], r1(ln2_b)[i], seq=seq, alpha=alpha,
                               dest=out, dest_rows=n_batch * seq, tile_offset=h * n_lat // COMBINE_TILE)
            else:
                streams[h] = _combine(x1, yg_a, yg_b, route, mods, r1(ln2_g)[i], r1(ln2_b)[i],
                                      seq=seq, alpha=alpha)
    return out.reshape(n_batch, seq, d)
```

```python
import functools

import numpy as np
import jax
import jax.numpy as jnp
from jax import lax
from jax.experimental import pallas as pl
from jax.experimental.pallas import tpu as pltpu
from jax.experimental.pallas import tpu_sc as plsc

F32 = jnp.float32
BF16 = jnp.bfloat16

GRID_W = 64
POOL_WINDOWS = (2, 4, 8, 16)
POOL_GROUP_DIM = 64
N_HEADS = 4
HEAD_DIM = 128
CHUNK = 128
TOP_K = 4
SWIGLU_LIMIT = 7.0
SWIGLU_ALPHA = 1.702
LN_EPS = 1e-5
HEAD_NORM_EPS = 1e-6

ROW_TILE = 256
INPROJ_TILE = 512
COMBINE_TILE = 1024
SEAM_ROWS = 16
ROUTE_ROWS = 16
EXPERT_TILE = 512
N_STREAMS = 2
SUBLANES = 8
LANES = 128
VMEM_LIMIT = 56 * 1024 * 1024


def _cparams(*sem):
    return pltpu.CompilerParams(dimension_semantics=sem, vmem_limit_bytes=VMEM_LIMIT)


def _ln_plain(x):
    mu = jnp.mean(x, axis=-1, keepdims=True)
    xc = x - mu
    var = jnp.mean(xc * xc, axis=-1, keepdims=True)
    return xc * lax.rsqrt(var + LN_EPS)


def _sigmoid(x):
    return 0.5 * jnp.tanh(0.5 * x) + 0.5


def _log_sigmoid(x):
    return jnp.minimum(x, 0.0) - jnp.log1p(jnp.exp(-jnp.abs(x)))


def _pack_rows(x):
    half = x.shape[1] // 2
    bits = lax.bitcast_convert_type(x.astype(BF16).astype(F32), jnp.uint32)
    return (bits[:, half:] & jnp.uint32(0xFFFF0000)) | (bits[:, :half] >> 16)


def _unpack_rows(parts):
    low = [lax.bitcast_convert_type(w << 16, F32) for w in parts]
    high = [lax.bitcast_convert_type(w & jnp.uint32(0xFFFF0000), F32) for w in parts]
    return jnp.concatenate(low + high, axis=1)


def _ada_kernel(c_ref, w_ref, b_ref, o_ref):
    c = c_ref[...]
    s = c * _sigmoid(c)
    o_ref[...] = jnp.dot(s, w_ref[...], preferred_element_type=F32,
                         precision=lax.Precision.HIGHEST) + b_ref[...]


def _ada_table(cc, w_ada, b_ada):
    depth, d, n = w_ada.shape
    rows = cc.shape[0]
    bn = 1536
    return pl.pallas_call(
        _ada_kernel,
        grid=(depth, n // bn),
        in_specs=[
            pl.BlockSpec((rows, d), lambda l, j: (0, 0)),
            pl.BlockSpec((None, d, bn), lambda l, j: (l, 0, j)),
            pl.BlockSpec((None, 1, bn), lambda l, j: (l, 0, j)),
        ],
        out_specs=pl.BlockSpec((None, rows, bn), lambda l, j: (l, 0, j)),
        out_shape=jax.ShapeDtypeStruct((depth, rows, n), F32),
        compiler_params=_cparams("parallel", "parallel"),
    )(cc, w_ada, b_ada.reshape(depth, 1, n))


_QK_WIDTH = 2 * N_HEADS * HEAD_DIM
_Z_PARTS = (("v", 512), ("o", 512), ("uv", 512), ("pool", 256), ("mg", 3072))
_N_GATE_COLS = 4 * N_HEADS


def _inproj_kernel(x_ref, xp_ref, xn_ref, mod_ref, w_ref, b_ref, wgt_ref, bgt_ref,
                   cw_ref, cb_ref, qk_ref, v_ref, o_ref, uv_ref, p_ref, mg_ref, g_ref, gt_ref, pad_sc,
                   *, n_lat, seq, ctx_len):
    tm, d = x_ref.shape
    i = pl.program_id(0)
    is_lat = i * tm < n_lat
    has_prev = jnp.logical_and(is_lat, (i * tm) % seq != 0).astype(F32)
    has_next = jnp.logical_and(is_lat, ((i + 1) * tm) % seq != 0).astype(F32)

    xe = jnp.concatenate([xp_ref[...], x_ref[...], xn_ref[...]], axis=0)
    he = (_ln_plain(xe) * (1.0 + mod_ref[:, d:2 * d]) + mod_ref[:, 0:d]).astype(BF16)
    hb = he[SUBLANES:SUBLANES + tm, :]

    half = _QK_WIDTH // 2

    def qk_project(part):
        c0 = part * half
        z = jnp.dot(he, w_ref[:, c0:c0 + half], preferred_element_type=F32) + b_ref[:, c0:c0 + half]
        pad_sc[part] = z
        pad_sc[part, SUBLANES - 1:SUBLANES, :] = pad_sc[part, SUBLANES - 1:SUBLANES, :] * has_prev
        pad_sc[part, SUBLANES + tm:SUBLANES + tm + 1, :] = (
            pad_sc[part, SUBLANES + tm:SUBLANES + tm + 1, :] * has_next)

    def conv_act(part, prev, cur, nxt):
        c0 = part * half
        y = (cb_ref[:, c0:c0 + half] + prev * cw_ref[0:1, c0:c0 + half]
             + cur * cw_ref[1:2, c0:c0 + half] + nxt * cw_ref[2:3, c0:c0 + half])
        act = y * _sigmoid(y)
        if part == 1:
            act = act * (HEAD_DIM ** -0.5)
        return act.astype(qk_ref.dtype)

    def qk_conv(part):
        qk_ref[:, part * half:(part + 1) * half] = conv_act(
            part, pad_sc[part, SUBLANES - 1:SUBLANES - 1 + tm, :], pad_sc[part, SUBLANES:SUBLANES + tm, :],
            pad_sc[part, SUBLANES + 1:SUBLANES + 1 + tm, :])

    def plain(ref, off, c0, cw):
        z = jnp.dot(hb, w_ref[:, off + c0:off + c0 + cw], preferred_element_type=F32)
        ref[:, c0:c0 + cw] = (z + b_ref[:, off + c0:off + c0 + cw]).astype(ref.dtype)

    mxu_tasks = []
    off = _QK_WIDTH
    for ref, (_, width) in zip((v_ref, o_ref, uv_ref, p_ref, mg_ref), _Z_PARTS):
        for c0 in range(0, width, 512):
            mxu_tasks.append(functools.partial(plain, ref, off, c0, min(512, width - c0)))
        off += width
    vpu_tasks = {2: functools.partial(_gate_prep, hb, wgt_ref, bgt_ref, g_ref, gt_ref),
                 4: functools.partial(qk_conv, 0), 6: functools.partial(qk_conv, 1)}
    qk_project(0)
    qk_project(1)
    for n, task in enumerate(mxu_tasks):
        if n in vpu_tasks:
            vpu_tasks[n]()
        task()

    if tm > ctx_len:
        @pl.when(jnp.logical_not(is_lat))
        def _():
            r = lax.broadcasted_iota(jnp.int32, (2 * SEAM_ROWS, 1), 0)
            for part in range(2):
                for p in range(ctx_len, tm, ctx_len):
                    lo = SUBLANES + p - SEAM_ROWS
                    prev = jnp.where(r == SEAM_ROWS, 0.0, pad_sc[part, lo - 1:lo - 1 + 2 * SEAM_ROWS, :])
                    nxt = jnp.where(r == SEAM_ROWS - 1, 0.0, pad_sc[part, lo + 1:lo + 1 + 2 * SEAM_ROWS, :])
                    qk_ref[p - SEAM_ROWS:p + SEAM_ROWS, part * half:(part + 1) * half] = conv_act(
                        part, prev, pad_sc[part, lo:lo + 2 * SEAM_ROWS, :], nxt)


def _gate_prep(hb, wgt_ref, bgt_ref, g_ref, gt_ref):
    tm = hb.shape[0]
    nd = 2 * N_HEADS
    ggt = lax.dot_general(wgt_ref[...], hb, (((1,), (1,)), ((), ())),
                          preferred_element_type=F32) + bgt_ref[...]
    git, gft = ggt[:nd, :], ggt[nd:, :]
    fwd = lax.broadcasted_iota(jnp.int32, (nd, tm), 0) < N_HEADS
    pos = lax.broadcasted_iota(jnp.int32, (nd, tm), 1) % CHUNK
    lsg = _log_sigmoid(gft)
    bcum = jnp.where(fwd, _chunk_scan(lsg, pos, 1, False, False), _chunk_scan(lsg, pos, 1, True, False))
    c = git - bcum
    cmax = jnp.where(fwd, _chunk_scan(c, pos, 1, False, True), _chunk_scan(c, pos, 1, True, True))
    gt_ref[...] = c
    rows = jnp.concatenate([bcum, c, bcum + cmax, cmax, jnp.zeros((LANES - 4 * nd, tm), F32)], axis=0)
    g_ref[...] = rows.T


def _chunk_scan(a, pos, axis, reverse, use_max):
    n = a.shape[axis]
    ident = -jnp.inf if use_max else 0.0
    sh = 1
    while sh < CHUNK:
        if reverse:
            moved = jnp.where(pos < CHUNK - sh, pltpu.roll(a, n - sh, axis=axis), ident)
        else:
            moved = jnp.where(pos >= sh, pltpu.roll(a, sh, axis=axis), ident)
        a = jnp.maximum(a, moved) if use_max else a + moved
        sh *= 2
    return a


def _inproj(x, mods, w, b, wgt, bgt, conv_w, conv_b, *, n_lat, seq, ctx_len):
    t, d = x.shape
    tm = INPROJ_TILE
    assert seq % tm == 0 and n_lat % tm == 0 and (t - n_lat) % tm == 0 and tm % ctx_len == 0
    n_batch = mods.shape[0] - 1
    per_tile = tm // SUBLANES

    def mod_idx(i):
        return (jnp.minimum((i * tm) // seq, n_batch), 0, 0)

    const = lambda i: (0, 0)
    row = lambda i: (i, 0)
    parts = (("qk", _QK_WIDTH),) + _Z_PARTS
    out_shapes = [jax.ShapeDtypeStruct((t, width), BF16) for _, width in parts]
    out_specs = [pl.BlockSpec((tm, width), row) for _, width in parts]
    out_shapes += [jax.ShapeDtypeStruct((t, LANES), F32), jax.ShapeDtypeStruct((2 * N_HEADS, t), F32)]
    out_specs += [pl.BlockSpec((tm, LANES), row), pl.BlockSpec((2 * N_HEADS, tm), lambda i: (0, i))]
    return pl.pallas_call(
        functools.partial(_inproj_kernel, n_lat=n_lat, seq=seq, ctx_len=ctx_len),
        grid=(t // tm,),
        in_specs=[
            pl.BlockSpec((tm, d), row),
            pl.BlockSpec((SUBLANES, d), lambda i: (jnp.maximum(i * per_tile - 1, 0), 0)),
            pl.BlockSpec((SUBLANES, d), lambda i: (jnp.minimum((i + 1) * per_tile, t // SUBLANES - 1), 0)),
            pl.BlockSpec((None, 1, mods.shape[2]), mod_idx),
            pl.BlockSpec(w.shape, const),
            pl.BlockSpec(b.shape, const),
            pl.BlockSpec(wgt.shape, const),
            pl.BlockSpec(bgt.shape, const),
            pl.BlockSpec(conv_w.shape, const),
            pl.BlockSpec(conv_b.shape, const),
        ],
        out_specs=out_specs,
        out_shape=out_shapes,
        scratch_shapes=[pltpu.VMEM((2, tm + 2 * SUBLANES, _QK_WIDTH // 2), F32)],
        compiler_params=_cparams("parallel"),
    )(x, x, x, mods, w, b, wgt, bgt, conv_w, conv_b)


def _mlstm_row_block(j, b, direction, *, n_batch, seq, ctx_len):
    ncc = ctx_len // CHUNK
    ncl = seq // CHUNK
    is_ctx = j < ncc
    if direction == 0:
        c = jnp.where(is_ctx, j, j - ncc)
    else:
        c = jnp.where(is_ctx, ncc - 1 - j, ncl - 1 - (j - ncc))
    base = jnp.where(is_ctx, (n_batch * seq + b * ctx_len) // CHUNK, (b * seq) // CHUNK)
    return base + c


def _mlstm_kernel(q_f, k_f, v_f, g_f, gt_f, q_b, k_b, v_b, g_b, gt_b, hf_ref, hb_ref, s_sc, m_sc):
    j = pl.program_id(1)

    @pl.when(j == 0)
    def _():
        s_sc[...] = jnp.zeros_like(s_sc)
        m_sc[...] = jnp.zeros_like(m_sc)

    ri = lax.broadcasted_iota(jnp.int32, (CHUNK, CHUNK), 0)
    ci = lax.broadcasted_iota(jnp.int32, (CHUNK, CHUNK), 1)
    ones_tile = jnp.ones((CHUNK, HEAD_DIM), BF16)
    dirs = ((q_f, k_f, v_f, g_f, gt_f, hf_ref), (q_b, k_b, v_b, g_b, gt_b, hb_ref))
    combos = [(d, hh) for d in range(2) for hh in range(N_HEADS)]
    heads = lambda ref, hh: ref[:, hh * HEAD_DIM:(hh + 1) * HEAD_DIM]

    nd = 2 * N_HEADS
    tri = ((ci <= ri), (ci >= ri))
    g = (g_f[...], g_b[...])
    gt = (gt_f[...], gt_b[...])

    qk_s, v1, state, m_prev = {}, {}, {}, {}
    for d, hh in combos:
        q_ref, k_ref, v_ref = dirs[d][:3]
        qk_s[d, hh] = lax.dot_general(heads(q_ref, hh), heads(k_ref, hh), (((1,), (1,)), ((), ())),
                                      preferred_element_type=F32)
        v1[d, hh] = jnp.concatenate([heads(v_ref, hh), ones_tile], axis=1)
        state[d, hh] = s_sc[d, hh]
        m_prev[d, hh] = m_sc[d * N_HEADS + hh][0:1, 0:1]

    q_state = {}
    for d, hh in combos:
        q_state[d, hh] = jnp.dot(heads(dirs[d][0], hh), state[d, hh].astype(BF16), preferred_element_type=F32)

    m_t, decay, scores = {}, {}, {}
    for d, hh in combos:
        col = d * N_HEADS + hh
        bcum = g[d][:, col:col + 1]
        inter = bcum + m_prev[d, hh]
        m_t[d, hh] = jnp.maximum(inter, g[d][:, 2 * nd + col:2 * nd + col + 1])
        decay[d, hh] = jnp.exp(inter - m_t[d, hh])
        dlog = jnp.where(tri[d], (bcum - m_t[d, hh]) + gt[d][col:col + 1, :], -jnp.inf)
        scores[d, hh] = (qk_s[d, hh] * jnp.exp(dlog)).astype(BF16)

    for d, hh in combos:
        numden = decay[d, hh] * q_state[d, hh] + jnp.dot(scores[d, hh], v1[d, hh], preferred_element_type=F32)
        num = numden[:, :HEAD_DIM]
        den = numden[:, HEAD_DIM:]
        out = num / jnp.maximum(jnp.abs(den), jnp.exp(-m_t[d, hh]))
        out_ref = dirs[d][5]
        out_ref[:, hh * HEAD_DIM:(hh + 1) * HEAD_DIM] = out.astype(out_ref.dtype)

    kw, a, m_new = {}, {}, {}
    for d, hh in combos:
        col = d * N_HEADS + hh
        last = CHUNK - 1 if d == 0 else 0
        total = g[d][last:last + 1, col:col + 1]
        m_new[d, hh] = total + jnp.maximum(m_prev[d, hh], g[d][last:last + 1, 3 * nd + col:3 * nd + col + 1])
        a[d, hh] = jnp.exp(total + m_prev[d, hh] - m_new[d, hh])
        wk = jnp.exp(total + g[d][:, nd + col:nd + col + 1] - m_new[d, hh])
        kw[d, hh] = (heads(dirs[d][1], hh).astype(F32) * wk).astype(BF16)

    for d, hh in combos:
        upd = lax.dot_general(kw[d, hh], v1[d, hh], (((0,), (0,)), ((), ())), preferred_element_type=F32)
        s_sc[d, hh] = a[d, hh] * state[d, hh] + upd
        m_sc[d * N_HEADS + hh] = jnp.broadcast_to(m_new[d, hh], m_sc.shape[1:])


def _mlstm(zqk, zv, g, gt, *, n_batch, seq, ctx_len):
    t = zqk.shape[0]
    nsteps = (seq + ctx_len) // CHUNK
    width = N_HEADS * HEAD_DIM
    idx = functools.partial(_mlstm_row_block, n_batch=n_batch, seq=seq, ctx_len=ctx_len)

    def specs(d):
        rb = lambda b, j: idx(j, b, d)
        return [
            pl.BlockSpec((CHUNK, width), lambda b, j: (rb(b, j), 0)),
            pl.BlockSpec((CHUNK, width), lambda b, j: (rb(b, j), 1)),
            pl.BlockSpec((CHUNK, width), lambda b, j: (rb(b, j), 0)),
            pl.BlockSpec((CHUNK, LANES), lambda b, j: (rb(b, j), 0)),
            pl.BlockSpec((2 * N_HEADS, CHUNK), lambda b, j: (0, rb(b, j))),
        ]

    out_spec = lambda d: pl.BlockSpec((CHUNK, width), lambda b, j: (idx(j, b, d), 0))
    return pl.pallas_call(
        _mlstm_kernel,
        grid=(n_batch, nsteps),
        in_specs=specs(0) + specs(1),
        out_specs=[out_spec(0), out_spec(1)],
        out_shape=[jax.ShapeDtypeStruct((t, width), BF16)] * 2,
        scratch_shapes=[
            pltpu.VMEM((2, N_HEADS, HEAD_DIM, 2 * HEAD_DIM), F32),
            pltpu.VMEM((2 * N_HEADS, 8, LANES), F32),
        ],
        compiler_params=_cparams("parallel", "arbitrary"),
    )(zqk, zqk, zv, g, gt, zqk, zqk, zv, g, gt)


def _pool_constants(ctx_len):
    tm = ROW_TILE
    masks = np.zeros((2, len(POOL_WINDOWS), tm, tm), np.float32)
    cnts = np.zeros((2, tm, len(POOL_WINDOWS) * POOL_GROUP_DIM), np.float32)
    for kind, length in enumerate((GRID_W, ctx_len)):
        for gi, w in enumerate(POOL_WINDOWS):
            for t in range(tm):
                base = (t // length) * length
                tl = t - base
                lo = min(max(tl - w // 2, 0), length)
                hi = min(max(tl + w // 2, 0), length)
                masks[kind, gi, t, base + lo:base + hi] = 1.0
                cnts[kind, t, gi * POOL_GROUP_DIM:(gi + 1) * POOL_GROUP_DIM] = hi - lo
    return jnp.asarray(masks, BF16), jnp.asarray(cnts, F32)


def _mix_kernel(x_ref, zp_ref, zuv_ref, zmg_ref, zo_ref, hf_ref, hb_ref, mod_ref,
                pm_ref, cnt_ref, pw_ref, ps_ref, sg_ref, sb_ref, sw_ref, sbm_ref, ng_ref,
                wbp_ref, wbm_ref, wbs_ref, wo_ref, l1g_ref, l1b_ref, wr_ref, br_ref,
                x1_ref, h2_ref, route_ref, cnt_out_ref, cnt_sc, *, alpha):
    tm, d = x_ref.shape
    i = pl.program_id(0)

    @pl.when(i == 0)
    def _():
        cnt_sc[...] = jnp.zeros_like(cnt_sc)

    lane256 = lax.broadcasted_iota(jnp.int32, (1, 4 * POOL_GROUP_DIM), 1) // POOL_GROUP_DIM

    xp = zp_ref[...]
    xp_f = xp.astype(F32)
    pooled = jnp.zeros(xp_f.shape, F32)
    for gi in range(len(POOL_WINDOWS)):
        s = jnp.dot(pm_ref[gi], xp, preferred_element_type=F32)
        pooled = jnp.where(lane256 == gi, s, pooled)
    diff = pooled / cnt_ref[...] - xp_f
    pool_o = jnp.dot(diff.astype(BF16), pw_ref[...], preferred_element_type=F32) * ps_ref[...]

    uv = zuv_ref[...].astype(F32)
    uv = 0.5 * uv * (1.0 + lax.erf(uv * (2.0 ** -0.5)))
    half = uv.shape[1] // 2
    u = uv[:, :half]
    vn = (_ln_plain(uv[:, half:]) * sg_ref[...] + sb_ref[...]).astype(BF16)
    mixed_chunks = []
    for c0 in range(0, tm, CHUNK):
        vc = vn[c0:c0 + CHUNK, :]
        mixed = jnp.zeros((CHUNK, half), F32)
        for gi in range(sw_ref.shape[0]):
            s = jnp.dot(sw_ref[gi], vc, preferred_element_type=F32)
            mixed = jnp.where(lane256 == gi, s, mixed)
        mixed_chunks.append(mixed + sbm_ref[...])
    sgu_o = u * jnp.concatenate(mixed_chunks, axis=0)

    hsum = hf_ref[...].astype(F32) + hb_ref[...].astype(F32)
    heads = []
    for hh in range(N_HEADS):
        hv = hsum[:, hh * HEAD_DIM:(hh + 1) * HEAD_DIM]
        mu = jnp.mean(hv, axis=-1, keepdims=True)
        hc = hv - mu
        var = jnp.mean(hc * hc, axis=-1, keepdims=True)
        heads.append(hc * lax.rsqrt(var + HEAD_NORM_EPS))
    hn = jnp.concatenate(heads, axis=1)
    twice_gate = lambda zh: jnp.tanh(zh.astype(F32)) + 1.0
    mlstm_o = hn * ng_ref[...] * twice_gate(zo_ref[...])

    y = twice_gate(zmg_ref[:, 0:d]) * jnp.dot(pool_o.astype(BF16), wbp_ref[...], preferred_element_type=F32)
    y += twice_gate(zmg_ref[:, d:2 * d]) * jnp.dot(mlstm_o.astype(BF16), wbm_ref[...],
                                                   preferred_element_type=F32)
    y += twice_gate(zmg_ref[:, 2 * d:3 * d]) * jnp.dot(sgu_o.astype(BF16), wbs_ref[...],
                                                       preferred_element_type=F32)
    y2 = jnp.dot(y.astype(BF16), wo_ref[...], preferred_element_type=F32)

    g1 = mod_ref[:, 2 * d:3 * d]
    x1 = _ln_plain(alpha * x_ref[...] + g1 * y2) * l1g_ref[...] + l1b_ref[...]
    x1_ref[...] = x1
    h2 = _ln_plain(x1) * (1.0 + mod_ref[:, 4 * d:5 * d]) + mod_ref[:, 3 * d:4 * d]
    h2b = h2.astype(BF16)
    h2_ref[...] = _pack_rows(h2b)

    logits = lax.dot_general(wr_ref[...], h2b, (((1,), (1,)), ((), ())),
                             preferred_element_type=F32) + br_ref[...]
    n_exp = logits.shape[0]
    expert = lax.broadcasted_iota(jnp.int32, logits.shape, 0)
    work = logits
    vals, idxs, hots = [], [], []
    for _ in range(TOP_K):
        mval = jnp.max(work, axis=0, keepdims=True)
        idx = jnp.min(jnp.where(work == mval, expert, n_exp), axis=0, keepdims=True)
        hot = expert == idx
        vals.append(mval)
        idxs.append(idx)
        hots.append(hot)
        work = jnp.where(hot, -jnp.inf, work)
    exps = [jnp.exp(v - vals[0]) for v in vals]
    denom = exps[0] + exps[1] + exps[2] + exps[3]
    sel = (hots[0] | hots[1] | hots[2] | hots[3]).astype(F32)
    r_i = lax.broadcasted_iota(jnp.int32, (tm, tm), 0)
    c_i = lax.broadcasted_iota(jnp.int32, (tm, tm), 1)
    earlier = (r_i < c_i).astype(BF16)
    rank = jnp.dot(sel.astype(BF16), earlier, preferred_element_type=F32) + cnt_sc[:, 0:1]
    rows = ([i_.astype(F32) for i_ in idxs] + [e / denom for e in exps]
            + [jnp.sum(jnp.where(h, rank, 0.0), axis=0, keepdims=True) for h in hots]
            + [jnp.zeros((route_ref.shape[0] - 3 * TOP_K, tm), F32)])
    route_ref[...] = jnp.concatenate(rows, axis=0)
    new_cnt = cnt_sc[:, 0:1] + jnp.sum(sel, axis=1, keepdims=True)
    cnt_sc[...] = jnp.broadcast_to(new_cnt, cnt_sc.shape)
    cnt_out_ref[...] = jnp.broadcast_to(new_cnt, cnt_out_ref.shape)


def _mix(x, zp, zuv, zmg, zo, hf, hb, mods, consts, *, n_rows, n_lat, seq, alpha):
    t, d = x.shape
    tm = ROW_TILE
    n_batch = mods.shape[0] - 1
    row = lambda i: (i, 0)
    const2 = lambda i: (0, 0)
    const3 = lambda i: (0, 0, 0)
    kind = lambda i: jnp.where(i * tm >= n_lat, 1, 0)

    def mod_idx(i):
        return (jnp.minimum((i * tm) // seq, n_batch), 0, 0)

    (pm, cnt, pw, ps, sg, sb, sw, sbm, ng, wbp, wbm, wbs, wo, l1g, l1b, wr, br) = consts
    in_specs = [
        pl.BlockSpec((tm, d), row),
        pl.BlockSpec((tm, zp.shape[1]), row),
        pl.BlockSpec((tm, zuv.shape[1]), row),
        pl.BlockSpec((tm, zmg.shape[1]), row),
        pl.BlockSpec((tm, zo.shape[1]), row),
        pl.BlockSpec((tm, hf.shape[1]), row),
        pl.BlockSpec((tm, hb.shape[1]), row),
        pl.BlockSpec((None, 1, mods.shape[2]), mod_idx),
        pl.BlockSpec((None,) + pm.shape[1:], lambda i: (kind(i), 0, 0, 0)),
        pl.BlockSpec((None,) + cnt.shape[1:], lambda i: (kind(i), 0, 0)),
        pl.BlockSpec(pw.shape, const2),
        pl.BlockSpec(ps.shape, const2),
        pl.BlockSpec(sg.shape, const2),
        pl.BlockSpec(sb.shape, const2),
        pl.BlockSpec(sw.shape, const3),
        pl.BlockSpec(sbm.shape, const2),
        pl.BlockSpec(ng.shape, const2),
        pl.BlockSpec(wbp.shape, const2),
        pl.BlockSpec(wbm.shape, const2),
        pl.BlockSpec(wbs.shape, const2),
        pl.BlockSpec(wo.shape, const2),
        pl.BlockSpec(l1g.shape, const2),
        pl.BlockSpec(l1b.shape, const2),
        pl.BlockSpec(wr.shape, const2),
        pl.BlockSpec(br.shape, const2),
    ]
    return pl.pallas_call(
        functools.partial(_mix_kernel, alpha=alpha),
        grid=(n_rows // tm,),
        in_specs=in_specs,
        out_specs=[
            pl.BlockSpec((tm, d), row),
            pl.BlockSpec((tm, d // 2), row),
            pl.BlockSpec((ROUTE_ROWS, tm), lambda i: (0, i)),
            pl.BlockSpec((wr.shape[0], LANES), const2),
        ],
        out_shape=[
            jax.ShapeDtypeStruct((n_rows, d), F32),
            jax.ShapeDtypeStruct((n_rows, d // 2), jnp.uint32),
            jax.ShapeDtypeStruct((ROUTE_ROWS, n_rows), F32),
            jax.ShapeDtypeStruct((wr.shape[0], LANES), F32),
        ],
        scratch_shapes=[pltpu.VMEM((wr.shape[0], LANES), F32)],
        compiler_params=_cparams("arbitrary"),
    )(x, zp, zuv, zmg, zo, hf, hb, mods, pm, cnt, pw, ps, sg, sb, sw, sbm, ng,
      wbp, wbm, wbs, wo, l1g, l1b, wr, br)


SCATTER_WINDOW = 128
SCATTER_PARTS = 2


def _scatter_rows(x, idx, n_out, part):
    n = x.shape[0]
    w = x.shape[1] // SCATTER_PARTS
    n_idx = idx.shape[0]
    mesh = plsc.VectorSubcoreMesh(core_axis_name="core", subcore_axis_name="subcore")

    @pl.kernel(out_type=jax.ShapeDtypeStruct((n_out, w), x.dtype), mesh=mesh, scratch_types=[])
    def scatter_kernel(x_hbm, i_hbm, o_hbm):
        def body(x_vmem, i_vmem):
            for k in range(n_idx):
                pltpu.sync_copy(x_vmem, o_hbm.at[i_vmem.at[k]])

        pltpu.emit_pipeline(
            body,
            grid=(n // SCATTER_WINDOW,),
            in_specs=[pl.BlockSpec((SCATTER_WINDOW, w), lambda i: (i, part)),
                      pl.BlockSpec((n_idx, SCATTER_WINDOW), lambda i: (0, i))],
            out_specs=[],
            core_axis_name=("core", "subcore"),
            dimension_semantics=(pltpu.PARALLEL,),
        )(x_hbm, i_hbm)

    return scatter_kernel(x, idx)


def _gather_rows(x, idx):
    w = x.shape[1]
    n_idx = idx.shape[1]
    mesh = plsc.VectorSubcoreMesh(core_axis_name="core", subcore_axis_name="subcore")

    @pl.kernel(out_type=jax.ShapeDtypeStruct((n_idx, w), x.dtype), mesh=mesh, scratch_types=[])
    def gather_kernel(x_hbm, i_hbm, o_hbm):
        def body(i_vmem, o_vmem):
            pltpu.sync_copy(x_hbm.at[i_vmem.at[0]], o_vmem)

        pltpu.emit_pipeline(
            body,
            grid=(n_idx // SCATTER_WINDOW,),
            in_specs=[pl.BlockSpec((1, SCATTER_WINDOW), lambda i: (0, i))],
            out_specs=[pl.BlockSpec((SCATTER_WINDOW, w), lambda i: (i, 0))],
            core_axis_name=("core", "subcore"),
            dimension_semantics=(pltpu.PARALLEL,),
        )(i_hbm, o_hbm)

    return gather_kernel(x, idx)


def _expert_kernel(te_ref, nt_ref, xa_ref, xb_ref, wgu_ref, bgu_ref, wd_ref, bd_ref, ya_ref, yb_ref,
                   wgu_sc, wd_sc):
    i = pl.program_id(0)
    f = wd_ref.shape[0]
    used = i < nt_ref[0]

    @pl.when(jnp.logical_and(used, jnp.logical_or(i == 0, te_ref[i] != te_ref[jnp.maximum(i - 1, 0)])))
    def _():
        for c0 in range(0, 2 * f, 512):
            wgu_sc[:, c0:c0 + 512] = wgu_ref[:, c0:c0 + 512].astype(BF16)
        for c0 in range(0, f, 256):
            wd_sc[c0:c0 + 256, :] = wd_ref[c0:c0 + 256, :].astype(BF16)

    @pl.when(used)
    def _():
        x = _unpack_rows([xa_ref[...], xb_ref[...]]).astype(BF16)
        acc = jnp.zeros((x.shape[0], wd_ref.shape[1]), F32)
        for c0 in range(0, f, 512):
            gate = jnp.dot(x, wgu_sc[:, c0:c0 + 512], preferred_element_type=F32) + bgu_ref[:, c0:c0 + 512]
            up = (jnp.dot(x, wgu_sc[:, f + c0:f + c0 + 512], preferred_element_type=F32)
                  + bgu_ref[:, f + c0:f + c0 + 512])
            gate = jnp.minimum(gate, SWIGLU_LIMIT)
            up = jnp.clip(up, -SWIGLU_LIMIT, SWIGLU_LIMIT)
            glu = gate * _sigmoid(SWIGLU_ALPHA * gate)
            act = ((up + 1.0) * glu).astype(BF16)
            acc += jnp.dot(act, wd_sc[c0:c0 + 512, :], preferred_element_type=F32)
        words = _pack_rows(acc + bd_ref[...])
        ya_ref[...] = words[:, :ya_ref.shape[1]]
        yb_ref[...] = words[:, ya_ref.shape[1]:]

    @pl.when(jnp.logical_not(used))
    def _():
        ya_ref[...] = jnp.zeros_like(ya_ref)
        yb_ref[...] = jnp.zeros_like(yb_ref)


def _experts(tile_expert, n_tiles_used, xa, xb, wgu, bgu, wd, bd, *, layer):
    p = xa.shape[0]
    d = 2 * SCATTER_PARTS * xa.shape[1]
    tm = EXPERT_TILE
    f2 = wgu.shape[3]
    grid_spec = pltpu.PrefetchScalarGridSpec(
        num_scalar_prefetch=2,
        grid=(p // tm,),
        in_specs=[
            pl.BlockSpec((tm, xa.shape[1]), lambda i, te, nt: (i, 0)),
            pl.BlockSpec((tm, xb.shape[1]), lambda i, te, nt: (i, 0)),
            pl.BlockSpec((None, None, d, f2), lambda i, te, nt: (layer, te[i], 0, 0)),
            pl.BlockSpec((None, None, 1, f2), lambda i, te, nt: (layer, te[i], 0, 0)),
            pl.BlockSpec((None, None, f2 // 2, d), lambda i, te, nt: (layer, te[i], 0, 0)),
            pl.BlockSpec((None, None, 1, d), lambda i, te, nt: (layer, te[i], 0, 0)),
        ],
        out_specs=[pl.BlockSpec((tm, xa.shape[1]), lambda i, te, nt: (i, 0)),
                   pl.BlockSpec((tm, xb.shape[1]), lambda i, te, nt: (i, 0))],
        scratch_shapes=[pltpu.VMEM((d, f2), BF16), pltpu.VMEM((f2 // 2, d), BF16)],
    )
    return pl.pallas_call(
        _expert_kernel,
        grid_spec=grid_spec,
        out_shape=[jax.ShapeDtypeStruct(xa.shape, xa.dtype), jax.ShapeDtypeStruct(xb.shape, xb.dtype)],
        compiler_params=_cparams("arbitrary"),
    )(tile_expert, n_tiles_used, xa, xb, wgu, bgu, wd, bd)


def _combine_kernel(x1_ref, ya_ref, yb_ref, route_ref, mod_ref, g_ref, b_ref, *rest, alpha):
    x2_ref = rest[-1]
    d = x1_ref.shape[1]
    tm = x1_ref.shape[0]
    route = jnp.concatenate([route_ref[...], jnp.zeros((LANES - ROUTE_ROWS, tm), F32)], axis=0).T
    f = jnp.zeros(x1_ref.shape, F32)
    for kk in range(TOP_K):
        f += route[:, TOP_K + kk:TOP_K + kk + 1] * _unpack_rows([ya_ref[kk], yb_ref[kk]])
    g2 = mod_ref[:, 5 * d:6 * d]
    x2_ref[...] = _ln_plain(alpha * x1_ref[...] + g2 * f) * g_ref[...] + b_ref[...]


def _combine(x1, yg_a, yg_b, route, mods, g, b, *, seq, alpha, dest=None, dest_rows=None, tile_offset=0):
    t, d = x1.shape
    tm = COMBINE_TILE
    assert t % tm == 0 and seq % tm == 0
    n_batch = mods.shape[0] - 1
    row = lambda i: (i, 0)

    def mod_idx(i):
        return (jnp.minimum((i * tm) // seq, n_batch), 0, 0)

    in_specs = [
        pl.BlockSpec((tm, d), row),
        pl.BlockSpec((TOP_K, tm, yg_a.shape[2]), lambda i: (0, i, 0)),
        pl.BlockSpec((TOP_K, tm, yg_b.shape[2]), lambda i: (0, i, 0)),
        pl.BlockSpec((ROUTE_ROWS, tm), lambda i: (0, i)),
        pl.BlockSpec((None, 1, mods.shape[2]), mod_idx),
        pl.BlockSpec(g.shape, lambda i: (0, 0)),
        pl.BlockSpec(b.shape, lambda i: (0, 0)),
    ]
    args = [x1, yg_a, yg_b, route, mods, g, b]
    aliases = {}
    if dest is not None:
        in_specs.append(pl.BlockSpec(memory_space=pl.ANY))
        args.append(dest)
        aliases = {len(args) - 1: 0}
    return pl.pallas_call(
        functools.partial(_combine_kernel, alpha=alpha),
        grid=(t // tm,),
        in_specs=in_specs,
        out_specs=pl.BlockSpec((tm, d), lambda i: (i + tile_offset, 0)),
        out_shape=jax.ShapeDtypeStruct((t if dest_rows is None else dest_rows, d), F32),
        input_output_aliases=aliases,
        compiler_params=_cparams("parallel"),
    )(*args)


def kernel(x, c, ctx, c_ctx, w_ada, b_ada, w_in, b_in, pool_w, pool_scale, qk_conv_w, qk_conv_b,
           mlstm_norm_g, sgu_ln_g, sgu_ln_b, sgu_w, sgu_b, w_br_pool, w_br_mlstm, w_br_sgu, w_out,
           ln1_g, ln1_b, w_router, b_router, w_gate_up, b_gate_up, w_down, b_down, ln2_g, ln2_b):
    n_batch, seq, d = x.shape
    ctx_len = ctx.shape[1]
    depth = w_in.shape[0]
    n_exp = w_router.shape[2]
    assert ctx_len == ROW_TILE and seq % ROW_TILE == 0 and ROW_TILE % GRID_W == 0
    assert n_batch % N_STREAMS == 0
    alpha = float((2 * depth) ** 0.25)

    rows = ((n_batch + 1 + 7) // 8) * 8
    cc = jnp.zeros((rows, d), F32).at[:n_batch].set(c).at[n_batch].set(c_ctx)
    mods_all = _ada_table(cc, w_ada, b_ada)[:, :n_batch + 1].reshape(depth, n_batch + 1, 1, 6 * d)

    pool_w_, mlw = 256, 512
    col = lambda a, lo, hi: a[..., lo:hi]
    g_lo = pool_w_ + 4 * mlw
    parts = ((pool_w_, pool_w_ + 2 * mlw), (pool_w_ + 2 * mlw, pool_w_ + 3 * mlw), (pool_w_ + 3 * mlw, g_lo),
             (g_lo + _N_GATE_COLS, g_lo + _N_GATE_COLS + 512), (0, pool_w_),
             (g_lo + _N_GATE_COLS + 512, w_in.shape[2]))
    part_scale = (1.0, 1.0, 0.5, 1.0, 1.0, 0.5)
    w_main = jnp.concatenate([col(w_in, lo, hi) * sc for (lo, hi), sc in zip(parts, part_scale)],
                             axis=-1).astype(BF16)
    b_main = jnp.concatenate([col(b_in, lo, hi) * sc for (lo, hi), sc in zip(parts, part_scale)],
                             axis=-1)[:, None, :]
    gw, gb = col(w_in, g_lo, g_lo + _N_GATE_COLS), col(b_in, g_lo, g_lo + _N_GATE_COLS)
    nh = N_HEADS
    pick = lambda a, k: jnp.concatenate([a[..., k * nh:(k + 1) * nh], a[..., (k + 2) * nh:(k + 3) * nh]], axis=-1)
    w_gate_t = jnp.swapaxes(jnp.concatenate([pick(gw, 0), pick(gw, 1)], axis=-1), 1, 2).astype(BF16)
    b_gate_t = jnp.concatenate([pick(gb, 0), pick(gb, 1)], axis=-1)[:, :, None]

    pm, cnt = _pool_constants(ctx_len)
    eye = jnp.eye(len(POOL_WINDOWS), dtype=F32)
    pool_bd = jnp.einsum('lgcd,gh->lgchd', pool_w, eye).reshape(depth, 256, 256).astype(BF16)
    sgu_bias_map = jnp.repeat(jnp.swapaxes(sgu_b, 1, 2), sgu_ln_g.shape[1] // sgu_b.shape[1], axis=2)
    w_router_t = jnp.swapaxes(w_router, 1, 2).astype(BF16)
    b_router_t = b_router[:, :, None]
    r1 = lambda a: a[:, None, :]
    b_gu4 = b_gate_up[:, :, None, :]
    b_d4 = b_down[:, :, None, :]

    nb = n_batch // N_STREAMS
    n_lat = nb * seq
    t_all = n_lat + nb * ctx_len
    streams, stream_mods = [], []
    for h in range(N_STREAMS):
        sl = slice(h * nb, (h + 1) * nb)
        streams.append(jnp.concatenate([x[sl].reshape(n_lat, d), ctx[sl].reshape(nb * ctx_len, d)], axis=0))
        stream_mods.append(jnp.concatenate([mods_all[:, sl], mods_all[:, n_batch:n_batch + 1]], axis=1))
    out = None
    for i in range(depth):
        last = i == depth - 1
        for h in range(N_STREAMS):
            stream = streams[h]
            mods = stream_mods[h][i]
            zqk, zv, zo, zuv, zp, zmg, g, gt = _inproj(
                stream, mods, w_main[i], b_main[i], w_gate_t[i], b_gate_t[i],
                qk_conv_w[i], r1(qk_conv_b)[i], n_lat=n_lat, seq=seq, ctx_len=ctx_len)
            hf, hb = _mlstm(zqk, zv, g, gt, n_batch=nb, seq=seq, ctx_len=ctx_len)
            n_rows = n_lat if last else t_all
            consts = (pm, cnt, pool_bd[i], r1(pool_scale)[i], r1(sgu_ln_g)[i], r1(sgu_ln_b)[i],
                      sgu_w[i].astype(BF16), sgu_bias_map[i], r1(0.5 * mlstm_norm_g)[i],
                      w_br_pool[i].astype(BF16), w_br_mlstm[i].astype(BF16), w_br_sgu[i].astype(BF16),
                      (0.5 * w_out[i]).astype(BF16), r1(ln1_g)[i], r1(ln1_b)[i], w_router_t[i], b_router_t[i])
            x1, h2, route, counts = _mix(stream, zp, zuv, zmg, zo, hf, hb, mods, consts,
                                         n_rows=n_rows, n_lat=n_lat, seq=seq, alpha=alpha)

            cnt_e = counts[:, 0].astype(jnp.int32)
            padded = ((cnt_e + EXPERT_TILE - 1) // EXPERT_TILE) * EXPERT_TILE
            ends = jnp.cumsum(padded)
            starts = ends - padded
            e_idx = route[0:TOP_K].astype(jnp.int32)
            rank = route[2 * TOP_K:3 * TOP_K].astype(jnp.int32)
            start_of = jnp.sum(jnp.where(e_idx[..., None] == jnp.arange(n_exp, dtype=jnp.int32), starts, 0),
                               axis=-1)
            pos_t = start_of + rank
            p_rows = n_rows * TOP_K + n_exp * EXPERT_TILE
            n_tiles = p_rows // EXPERT_TILE
            tile_start = jnp.arange(n_tiles, dtype=jnp.int32) * EXPERT_TILE
            tile_expert = jnp.minimum(
                jnp.sum((ends[None, :] <= tile_start[:, None]).astype(jnp.int32), axis=1), n_exp - 1)
            n_used = (ends[-1] // EXPERT_TILE).astype(jnp.int32).reshape(1)
            xs_a = _scatter_rows(h2, pos_t, p_rows, 0)
            xs_b = _scatter_rows(h2, pos_t, p_rows, 1)
            ys_a, ys_b = _experts(tile_expert, n_used, xs_a, xs_b, w_gate_up, b_gu4, w_down, b_d4, layer=i)
            pos_flat = pos_t.reshape(1, TOP_K * n_rows)
            yg_a = _gather_rows(ys_a, pos_flat).reshape(TOP_K, n_rows, -1)
            yg_b = _gather_rows(ys_b, pos_flat).reshape(TOP_K, n_rows, -1)
            if last:
                out = _combine(x1, yg_a, yg_b, route, mods, r1(ln2_g)[i], r1(ln2_b)[i], seq=seq, alpha=alpha,
                               dest=out, dest_rows=n_batch * seq, tile_offset=h * n_lat // COMBINE_TILE)
            else:
                streams[h] = _combine(x1, yg_a, yg_b, route, mods, r1(ln2_g)[i], r1(ln2_b)[i],
                                      seq=seq, alpha=alpha)
    return out.reshape(n_batch, seq, d)
```

```python
import functools

import numpy as np
import jax
import jax.numpy as jnp
from jax import lax
from jax.experimental import pallas as pl
from jax.experimental.pallas import tpu as pltpu
from jax.experimental.pallas import tpu_sc as plsc

F32 = jnp.float32
BF16 = jnp.bfloat16

GRID_W = 64
POOL_WINDOWS = (2, 4, 8, 16)
POOL_GROUP_DIM = 64
N_HEADS = 4
HEAD_DIM = 128
CHUNK = 128
TOP_K = 4
SWIGLU_LIMIT = 7.0
SWIGLU_ALPHA = 1.702
LN_EPS = 1e-5
HEAD_NORM_EPS = 1e-6

ROW_TILE = 256
INPROJ_TILE = 512
COMBINE_TILE = 1024
SEAM_ROWS = 16
ROUTE_ROWS = 16
EXPERT_TILE = 512
N_STREAMS = 2
SUBLANES = 8
LANES = 128
VMEM_LIMIT = 56 * 1024 * 1024


def _cparams(*sem):
    return pltpu.CompilerParams(dimension_semantics=sem, vmem_limit_bytes=VMEM_LIMIT)


def _ln_plain(x):
    mu = jnp.mean(x, axis=-1, keepdims=True)
    xc = x - mu
    var = jnp.mean(xc * xc, axis=-1, keepdims=True)
    return xc * lax.rsqrt(var + LN_EPS)


def _sigmoid(x):
    return 0.5 * jnp.tanh(0.5 * x) + 0.5


def _log_sigmoid(x):
    return jnp.minimum(x, 0.0) - jnp.log1p(jnp.exp(-jnp.abs(x)))


def _pack_rows(x):
    half = x.shape[1] // 2
    bits = lax.bitcast_convert_type(x.astype(BF16).astype(F32), jnp.uint32)
    return (bits[:, half:] & jnp.uint32(0xFFFF0000)) | (bits[:, :half] >> 16)


def _unpack_rows(parts):
    low = [lax.bitcast_convert_type(w << 16, F32) for w in parts]
    high = [lax.bitcast_convert_type(w & jnp.uint32(0xFFFF0000), F32) for w in parts]
    return jnp.concatenate(low + high, axis=1)


def _ada_kernel(c_ref, w_ref, b_ref, o_ref):
    c = c_ref[...]
    s = c * _sigmoid(c)
    o_ref[...] = jnp.dot(s, w_ref[...], preferred_element_type=F32,
                         precision=lax.Precision.HIGHEST) + b_ref[...]


def _ada_table(cc, w_ada, b_ada):
    depth, d, n = w_ada.shape
    rows = cc.shape[0]
    bn = 1536
    return pl.pallas_call(
        _ada_kernel,
        grid=(depth, n // bn),
        in_specs=[
            pl.BlockSpec((rows, d), lambda l, j: (0, 0)),
            pl.BlockSpec((None, d, bn), lambda l, j: (l, 0, j)),
            pl.BlockSpec((None, 1, bn), lambda l, j: (l, 0, j)),
        ],
        out_specs=pl.BlockSpec((None, rows, bn), lambda l, j: (l, 0, j)),
        out_shape=jax.ShapeDtypeStruct((depth, rows, n), F32),
        compiler_params=_cparams("parallel", "parallel"),
    )(cc, w_ada, b_ada.reshape(depth, 1, n))


_QK_WIDTH = 2 * N_HEADS * HEAD_DIM
_Z_PARTS = (("v", 512), ("o", 512), ("uv", 512), ("pool", 256), ("mg", 3072))
_N_GATE_COLS = 4 * N_HEADS


def _inproj_kernel(x_ref, xp_ref, xn_ref, mod_ref, w_ref, b_ref, wgt_ref, bgt_ref,
                   cw_ref, cb_ref, qk_ref, v_ref, o_ref, uv_ref, p_ref, mg_ref, g_ref, gt_ref, pad_sc,
                   *, n_lat, seq, ctx_len):
    tm, d = x_ref.shape
    i = pl.program_id(0)
    is_lat = i * tm < n_lat
    has_prev = jnp.logical_and(is_lat, (i * tm) % seq != 0).astype(F32)
    has_next = jnp.logical_and(is_lat, ((i + 1) * tm) % seq != 0).astype(F32)

    xe = jnp.concatenate([xp_ref[...], x_ref[...], xn_ref[...]], axis=0)
    he = (_ln_plain(xe) * (1.0 + mod_ref[:, d:2 * d]) + mod_ref[:, 0:d]).astype(BF16)
    hb = he[SUBLANES:SUBLANES + tm, :]

    half = _QK_WIDTH // 2

    def qk_project(part):
        c0 = part * half
        z = jnp.dot(he, w_ref[:, c0:c0 + half], preferred_element_type=F32) + b_ref[:, c0:c0 + half]
        pad_sc[part] = z
        pad_sc[part, SUBLANES - 1:SUBLANES, :] = pad_sc[part, SUBLANES - 1:SUBLANES, :] * has_prev
        pad_sc[part, SUBLANES + tm:SUBLANES + tm + 1, :] = (
            pad_sc[part, SUBLANES + tm:SUBLANES + tm + 1, :] * has_next)

    def conv_act(part, prev, cur, nxt):
        c0 = part * half
        y = (cb_ref[:, c0:c0 + half] + prev * cw_ref[0:1, c0:c0 + half]
             + cur * cw_ref[1:2, c0:c0 + half] + nxt * cw_ref[2:3, c0:c0 + half])
        act = y * _sigmoid(y)
        if part == 1:
            act = act * (HEAD_DIM ** -0.5)
        return act.astype(qk_ref.dtype)

    def qk_conv(part):
        qk_ref[:, part * half:(part + 1) * half] = conv_act(
            part, pad_sc[part, SUBLANES - 1:SUBLANES - 1 + tm, :], pad_sc[part, SUBLANES:SUBLANES + tm, :],
            pad_sc[part, SUBLANES + 1:SUBLANES + 1 + tm, :])

    def plain(ref, off, c0, cw):
        z = jnp.dot(hb, w_ref[:, off + c0:off + c0 + cw], preferred_element_type=F32)
        ref[:, c0:c0 + cw] = (z + b_ref[:, off + c0:off + c0 + cw]).astype(ref.dtype)

    mxu_tasks = []
    off = _QK_WIDTH
    for ref, (_, width) in zip((v_ref, o_ref, uv_ref, p_ref, mg_ref), _Z_PARTS):
        for c0 in range(0, width, 512):
            mxu_tasks.append(functools.partial(plain, ref, off, c0, min(512, width - c0)))
        off += width
    vpu_tasks = {2: functools.partial(_gate_prep, hb, wgt_ref, bgt_ref, g_ref, gt_ref),
                 4: functools.partial(qk_conv, 0), 6: functools.partial(qk_conv, 1)}
    qk_project(0)
    qk_project(1)
    for n, task in enumerate(mxu_tasks):
        if n in vpu_tasks:
            vpu_tasks[n]()
        task()

    if tm > ctx_len:
        @pl.when(jnp.logical_not(is_lat))
        def _():
            r = lax.broadcasted_iota(jnp.int32, (2 * SEAM_ROWS, 1), 0)
            for part in range(2):
                for p in range(ctx_len, tm, ctx_len):
                    lo = SUBLANES + p - SEAM_ROWS
                    prev = jnp.where(r == SEAM_ROWS, 0.0, pad_sc[part, lo - 1:lo - 1 + 2 * SEAM_ROWS, :])
                    nxt = jnp.where(r == SEAM_ROWS - 1, 0.0, pad_sc[part, lo + 1:lo + 1 + 2 * SEAM_ROWS, :])
                    qk_ref[p - SEAM_ROWS:p + SEAM_ROWS, part * half:(part + 1) * half] = conv_act(
                        part, prev, pad_sc[part, lo:lo + 2 * SEAM_ROWS, :], nxt)


def _gate_prep(hb, wgt_ref, bgt_ref, g_ref, gt_ref):
    tm = hb.shape[0]
    nd = 2 * N_HEADS
    ggt = lax.dot_general(wgt_ref[...], hb, (((1,), (1,)), ((), ())),
                          preferred_element_type=F32) + bgt_ref[...]
    git, gft = ggt[:nd, :], ggt[nd:, :]
    fwd = lax.broadcasted_iota(jnp.int32, (nd, tm), 0) < N_HEADS
    pos = lax.broadcasted_iota(jnp.int32, (nd, tm), 1) % CHUNK
    lsg = _log_sigmoid(gft)
    bcum = jnp.where(fwd, _chunk_scan(lsg, pos, 1, False, False), _chunk_scan(lsg, pos, 1, True, False))
    c = git - bcum
    cmax = jnp.where(fwd, _chunk_scan(c, pos, 1, False, True), _chunk_scan(c, pos, 1, True, True))
    gt_ref[...] = c
    rows = jnp.concatenate([bcum, c, bcum + cmax, cmax, jnp.zeros((LANES - 4 * nd, tm), F32)], axis=0)
    g_ref[...] = rows.T


def _chunk_scan(a, pos, axis, reverse, use_max):
    n = a.shape[axis]
    ident = -jnp.inf if use_max else 0.0
    sh = 1
    while sh < CHUNK:
        if reverse:
            moved = jnp.where(pos < CHUNK - sh, pltpu.roll(a, n - sh, axis=axis), ident)
        else:
            moved = jnp.where(pos >= sh, pltpu.roll(a, sh, axis=axis), ident)
        a = jnp.maximum(a, moved) if use_max else a + moved
        sh *= 2
    return a


def _inproj(x, mods, w, b, wgt, bgt, conv_w, conv_b, *, n_lat, seq, ctx_len):
    t, d = x.shape
    tm = INPROJ_TILE
    assert seq % tm == 0 and n_lat % tm == 0 and (t - n_lat) % tm == 0 and tm % ctx_len == 0
    n_batch = mods.shape[0] - 1
    per_tile = tm // SUBLANES

    def mod_idx(i):
        return (jnp.minimum((i * tm) // seq, n_batch), 0, 0)

    const = lambda i: (0, 0)
    row = lambda i: (i, 0)
    parts = (("qk", _QK_WIDTH),) + _Z_PARTS
    out_shapes = [jax.ShapeDtypeStruct((t, width), BF16) for _, width in parts]
    out_specs = [pl.BlockSpec((tm, width), row) for _, width in parts]
    out_shapes += [jax.ShapeDtypeStruct((t, LANES), F32), jax.ShapeDtypeStruct((2 * N_HEADS, t), F32)]
    out_specs += [pl.BlockSpec((tm, LANES), row), pl.BlockSpec((2 * N_HEADS, tm), lambda i: (0, i))]
    return pl.pallas_call(
        functools.partial(_inproj_kernel, n_lat=n_lat, seq=seq, ctx_len=ctx_len),
        grid=(t // tm,),
        in_specs=[
            pl.BlockSpec((tm, d), row),
            pl.BlockSpec((SUBLANES, d), lambda i: (jnp.maximum(i * per_tile - 1, 0), 0)),
            pl.BlockSpec((SUBLANES, d), lambda i: (jnp.minimum((i + 1) * per_tile, t // SUBLANES - 1), 0)),
            pl.BlockSpec((None, 1, mods.shape[2]), mod_idx),
            pl.BlockSpec(w.shape, const),
            pl.BlockSpec(b.shape, const),
            pl.BlockSpec(wgt.shape, const),
            pl.BlockSpec(bgt.shape, const),
            pl.BlockSpec(conv_w.shape, const),
            pl.BlockSpec(conv_b.shape, const),
        ],
        out_specs=out_specs,
        out_shape=out_shapes,
        scratch_shapes=[pltpu.VMEM((2, tm + 2 * SUBLANES, _QK_WIDTH // 2), F32)],
        compiler_params=_cparams("parallel"),
    )(x, x, x, mods, w, b, wgt, bgt, conv_w, conv_b)


def _mlstm_row_block(j, b, direction, *, n_batch, seq, ctx_len):
    ncc = ctx_len // CHUNK
    ncl = seq // CHUNK
    is_ctx = j < ncc
    if direction == 0:
        c = jnp.where(is_ctx, j, j - ncc)
    else:
        c = jnp.where(is_ctx, ncc - 1 - j, ncl - 1 - (j - ncc))
    base = jnp.where(is_ctx, (n_batch * seq + b * ctx_len) // CHUNK, (b * seq) // CHUNK)
    return base + c


def _mlstm_kernel(q_f, k_f, v_f, g_f, gt_f, q_b, k_b, v_b, g_b, gt_b, hf_ref, hb_ref, s_sc, m_sc):
    j = pl.program_id(1)

    @pl.when(j == 0)
    def _():
        s_sc[...] = jnp.zeros_like(s_sc)
        m_sc[...] = jnp.zeros_like(m_sc)

    ri = lax.broadcasted_iota(jnp.int32, (CHUNK, CHUNK), 0)
    ci = lax.broadcasted_iota(jnp.int32, (CHUNK, CHUNK), 1)
    ones_tile = jnp.ones((CHUNK, HEAD_DIM), BF16)
    dirs = ((q_f, k_f, v_f, g_f, gt_f, hf_ref), (q_b, k_b, v_b, g_b, gt_b, hb_ref))
    combos = [(d, hh) for d in range(2) for hh in range(N_HEADS)]
    heads = lambda ref, hh: ref[:, hh * HEAD_DIM:(hh + 1) * HEAD_DIM]

    nd = 2 * N_HEADS
    tri = ((ci <= ri), (ci >= ri))
    g = (g_f[...], g_b[...])
    gt = (gt_f[...], gt_b[...])

    qk_s, v1, state, m_prev = {}, {}, {}, {}
    for d, hh in combos:
        q_ref, k_ref, v_ref = dirs[d][:3]
        qk_s[d, hh] = lax.dot_general(heads(q_ref, hh), heads(k_ref, hh), (((1,), (1,)), ((), ())),
                                      preferred_element_type=F32)
        v1[d, hh] = jnp.concatenate([heads(v_ref, hh), ones_tile], axis=1)
        state[d, hh] = s_sc[d, hh]
        m_prev[d, hh] = m_sc[d * N_HEADS + hh][0:1, 0:1]

    q_state = {}
    for d, hh in combos:
        q_state[d, hh] = jnp.dot(heads(dirs[d][0], hh), state[d, hh].astype(BF16), preferred_element_type=F32)

    m_t, decay, scores = {}, {}, {}
    for d, hh in combos:
        col = d * N_HEADS + hh
        bcum = g[d][:, col:col + 1]
        inter = bcum + m_prev[d, hh]
        m_t[d, hh] = jnp.maximum(inter, g[d][:, 2 * nd + col:2 * nd + col + 1])
        decay[d, hh] = jnp.exp(inter - m_t[d, hh])
        dlog = jnp.where(tri[d], (bcum - m_t[d, hh]) + gt[d][col:col + 1, :], -jnp.inf)
        scores[d, hh] = (qk_s[d, hh] * jnp.exp(dlog)).astype(BF16)

    for d, hh in combos:
        numden = decay[d, hh] * q_state[d, hh] + jnp.dot(scores[d, hh], v1[d, hh], preferred_element_type=F32)
        num = numden[:, :HEAD_DIM]
        den = numden[:, HEAD_DIM:]
        out = num / jnp.maximum(jnp.abs(den), jnp.exp(-m_t[d, hh]))
        out_ref = dirs[d][5]
        out_ref[:, hh * HEAD_DIM:(hh + 1) * HEAD_DIM] = out.astype(out_ref.dtype)

    kw, a, m_new = {}, {}, {}
    for d, hh in combos:
        col = d * N_HEADS + hh
        last = CHUNK - 1 if d == 0 else 0
        total = g[d][last:last + 1, col:col + 1]
        m_new[d, hh] = total + jnp.maximum(m_prev[d, hh], g[d][last:last + 1, 3 * nd + col:3 * nd + col + 1])
        a[d, hh] = jnp.exp(total + m_prev[d, hh] - m_new[d, hh])
        wk = jnp.exp(total + g[d][:, nd + col:nd + col + 1] - m_new[d, hh])
        kw[d, hh] = (heads(dirs[d][1], hh).astype(F32) * wk).astype(BF16)

    for d, hh in combos:
        upd = lax.dot_general(kw[d, hh], v1[d, hh], (((0,), (0,)), ((), ())), preferred_element_type=F32)
        s_sc[d, hh] = a[d, hh] * state[d, hh] + upd
        m_sc[d * N_HEADS + hh] = jnp.broadcast_to(m_new[d, hh], m_sc.shape[1:])


def _mlstm(zqk, zv, g, gt, *, n_batch, seq, ctx_len):
    t = zqk.shape[0]
    nsteps = (seq + ctx_len) // CHUNK
    width = N_HEADS * HEAD_DIM
    idx = functools.partial(_mlstm_row_block, n_batch=n_batch, seq=seq, ctx_len=ctx_len)

    def specs(d):
        rb = lambda b, j: idx(j, b, d)
        return [
            pl.BlockSpec((CHUNK, width), lambda b, j: (rb(b, j), 0)),
            pl.BlockSpec((CHUNK, width), lambda b, j: (rb(b, j), 1)),
            pl.BlockSpec((CHUNK, width), lambda b, j: (rb(b, j), 0)),
            pl.BlockSpec((CHUNK, LANES), lambda b, j: (rb(b, j), 0)),
            pl.BlockSpec((2 * N_HEADS, CHUNK), lambda b, j: (0, rb(b, j))),
        ]

    out_spec = lambda d: pl.BlockSpec((CHUNK, width), lambda b, j: (idx(j, b, d), 0))
    return pl.pallas_call(
        _mlstm_kernel,
        grid=(n_batch, nsteps),
        in_specs=specs(0) + specs(1),
        out_specs=[out_spec(0), out_spec(1)],
        out_shape=[jax.ShapeDtypeStruct((t, width), BF16)] * 2,
        scratch_shapes=[
            pltpu.VMEM((2, N_HEADS, HEAD_DIM, 2 * HEAD_DIM), F32),
            pltpu.VMEM((2 * N_HEADS, 8, LANES), F32),
        ],
        compiler_params=_cparams("parallel", "arbitrary"),
    )(zqk, zqk, zv, g, gt, zqk, zqk, zv, g, gt)


def _pool_constants(ctx_len):
    tm = ROW_TILE
    masks = np.zeros((2, len(POOL_WINDOWS), tm, tm), np.float32)
    cnts = np.zeros((2, tm, len(POOL_WINDOWS) * POOL_GROUP_DIM), np.float32)
    for kind, length in enumerate((GRID_W, ctx_len)):
        for gi, w in enumerate(POOL_WINDOWS):
            for t in range(tm):
                base = (t // length) * length
                tl = t - base
                lo = min(max(tl - w // 2, 0), length)
                hi = min(max(tl + w // 2, 0), length)
                masks[kind, gi, t, base + lo:base + hi] = 1.0
                cnts[kind, t, gi * POOL_GROUP_DIM:(gi + 1) * POOL_GROUP_DIM] = hi - lo
    return jnp.asarray(masks, BF16), jnp.asarray(cnts, F32)


def _mix_kernel(x_ref, zp_ref, zuv_ref, zmg_ref, zo_ref, hf_ref, hb_ref, mod_ref,
                pm_ref, cnt_ref, pw_ref, ps_ref, sg_ref, sb_ref, sw_ref, sbm_ref, ng_ref,
                wbp_ref, wbm_ref, wbs_ref, wo_ref, l1g_ref, l1b_ref, wr_ref, br_ref,
                x1_ref, h2_ref, route_ref, cnt_out_ref, cnt_sc, *, alpha):
    tm, d = x_ref.shape
    i = pl.program_id(0)

    @pl.when(i == 0)
    def _():
        cnt_sc[...] = jnp.zeros_like(cnt_sc)

    lane256 = lax.broadcasted_iota(jnp.int32, (1, 4 * POOL_GROUP_DIM), 1) // POOL_GROUP_DIM

    xp = zp_ref[...]
    xp_f = xp.astype(F32)
    pooled = jnp.zeros(xp_f.shape, F32)
    for gi in range(len(POOL_WINDOWS)):
        s = jnp.dot(pm_ref[gi], xp, preferred_element_type=F32)
        pooled = jnp.where(lane256 == gi, s, pooled)
    diff = pooled / cnt_ref[...] - xp_f
    pool_o = jnp.dot(diff.astype(BF16), pw_ref[...], preferred_element_type=F32) * ps_ref[...]

    uv = zuv_ref[...].astype(F32)
    uv = 0.5 * uv * (1.0 + lax.erf(uv * (2.0 ** -0.5)))
    half = uv.shape[1] // 2
    u = uv[:, :half]
    vn = (_ln_plain(uv[:, half:]) * sg_ref[...] + sb_ref[...]).astype(BF16)
    mixed_chunks = []
    for c0 in range(0, tm, CHUNK):
        vc = vn[c0:c0 + CHUNK, :]
        mixed = jnp.zeros((CHUNK, half), F32)
        for gi in range(sw_ref.shape[0]):
            s = jnp.dot(sw_ref[gi], vc, preferred_element_type=F32)
            mixed = jnp.where(lane256 == gi, s, mixed)
        mixed_chunks.append(mixed + sbm_ref[...])
    sgu_o = u * jnp.concatenate(mixed_chunks, axis=0)

    hsum = hf_ref[...].astype(F32) + hb_ref[...].astype(F32)
    heads = []
    for hh in range(N_HEADS):
        hv = hsum[:, hh * HEAD_DIM:(hh + 1) * HEAD_DIM]
        mu = jnp.mean(hv, axis=-1, keepdims=True)
        hc = hv - mu
        var = jnp.mean(hc * hc, axis=-1, keepdims=True)
        heads.append(hc * lax.rsqrt(var + HEAD_NORM_EPS))
    hn = jnp.concatenate(heads, axis=1)
    twice_gate = lambda zh: jnp.tanh(zh.astype(F32)) + 1.0
    mlstm_o = hn * ng_ref[...] * twice_gate(zo_ref[...])

    y = twice_gate(zmg_ref[:, 0:d]) * jnp.dot(pool_o.astype(BF16), wbp_ref[...], preferred_element_type=F32)
    y += twice_gate(zmg_ref[:, d:2 * d]) * jnp.dot(mlstm_o.astype(BF16), wbm_ref[...],
                                                   preferred_element_type=F32)
    y += twice_gate(zmg_ref[:, 2 * d:3 * d]) * jnp.dot(sgu_o.astype(BF16), wbs_ref[...],
                                                       preferred_element_type=F32)
    y2 = jnp.dot(y.astype(BF16), wo_ref[...], preferred_element_type=F32)

    g1 = mod_ref[:, 2 * d:3 * d]
    x1 = _ln_plain(alpha * x_ref[...] + g1 * y2) * l1g_ref[...] + l1b_ref[...]
    x1_ref[...] = x1
    h2 = _ln_plain(x1) * (1.0 + mod_ref[:, 4 * d:5 * d]) + mod_ref[:, 3 * d:4 * d]
    h2b = h2.astype(BF16)
    h2_ref[...] = _pack_rows(h2b)

    logits = lax.dot_general(wr_ref[...], h2b, (((1,), (1,)), ((), ())),
                             preferred_element_type=F32) + br_ref[...]
    n_exp = logits.shape[0]
    expert = lax.broadcasted_iota(jnp.int32, logits.shape, 0)
    work = logits
    vals, idxs, hots = [], [], []
    for _ in range(TOP_K):
        mval = jnp.max(work, axis=0, keepdims=True)
        idx = jnp.min(jnp.where(work == mval, expert, n_exp), axis=0, keepdims=True)
        hot = expert == idx
        vals.append(mval)
        idxs.append(idx)
        hots.append(hot)
        work = jnp.where(hot, -jnp.inf, work)
    exps = [jnp.exp(v - vals[0]) for v in vals]
    denom = exps[0] + exps[1] + exps[2] + exps[3]
    sel = (hots[0] | hots[1] | hots[2] | hots[3]).astype(F32)
    r_i = lax.broadcasted_iota(jnp.int32, (tm, tm), 0)
    c_i = lax.broadcasted_iota(jnp.int32, (tm, tm), 1)
    earlier = (r_i < c_i).astype(BF16)
    rank = jnp.dot(sel.astype(BF16), earlier, preferred_element_type=F32) + cnt_sc[:, 0:1]
    rows = ([i_.astype(F32) for i_ in idxs] + [e / denom for e in exps]
            + [jnp.sum(jnp.where(h, rank, 0.0), axis=0, keepdims=True) for h in hots]
            + [jnp.zeros((route_ref.shape[0] - 3 * TOP_K, tm), F32)])
    route_ref[...] = jnp.concatenate(rows, axis=0)
    new_cnt = cnt_sc[:, 0:1] + jnp.sum(sel, axis=1, keepdims=True)
    cnt_sc[...] = jnp.broadcast_to(new_cnt, cnt_sc.shape)
    cnt_out_ref[...] = jnp.broadcast_to(new_cnt, cnt_out_ref.shape)


def _mix(x, zp, zuv, zmg, zo, hf, hb, mods, consts, *, n_rows, n_lat, seq, alpha):
    t, d = x.shape
    tm = ROW_TILE
    n_batch = mods.shape[0] - 1
    row = lambda i: (i, 0)
    const2 = lambda i: (0, 0)
    const3 = lambda i: (0, 0, 0)
    kind = lambda i: jnp.where(i * tm >= n_lat, 1, 0)

    def mod_idx(i):
        return (jnp.minimum((i * tm) // seq, n_batch), 0, 0)

    (pm, cnt, pw, ps, sg, sb, sw, sbm, ng, wbp, wbm, wbs, wo, l1g, l1b, wr, br) = consts
    in_specs = [
        pl.BlockSpec((tm, d), row),
        pl.BlockSpec((tm, zp.shape[1]), row),
        pl.BlockSpec((tm, zuv.shape[1]), row),
        pl.BlockSpec((tm, zmg.shape[1]), row),
        pl.BlockSpec((tm, zo.shape[1]), row),
        pl.BlockSpec((tm, hf.shape[1]), row),
        pl.BlockSpec((tm, hb.shape[1]), row),
        pl.BlockSpec((None, 1, mods.shape[2]), mod_idx),
        pl.BlockSpec((None,) + pm.shape[1:], lambda i: (kind(i), 0, 0, 0)),
        pl.BlockSpec((None,) + cnt.shape[1:], lambda i: (kind(i), 0, 0)),
        pl.BlockSpec(pw.shape, const2),
        pl.BlockSpec(ps.shape, const2),
        pl.BlockSpec(sg.shape, const2),
        pl.BlockSpec(sb.shape, const2),
        pl.BlockSpec(sw.shape, const3),
        pl.BlockSpec(sbm.shape, const2),
        pl.BlockSpec(ng.shape, const2),
        pl.BlockSpec(wbp.shape, const2),
        pl.BlockSpec(wbm.shape, const2),
        pl.BlockSpec(wbs.shape, const2),
        pl.BlockSpec(wo.shape, const2),
        pl.BlockSpec(l1g.shape, const2),
        pl.BlockSpec(l1b.shape, const2),
        pl.BlockSpec(wr.shape, const2),
        pl.BlockSpec(br.shape, const2),
    ]
    return pl.pallas_call(
        functools.partial(_mix_kernel, alpha=alpha),
        grid=(n_rows // tm,),
        in_specs=in_specs,
        out_specs=[
            pl.BlockSpec((tm, d), row),
            pl.BlockSpec((tm, d // 2), row),
            pl.BlockSpec((ROUTE_ROWS, tm), lambda i: (0, i)),
            pl.BlockSpec((wr.shape[0], LANES), const2),
        ],
        out_shape=[
            jax.ShapeDtypeStruct((n_rows, d), F32),
            jax.ShapeDtypeStruct((n_rows, d // 2), jnp.uint32),
            jax.ShapeDtypeStruct((ROUTE_ROWS, n_rows), F32),
            jax.ShapeDtypeStruct((wr.shape[0], LANES), F32),
        ],
        scratch_shapes=[pltpu.VMEM((wr.shape[0], LANES), F32)],
        compiler_params=_cparams("arbitrary"),
    )(x, zp, zuv, zmg, zo, hf, hb, mods, pm, cnt, pw, ps, sg, sb, sw, sbm, ng,
      wbp, wbm, wbs, wo, l1g, l1b, wr, br)


SCATTER_WINDOW = 128
SCATTER_PARTS = 2


def _scatter_rows(x, idx, n_out, part):
    n = x.shape[0]
    w = x.shape[1] // SCATTER_PARTS
    n_idx = idx.shape[0]
    mesh = plsc.VectorSubcoreMesh(core_axis_name="core", subcore_axis_name="subcore")

    @pl.kernel(out_type=jax.ShapeDtypeStruct((n_out, w), x.dtype), mesh=mesh, scratch_types=[])
    def scatter_kernel(x_hbm, i_hbm, o_hbm):
        def body(x_vmem, i_vmem):
            for k in range(n_idx):
                pltpu.sync_copy(x_vmem, o_hbm.at[i_vmem.at[k]])

        pltpu.emit_pipeline(
            body,
            grid=(n // SCATTER_WINDOW,),
            in_specs=[pl.BlockSpec((SCATTER_WINDOW, w), lambda i: (i, part)),
                      pl.BlockSpec((n_idx, SCATTER_WINDOW), lambda i: (0, i))],
            out_specs=[],
            core_axis_name=("core", "subcore"),
            dimension_semantics=(pltpu.PARALLEL,),
        )(x_hbm, i_hbm)

    return scatter_kernel(x, idx)


def _gather_rows(x, idx):
    w = x.shape[1]
    n_idx = idx.shape[1]
    mesh = plsc.VectorSubcoreMesh(core_axis_name="core", subcore_axis_name="subcore")

    @pl.kernel(out_type=jax.ShapeDtypeStruct((n_idx, w), x.dtype), mesh=mesh, scratch_types=[])
    def gather_kernel(x_hbm, i_hbm, o_hbm):
        def body(i_vmem, o_vmem):
            pltpu.sync_copy(x_hbm.at[i_vmem.at[0]], o_vmem)

        pltpu.emit_pipeline(
            body,
            grid=(n_idx // SCATTER_WINDOW,),
            in_specs=[pl.BlockSpec((1, SCATTER_WINDOW), lambda i: (0, i))],
            out_specs=[pl.BlockSpec((SCATTER_WINDOW, w), lambda i: (i, 0))],
            core_axis_name=("core", "subcore"),
            dimension_semantics=(pltpu.PARALLEL,),
        )(i_hbm, o_hbm)

    return gather_kernel(x, idx)


def _expert_kernel(te_ref, nt_ref, xa_ref, xb_ref, wgu_ref, bgu_ref, wd_ref, bd_ref, ya_ref, yb_ref,
                   wgu_sc, wd_sc):
    i = pl.program_id(0)
    f = wd_ref.shape[0]
    used = i < nt_ref[0]

    @pl.when(jnp.logical_and(used, jnp.logical_or(i == 0, te_ref[i] != te_ref[jnp.maximum(i - 1, 0)])))
    def _():
        for c0 in range(0, 2 * f, 512):
            wgu_sc[:, c0:c0 + 512] = wgu_ref[:, c0:c0 + 512].astype(BF16)
        for c0 in range(0, f, 256):
            wd_sc[c0:c0 + 256, :] = wd_ref[c0:c0 + 256, :].astype(BF16)

    @pl.when(used)
    def _():
        x = _unpack_rows([xa_ref[...], xb_ref[...]]).astype(BF16)
        acc = jnp.zeros((x.shape[0], wd_ref.shape[1]), F32)
        for c0 in range(0, f, 512):
            gate = jnp.dot(x, wgu_sc[:, c0:c0 + 512], preferred_element_type=F32) + bgu_ref[:, c0:c0 + 512]
            up = (jnp.dot(x, wgu_sc[:, f + c0:f + c0 + 512], preferred_element_type=F32)
                  + bgu_ref[:, f + c0:f + c0 + 512])
            gate = jnp.minimum(gate, SWIGLU_LIMIT)
            up = jnp.clip(up, -SWIGLU_LIMIT, SWIGLU_LIMIT)
            glu = gate * _sigmoid(SWIGLU_ALPHA * gate)
            act = ((up + 1.0) * glu).astype(BF16)
            acc += jnp.dot(act, wd_sc[c0:c0 + 512, :], preferred_element_type=F32)
        words = _pack_rows(acc + bd_ref[...])
        ya_ref[...] = words[:, :ya_ref.shape[1]]
        yb_ref[...] = words[:, ya_ref.shape[1]:]

    @pl.when(jnp.logical_not(used))
    def _():
        ya_ref[...] = jnp.zeros_like(ya_ref)
        yb_ref[...] = jnp.zeros_like(yb_ref)


def _experts(tile_expert, n_tiles_used, xa, xb, wgu, bgu, wd, bd, *, layer):
    p = xa.shape[0]
    d = 2 * SCATTER_PARTS * xa.shape[1]
    tm = EXPERT_TILE
    f2 = wgu.shape[3]
    grid_spec = pltpu.PrefetchScalarGridSpec(
        num_scalar_prefetch=2,
        grid=(p // tm,),
        in_specs=[
            pl.BlockSpec((tm, xa.shape[1]), lambda i, te, nt: (i, 0)),
            pl.BlockSpec((tm, xb.shape[1]), lambda i, te, nt: (i, 0)),
            pl.BlockSpec((None, None, d, f2), lambda i, te, nt: (layer, te[i], 0, 0)),
            pl.BlockSpec((None, None, 1, f2), lambda i, te, nt: (layer, te[i], 0, 0)),
            pl.BlockSpec((None, None, f2 // 2, d), lambda i, te, nt: (layer, te[i], 0, 0)),
            pl.BlockSpec((None, None, 1, d), lambda i, te, nt: (layer, te[i], 0, 0)),
        ],
        out_specs=[pl.BlockSpec((tm, xa.shape[1]), lambda i, te, nt: (i, 0)),
                   pl.BlockSpec((tm, xb.shape[1]), lambda i, te, nt: (i, 0))],
        scratch_shapes=[pltpu.VMEM((d, f2), BF16), pltpu.VMEM((f2 // 2, d), BF16)],
    )
    return pl.pallas_call(
        _expert_kernel,
        grid_spec=grid_spec,
        out_shape=[jax.ShapeDtypeStruct(xa.shape, xa.dtype), jax.ShapeDtypeStruct(xb.shape, xb.dtype)],
        compiler_params=_cparams("arbitrary"),
    )(tile_expert, n_tiles_used, xa, xb, wgu, bgu, wd, bd)


def _combine_kernel(x1_ref, ya_ref, yb_ref, route_ref, mod_ref, g_ref, b_ref, *rest, alpha):
    x2_ref = rest[-1]
    d = x1_ref.shape[1]
    tm = x1_ref.shape[0]
    route = jnp.concatenate([route_ref[...], jnp.zeros((LANES - ROUTE_ROWS, tm), F32)], axis=0).T
    f = jnp.zeros(x1_ref.shape, F32)
    for kk in range(TOP_K):
        f += route[:, TOP_K + kk:TOP_K + kk + 1] * _unpack_rows([ya_ref[kk], yb_ref[kk]])
    g2 = mod_ref[:, 5 * d:6 * d]
    x2_ref[...] = _ln_plain(alpha * x1_ref[...] + g2 * f) * g_ref[...] + b_ref[...]


def _combine(x1, yg_a, yg_b, route, mods, g, b, *, seq, alpha, dest=None, dest_rows=None, tile_offset=0):
    t, d = x1.shape
    tm = COMBINE_TILE
    assert t % tm == 0 and seq % tm == 0
    n_batch = mods.shape[0] - 1
    row = lambda i: (i, 0)

    def mod_idx(i):
        return (jnp.minimum((i * tm) // seq, n_batch), 0, 0)

    in_specs = [
        pl.BlockSpec((tm, d), row),
        pl.BlockSpec((TOP_K, tm, yg_a.shape[2]), lambda i: (0, i, 0)),
        pl.BlockSpec((TOP_K, tm, yg_b.shape[2]), lambda i: (0, i, 0)),
        pl.BlockSpec((ROUTE_ROWS, tm), lambda i: (0, i)),
        pl.BlockSpec((None, 1, mods.shape[2]), mod_idx),
        pl.BlockSpec(g.shape, lambda i: (0, 0)),
        pl.BlockSpec(b.shape, lambda i: (0, 0)),
    ]
    args = [x1, yg_a, yg_b, route, mods, g, b]
    aliases = {}
    if dest is not None:
        in_specs.append(pl.BlockSpec(memory_space=pl.ANY))
        args.append(dest)
        aliases = {len(args) - 1: 0}
    return pl.pallas_call(
        functools.partial(_combine_kernel, alpha=alpha),
        grid=(t // tm,),
        in_specs=in_specs,
        out_specs=pl.BlockSpec((tm, d), lambda i: (i + tile_offset, 0)),
        out_shape=jax.ShapeDtypeStruct((t if dest_rows is None else dest_rows, d), F32),
        input_output_aliases=aliases,
        compiler_params=_cparams("parallel"),
    )(*args)


def kernel(x, c, ctx, c_ctx, w_ada, b_ada, w_in, b_in, pool_w, pool_scale, qk_conv_w, qk_conv_b,
           mlstm_norm_g, sgu_ln_g, sgu_ln_b, sgu_w, sgu_b, w_br_pool, w_br_mlstm, w_br_sgu, w_out,
           ln1_g, ln1_b, w_router, b_router, w_gate_up, b_gate_up, w_down, b_down, ln2_g, ln2_b):
    n_batch, seq, d = x.shape
    ctx_len = ctx.shape[1]
    depth = w_in.shape[0]
    n_exp = w_router.shape[2]
    assert ctx_len == ROW_TILE and seq % ROW_TILE == 0 and ROW_TILE % GRID_W == 0
    assert n_batch % N_STREAMS == 0
    alpha = float((2 * depth) ** 0.25)

    rows = ((n_batch + 1 + 7) // 8) * 8
    cc = jnp.zeros((rows, d), F32).at[:n_batch].set(c).at[n_batch].set(c_ctx)
    mods_all = _ada_table(cc, w_ada, b_ada)[:, :n_batch + 1].reshape(depth, n_batch + 1, 1, 6 * d)

    pool_w_ = len(POOL_WINDOWS) * POOL_GROUP_DIM
    mlw = N_HEADS * HEAD_DIM
    col = lambda a, lo, hi: a[..., lo:hi]
    g_lo = pool_w_ + 4 * mlw
    uv_hi = g_lo + _N_GATE_COLS + 2 * pool_w_
    parts = ((pool_w_, pool_w_ + 2 * mlw), (pool_w_ + 2 * mlw, pool_w_ + 3 * mlw), (pool_w_ + 3 * mlw, g_lo),
             (g_lo + _N_GATE_COLS, uv_hi), (0, pool_w_), (uv_hi, w_in.shape[2]))
    part_scale = (1.0, 1.0, 0.5, 1.0, 1.0, 0.5)
    w_main = jnp.concatenate([col(w_in, lo, hi) * sc for (lo, hi), sc in zip(parts, part_scale)],
                             axis=-1).astype(BF16)
    b_main = jnp.concatenate([col(b_in, lo, hi) * sc for (lo, hi), sc in zip(parts, part_scale)],
                             axis=-1)[:, None, :]
    gw, gb = col(w_in, g_lo, g_lo + _N_GATE_COLS), col(b_in, g_lo, g_lo + _N_GATE_COLS)
    nh = N_HEADS
    pick = lambda a, k: jnp.concatenate([a[..., k * nh:(k + 1) * nh], a[..., (k + 2) * nh:(k + 3) * nh]], axis=-1)
    w_gate_t = jnp.swapaxes(jnp.concatenate([pick(gw, 0), pick(gw, 1)], axis=-1), 1, 2).astype(BF16)
    b_gate_t = jnp.concatenate([pick(gb, 0), pick(gb, 1)], axis=-1)[:, :, None]

    pm, cnt = _pool_constants(ctx_len)
    eye = jnp.eye(len(POOL_WINDOWS), dtype=F32)
    pool_bd = jnp.einsum('lgcd,gh->lgchd', pool_w, eye).reshape(depth, pool_w_, pool_w_).astype(BF16)
    sgu_bias_map = jnp.repeat(jnp.swapaxes(sgu_b, 1, 2), sgu_ln_g.shape[1] // sgu_b.shape[1], axis=2)
    w_router_t = jnp.swapaxes(w_router, 1, 2).astype(BF16)
    b_router_t = b_router[:, :, None]
    r1 = lambda a: a[:, None, :]
    b_gu4 = b_gate_up[:, :, None, :]
    b_d4 = b_down[:, :, None, :]

    nb = n_batch // N_STREAMS
    n_lat = nb * seq
    t_all = n_lat + nb * ctx_len
    streams, stream_mods = [], []
    for h in range(N_STREAMS):
        sl = slice(h * nb, (h + 1) * nb)
        streams.append(jnp.concatenate([x[sl].reshape(n_lat, d), ctx[sl].reshape(nb * ctx_len, d)], axis=0))
        stream_mods.append(jnp.concatenate([mods_all[:, sl], mods_all[:, n_batch:n_batch + 1]], axis=1))
    out = None
    for i in range(depth):
        last = i == depth - 1
        for h in range(N_STREAMS):
            stream = streams[h]
            mods = stream_mods[h][i]
            zqk, zv, zo, zuv, zp, zmg, g, gt = _inproj(
                stream, mods, w_main[i], b_main[i], w_gate_t[i], b_gate_t[i],
                qk_conv_w[i], r1(qk_conv_b)[i], n_lat=n_lat, seq=seq, ctx_len=ctx_len)
            hf, hb = _mlstm(zqk, zv, g, gt, n_batch=nb, seq=seq, ctx_len=ctx_len)
            n_rows = n_lat if last else t_all
            consts = (pm, cnt, pool_bd[i], r1(pool_scale)[i], r1(sgu_ln_g)[i], r1(sgu_ln_b)[i],
                      sgu_w[i].astype(BF16), sgu_bias_map[i], r1(0.5 * mlstm_norm_g)[i],
                      w_br_pool[i].astype(BF16), w_br_mlstm[i].astype(BF16), w_br_sgu[i].astype(BF16),
                      (0.5 * w_out[i]).astype(BF16), r1(ln1_g)[i], r1(ln1_b)[i], w_router_t[i], b_router_t[i])
            x1, h2, route, counts = _mix(stream, zp, zuv, zmg, zo, hf, hb, mods, consts,
                                         n_rows=n_rows, n_lat=n_lat, seq=seq, alpha=alpha)

            cnt_e = counts[:, 0].astype(jnp.int32)
            padded = ((cnt_e + EXPERT_TILE - 1) // EXPERT_TILE) * EXPERT_TILE
            ends = jnp.cumsum(padded)
            starts = ends - padded
            e_idx = route[0:TOP_K].astype(jnp.int32)
            rank = route[2 * TOP_K:3 * TOP_K].astype(jnp.int32)
            start_of = jnp.sum(jnp.where(e_idx[..., None] == jnp.arange(n_exp, dtype=jnp.int32), starts, 0),
                               axis=-1)
            pos_t = start_of + rank
            p_rows = n_rows * TOP_K + n_exp * EXPERT_TILE
            n_tiles = p_rows // EXPERT_TILE
            tile_start = jnp.arange(n_tiles, dtype=jnp.int32) * EXPERT_TILE
            tile_expert = jnp.minimum(
                jnp.sum((ends[None, :] <= tile_start[:, None]).astype(jnp.int32), axis=1), n_exp - 1)
            n_used = (ends[-1] // EXPERT_TILE).astype(jnp.int32).reshape(1)
            xs_a = _scatter_rows(h2, pos_t, p_rows, 0)
            xs_b = _scatter_rows(h2, pos_t, p_rows, 1)
            ys_a, ys_b = _experts(tile_expert, n_used, xs_a, xs_b, w_gate_up, b_gu4, w_down, b_d4, layer=i)
            pos_flat = pos_t.reshape(1, TOP_K * n_rows)
            yg_a = _gather_rows(ys_a, pos_flat).reshape(TOP_K, n_rows, -1)
            yg_b = _gather_rows(ys_b, pos_flat).reshape(TOP_K, n_rows, -1)
            if last:
                out = _combine(x1, yg_a, yg_b, route, mods, r1(ln2_g)[i], r1(ln2_b)[i], seq=seq, alpha=alpha,
                               dest=out, dest_rows=n_batch * seq, tile_offset=h * n_lat // COMBINE_TILE)
            else:
                streams[h] = _combine(x1, yg_a, yg_b, route, mods, r1(ln2_g)[i], r1(ln2_b)[i],
                                      seq=seq, alpha=alpha)
    return out.reshape(n_batch, seq, d)
```

```python
import functools

import numpy as np
import jax
import jax.numpy as jnp
from jax import lax
from jax.experimental import pallas as pl
from jax.experimental.pallas import tpu as pltpu
from jax.experimental.pallas import tpu_sc as plsc

F32 = jnp.float32
BF16 = jnp.bfloat16

GRID_W = 64
POOL_WINDOWS = (2, 4, 8, 16)
POOL_GROUP_DIM = 64
N_HEADS = 4
HEAD_DIM = 128
CHUNK = 128
TOP_K = 4
SWIGLU_LIMIT = 7.0
SWIGLU_ALPHA = 1.702
LN_EPS = 1e-5
HEAD_NORM_EPS = 1e-6

ROW_TILE = 256
INPROJ_TILE = 1024
COMBINE_TILE = 1024
SEAM_ROWS = 16
ROUTE_ROWS = 16
EXPERT_TILE = 512
N_STREAMS = 2
SUBLANES = 8
LANES = 128
VMEM_LIMIT = 56 * 1024 * 1024


def _cparams(*sem):
    return pltpu.CompilerParams(dimension_semantics=sem, vmem_limit_bytes=VMEM_LIMIT)


def _ln_plain(x):
    mu = jnp.mean(x, axis=-1, keepdims=True)
    xc = x - mu
    var = jnp.mean(xc * xc, axis=-1, keepdims=True)
    return xc * lax.rsqrt(var + LN_EPS)


def _sigmoid(x):
    return 0.5 * jnp.tanh(0.5 * x) + 0.5


def _log_sigmoid(x):
    return jnp.minimum(x, 0.0) - jnp.log1p(jnp.exp(-jnp.abs(x)))


def _pack_rows(x):
    half = x.shape[1] // 2
    bits = lax.bitcast_convert_type(x.astype(BF16).astype(F32), jnp.uint32)
    return (bits[:, half:] & jnp.uint32(0xFFFF0000)) | (bits[:, :half] >> 16)


def _unpack_rows(parts):
    low = [lax.bitcast_convert_type(w << 16, F32) for w in parts]
    high = [lax.bitcast_convert_type(w & jnp.uint32(0xFFFF0000), F32) for w in parts]
    return jnp.concatenate(low + high, axis=1)


def _ada_kernel(c_ref, w_ref, b_ref, o_ref):
    c = c_ref[...]
    s = c * _sigmoid(c)
    o_ref[...] = jnp.dot(s, w_ref[...], preferred_element_type=F32,
                         precision=lax.Precision.HIGHEST) + b_ref[...]


def _ada_table(cc, w_ada, b_ada):
    depth, d, n = w_ada.shape
    rows = cc.shape[0]
    bn = 1536
    return pl.pallas_call(
        _ada_kernel,
        grid=(depth, n // bn),
        in_specs=[
            pl.BlockSpec((rows, d), lambda l, j: (0, 0)),
            pl.BlockSpec((None, d, bn), lambda l, j: (l, 0, j)),
            pl.BlockSpec((None, 1, bn), lambda l, j: (l, 0, j)),
        ],
        out_specs=pl.BlockSpec((None, rows, bn), lambda l, j: (l, 0, j)),
        out_shape=jax.ShapeDtypeStruct((depth, rows, n), F32),
        compiler_params=_cparams("parallel", "parallel"),
    )(cc, w_ada, b_ada.reshape(depth, 1, n))


_QK_WIDTH = 2 * N_HEADS * HEAD_DIM
_Z_PARTS = (("v", 512), ("o", 512), ("uv", 512), ("pool", 256), ("mg", 3072))
_N_GATE_COLS = 4 * N_HEADS


def _inproj_kernel(x_ref, xp_ref, xn_ref, mod_ref, w_ref, b_ref, wgt_ref, bgt_ref,
                   cw_ref, cb_ref, qk_ref, v_ref, o_ref, uv_ref, p_ref, mg_ref, g_ref, gt_ref, pad_sc,
                   *, n_lat, seq, ctx_len):
    tm, d = x_ref.shape
    i = pl.program_id(0)
    is_lat = i * tm < n_lat
    has_prev = jnp.logical_and(is_lat, (i * tm) % seq != 0).astype(F32)
    has_next = jnp.logical_and(is_lat, ((i + 1) * tm) % seq != 0).astype(F32)

    xe = jnp.concatenate([xp_ref[...], x_ref[...], xn_ref[...]], axis=0)
    he = (_ln_plain(xe) * (1.0 + mod_ref[:, d:2 * d]) + mod_ref[:, 0:d]).astype(BF16)
    hb = he[SUBLANES:SUBLANES + tm, :]

    half = _QK_WIDTH // 2

    def qk_project(part):
        c0 = part * half
        z = jnp.dot(he, w_ref[:, c0:c0 + half], preferred_element_type=F32) + b_ref[:, c0:c0 + half]
        pad_sc[part] = z
        pad_sc[part, SUBLANES - 1:SUBLANES, :] = pad_sc[part, SUBLANES - 1:SUBLANES, :] * has_prev
        pad_sc[part, SUBLANES + tm:SUBLANES + tm + 1, :] = (
            pad_sc[part, SUBLANES + tm:SUBLANES + tm + 1, :] * has_next)

    def conv_act(part, prev, cur, nxt):
        c0 = part * half
        y = (cb_ref[:, c0:c0 + half] + prev * cw_ref[0:1, c0:c0 + half]
             + cur * cw_ref[1:2, c0:c0 + half] + nxt * cw_ref[2:3, c0:c0 + half])
        act = y * _sigmoid(y)
        if part == 1:
            act = act * (HEAD_DIM ** -0.5)
        return act.astype(qk_ref.dtype)

    def qk_conv(part):
        qk_ref[:, part * half:(part + 1) * half] = conv_act(
            part, pad_sc[part, SUBLANES - 1:SUBLANES - 1 + tm, :], pad_sc[part, SUBLANES:SUBLANES + tm, :],
            pad_sc[part, SUBLANES + 1:SUBLANES + 1 + tm, :])

    def plain(ref, off, c0, cw):
        z = jnp.dot(hb, w_ref[:, off + c0:off + c0 + cw], preferred_element_type=F32)
        ref[:, c0:c0 + cw] = (z + b_ref[:, off + c0:off + c0 + cw]).astype(ref.dtype)

    mxu_tasks = []
    off = _QK_WIDTH
    for ref, (_, width) in zip((v_ref, o_ref, uv_ref, p_ref, mg_ref), _Z_PARTS):
        for c0 in range(0, width, 512):
            mxu_tasks.append(functools.partial(plain, ref, off, c0, min(512, width - c0)))
        off += width
    vpu_tasks = {2: functools.partial(_gate_prep, hb, wgt_ref, bgt_ref, g_ref, gt_ref),
                 4: functools.partial(qk_conv, 0), 6: functools.partial(qk_conv, 1)}
    qk_project(0)
    qk_project(1)
    for n, task in enumerate(mxu_tasks):
        if n in vpu_tasks:
            vpu_tasks[n]()
        task()

    if tm > ctx_len:
        @pl.when(jnp.logical_not(is_lat))
        def _():
            r = lax.broadcasted_iota(jnp.int32, (2 * SEAM_ROWS, 1), 0)
            for part in range(2):
                for p in range(ctx_len, tm, ctx_len):
                    lo = SUBLANES + p - SEAM_ROWS
                    prev = jnp.where(r == SEAM_ROWS, 0.0, pad_sc[part, lo - 1:lo - 1 + 2 * SEAM_ROWS, :])
                    nxt = jnp.where(r == SEAM_ROWS - 1, 0.0, pad_sc[part, lo + 1:lo + 1 + 2 * SEAM_ROWS, :])
                    qk_ref[p - SEAM_ROWS:p + SEAM_ROWS, part * half:(part + 1) * half] = conv_act(
                        part, prev, pad_sc[part, lo:lo + 2 * SEAM_ROWS, :], nxt)


def _gate_prep(hb, wgt_ref, bgt_ref, g_ref, gt_ref):
    tm = hb.shape[0]
    nd = 2 * N_HEADS
    ggt = lax.dot_general(wgt_ref[...], hb, (((1,), (1,)), ((), ())),
                          preferred_element_type=F32) + bgt_ref[...]
    git, gft = ggt[:nd, :], ggt[nd:, :]
    fwd = lax.broadcasted_iota(jnp.int32, (nd, tm), 0) < N_HEADS
    pos = lax.broadcasted_iota(jnp.int32, (nd, tm), 1) % CHUNK
    lsg = _log_sigmoid(gft)
    bcum = jnp.where(fwd, _chunk_scan(lsg, pos, 1, False, False), _chunk_scan(lsg, pos, 1, True, False))
    c = git - bcum
    cmax = jnp.where(fwd, _chunk_scan(c, pos, 1, False, True), _chunk_scan(c, pos, 1, True, True))
    gt_ref[...] = c
    rows = jnp.concatenate([bcum, c, bcum + cmax, cmax, jnp.zeros((LANES - 4 * nd, tm), F32)], axis=0)
    g_ref[...] = rows.T


def _chunk_scan(a, pos, axis, reverse, use_max):
    n = a.shape[axis]
    ident = -jnp.inf if use_max else 0.0
    sh = 1
    while sh < CHUNK:
        if reverse:
            moved = jnp.where(pos < CHUNK - sh, pltpu.roll(a, n - sh, axis=axis), ident)
        else:
            moved = jnp.where(pos >= sh, pltpu.roll(a, sh, axis=axis), ident)
        a = jnp.maximum(a, moved) if use_max else a + moved
        sh *= 2
    return a


def _inproj(x, mods, w, b, wgt, bgt, conv_w, conv_b, *, n_lat, seq, ctx_len):
    t, d = x.shape
    tm = INPROJ_TILE
    assert seq % tm == 0 and n_lat % tm == 0 and (t - n_lat) % tm == 0 and tm % ctx_len == 0
    n_batch = mods.shape[0] - 1
    per_tile = tm // SUBLANES

    def mod_idx(i):
        return (jnp.minimum((i * tm) // seq, n_batch), 0, 0)

    const = lambda i: (0, 0)
    row = lambda i: (i, 0)
    parts = (("qk", _QK_WIDTH),) + _Z_PARTS
    out_shapes = [jax.ShapeDtypeStruct((t, width), BF16) for _, width in parts]
    out_specs = [pl.BlockSpec((tm, width), row) for _, width in parts]
    out_shapes += [jax.ShapeDtypeStruct((t, LANES), F32), jax.ShapeDtypeStruct((2 * N_HEADS, t), F32)]
    out_specs += [pl.BlockSpec((tm, LANES), row), pl.BlockSpec((2 * N_HEADS, tm), lambda i: (0, i))]
    return pl.pallas_call(
        functools.partial(_inproj_kernel, n_lat=n_lat, seq=seq, ctx_len=ctx_len),
        grid=(t // tm,),
        in_specs=[
            pl.BlockSpec((tm, d), row),
            pl.BlockSpec((SUBLANES, d), lambda i: (jnp.maximum(i * per_tile - 1, 0), 0)),
            pl.BlockSpec((SUBLANES, d), lambda i: (jnp.minimum((i + 1) * per_tile, t // SUBLANES - 1), 0)),
            pl.BlockSpec((None, 1, mods.shape[2]), mod_idx),
            pl.BlockSpec(w.shape, const, pipeline_mode=pl.Buffered(1)),
            pl.BlockSpec(b.shape, const),
            pl.BlockSpec(wgt.shape, const),
            pl.BlockSpec(bgt.shape, const),
            pl.BlockSpec(conv_w.shape, const),
            pl.BlockSpec(conv_b.shape, const),
        ],
        out_specs=out_specs,
        out_shape=out_shapes,
        scratch_shapes=[pltpu.VMEM((2, tm + 2 * SUBLANES, _QK_WIDTH // 2), F32)],
        compiler_params=_cparams("parallel"),
    )(x, x, x, mods, w, b, wgt, bgt, conv_w, conv_b)


def _mlstm_row_block(j, b, direction, *, n_batch, seq, ctx_len):
    ncc = ctx_len // CHUNK
    ncl = seq // CHUNK
    is_ctx = j < ncc
    if direction == 0:
        c = jnp.where(is_ctx, j, j - ncc)
    else:
        c = jnp.where(is_ctx, ncc - 1 - j, ncl - 1 - (j - ncc))
    base = jnp.where(is_ctx, (n_batch * seq + b * ctx_len) // CHUNK, (b * seq) // CHUNK)
    return base + c


def _mlstm_kernel(q_f, k_f, v_f, g_f, gt_f, q_b, k_b, v_b, g_b, gt_b, hf_ref, hb_ref, s_sc, m_sc):
    j = pl.program_id(1)

    @pl.when(j == 0)
    def _():
        s_sc[...] = jnp.zeros_like(s_sc)
        m_sc[...] = jnp.zeros_like(m_sc)

    ri = lax.broadcasted_iota(jnp.int32, (CHUNK, CHUNK), 0)
    ci = lax.broadcasted_iota(jnp.int32, (CHUNK, CHUNK), 1)
    ones_tile = jnp.ones((CHUNK, HEAD_DIM), BF16)
    dirs = ((q_f, k_f, v_f, g_f, gt_f, hf_ref), (q_b, k_b, v_b, g_b, gt_b, hb_ref))
    combos = [(d, hh) for d in range(2) for hh in range(N_HEADS)]
    heads = lambda ref, hh: ref[:, hh * HEAD_DIM:(hh + 1) * HEAD_DIM]

    nd = 2 * N_HEADS
    tri = ((ci <= ri), (ci >= ri))
    g = (g_f[...], g_b[...])
    gt = (gt_f[...], gt_b[...])

    qk_s, v1, state, m_prev = {}, {}, {}, {}
    for d, hh in combos:
        q_ref, k_ref, v_ref = dirs[d][:3]
        qk_s[d, hh] = lax.dot_general(heads(q_ref, hh), heads(k_ref, hh), (((1,), (1,)), ((), ())),
                                      preferred_element_type=F32)
        v1[d, hh] = jnp.concatenate([heads(v_ref, hh), ones_tile], axis=1)
        state[d, hh] = s_sc[d, hh]
        m_prev[d, hh] = m_sc[d * N_HEADS + hh][0:1, 0:1]

    q_state = {}
    for d, hh in combos:
        q_state[d, hh] = jnp.dot(heads(dirs[d][0], hh), state[d, hh].astype(BF16), preferred_element_type=F32)

    m_t, decay, scores = {}, {}, {}
    for d, hh in combos:
        col = d * N_HEADS + hh
        bcum = g[d][:, col:col + 1]
        inter = bcum + m_prev[d, hh]
        m_t[d, hh] = jnp.maximum(inter, g[d][:, 2 * nd + col:2 * nd + col + 1])
        decay[d, hh] = jnp.exp(inter - m_t[d, hh])
        dlog = jnp.where(tri[d], (bcum - m_t[d, hh]) + gt[d][col:col + 1, :], -jnp.inf)
        scores[d, hh] = (qk_s[d, hh] * jnp.exp(dlog)).astype(BF16)

    for d, hh in combos:
        numden = decay[d, hh] * q_state[d, hh] + jnp.dot(scores[d, hh], v1[d, hh], preferred_element_type=F32)
        num = numden[:, :HEAD_DIM]
        den = numden[:, HEAD_DIM:]
        out = num / jnp.maximum(jnp.abs(den), jnp.exp(-m_t[d, hh]))
        out_ref = dirs[d][5]
        out_ref[:, hh * HEAD_DIM:(hh + 1) * HEAD_DIM] = out.astype(out_ref.dtype)

    kw, a, m_new = {}, {}, {}
    for d, hh in combos:
        col = d * N_HEADS + hh
        last = CHUNK - 1 if d == 0 else 0
        total = g[d][last:last + 1, col:col + 1]
        m_new[d, hh] = total + jnp.maximum(m_prev[d, hh], g[d][last:last + 1, 3 * nd + col:3 * nd + col + 1])
        a[d, hh] = jnp.exp(total + m_prev[d, hh] - m_new[d, hh])
        wk = jnp.exp(total + g[d][:, nd + col:nd + col + 1] - m_new[d, hh])
        kw[d, hh] = (heads(dirs[d][1], hh).astype(F32) * wk).astype(BF16)

    for d, hh in combos:
        upd = lax.dot_general(kw[d, hh], v1[d, hh], (((0,), (0,)), ((), ())), preferred_element_type=F32)
        s_sc[d, hh] = a[d, hh] * state[d, hh] + upd
        m_sc[d * N_HEADS + hh] = jnp.broadcast_to(m_new[d, hh], m_sc.shape[1:])


def _mlstm(zqk, zv, g, gt, *, n_batch, seq, ctx_len):
    t = zqk.shape[0]
    nsteps = (seq + ctx_len) // CHUNK
    width = N_HEADS * HEAD_DIM
    idx = functools.partial(_mlstm_row_block, n_batch=n_batch, seq=seq, ctx_len=ctx_len)

    def specs(d):
        rb = lambda b, j: idx(j, b, d)
        return [
            pl.BlockSpec((CHUNK, width), lambda b, j: (rb(b, j), 0)),
            pl.BlockSpec((CHUNK, width), lambda b, j: (rb(b, j), 1)),
            pl.BlockSpec((CHUNK, width), lambda b, j: (rb(b, j), 0)),
            pl.BlockSpec((CHUNK, LANES), lambda b, j: (rb(b, j), 0)),
            pl.BlockSpec((2 * N_HEADS, CHUNK), lambda b, j: (0, rb(b, j))),
        ]

    out_spec = lambda d: pl.BlockSpec((CHUNK, width), lambda b, j: (idx(j, b, d), 0))
    return pl.pallas_call(
        _mlstm_kernel,
        grid=(n_batch, nsteps),
        in_specs=specs(0) + specs(1),
        out_specs=[out_spec(0), out_spec(1)],
        out_shape=[jax.ShapeDtypeStruct((t, width), BF16)] * 2,
        scratch_shapes=[
            pltpu.VMEM((2, N_HEADS, HEAD_DIM, 2 * HEAD_DIM), F32),
            pltpu.VMEM((2 * N_HEADS, 8, LANES), F32),
        ],
        compiler_params=_cparams("parallel", "arbitrary"),
    )(zqk, zqk, zv, g, gt, zqk, zqk, zv, g, gt)


def _pool_constants(ctx_len):
    tm = ROW_TILE
    masks = np.zeros((2, len(POOL_WINDOWS), tm, tm), np.float32)
    cnts = np.zeros((2, tm, len(POOL_WINDOWS) * POOL_GROUP_DIM), np.float32)
    for kind, length in enumerate((GRID_W, ctx_len)):
        for gi, w in enumerate(POOL_WINDOWS):
            for t in range(tm):
                base = (t // length) * length
                tl = t - base
                lo = min(max(tl - w // 2, 0), length)
                hi = min(max(tl + w // 2, 0), length)
                masks[kind, gi, t, base + lo:base + hi] = 1.0
                cnts[kind, t, gi * POOL_GROUP_DIM:(gi + 1) * POOL_GROUP_DIM] = hi - lo
    return jnp.asarray(masks, BF16), jnp.asarray(cnts, F32)


def _mix_kernel(x_ref, zp_ref, zuv_ref, zmg_ref, zo_ref, hf_ref, hb_ref, mod_ref,
                pm_ref, cnt_ref, pw_ref, ps_ref, sg_ref, sb_ref, sw_ref, sbm_ref, ng_ref,
                wbp_ref, wbm_ref, wbs_ref, wo_ref, l1g_ref, l1b_ref, wr_ref, br_ref,
                x1_ref, h2_ref, route_ref, cnt_out_ref, cnt_sc, *, alpha):
    tm, d = x_ref.shape
    i = pl.program_id(0)

    @pl.when(i == 0)
    def _():
        cnt_sc[...] = jnp.zeros_like(cnt_sc)

    lane256 = lax.broadcasted_iota(jnp.int32, (1, 4 * POOL_GROUP_DIM), 1) // POOL_GROUP_DIM

    xp = zp_ref[...]
    xp_f = xp.astype(F32)
    pooled = jnp.zeros(xp_f.shape, F32)
    for gi in range(len(POOL_WINDOWS)):
        s = jnp.dot(pm_ref[gi], xp, preferred_element_type=F32)
        pooled = jnp.where(lane256 == gi, s, pooled)
    diff = pooled / cnt_ref[...] - xp_f
    pool_o = jnp.dot(diff.astype(BF16), pw_ref[...], preferred_element_type=F32) * ps_ref[...]

    uv = zuv_ref[...].astype(F32)
    uv = 0.5 * uv * (1.0 + lax.erf(uv * (2.0 ** -0.5)))
    half = uv.shape[1] // 2
    u = uv[:, :half]
    vn = (_ln_plain(uv[:, half:]) * sg_ref[...] + sb_ref[...]).astype(BF16)
    mixed_chunks = []
    for c0 in range(0, tm, CHUNK):
        vc = vn[c0:c0 + CHUNK, :]
        mixed = jnp.zeros((CHUNK, half), F32)
        for gi in range(sw_ref.shape[0]):
            s = jnp.dot(sw_ref[gi], vc, preferred_element_type=F32)
            mixed = jnp.where(lane256 == gi, s, mixed)
        mixed_chunks.append(mixed + sbm_ref[...])
    sgu_o = u * jnp.concatenate(mixed_chunks, axis=0)

    hsum = hf_ref[...].astype(F32) + hb_ref[...].astype(F32)
    heads = []
    for hh in range(N_HEADS):
        hv = hsum[:, hh * HEAD_DIM:(hh + 1) * HEAD_DIM]
        mu = jnp.mean(hv, axis=-1, keepdims=True)
        hc = hv - mu
        var = jnp.mean(hc * hc, axis=-1, keepdims=True)
        heads.append(hc * lax.rsqrt(var + HEAD_NORM_EPS))
    hn = jnp.concatenate(heads, axis=1)
    twice_gate = lambda zh: jnp.tanh(zh.astype(F32)) + 1.0
    mlstm_o = hn * ng_ref[...] * twice_gate(zo_ref[...])

    y = twice_gate(zmg_ref[:, 0:d]) * jnp.dot(pool_o.astype(BF16), wbp_ref[...], preferred_element_type=F32)
    y += twice_gate(zmg_ref[:, d:2 * d]) * jnp.dot(mlstm_o.astype(BF16), wbm_ref[...],
                                                   preferred_element_type=F32)
    y += twice_gate(zmg_ref[:, 2 * d:3 * d]) * jnp.dot(sgu_o.astype(BF16), wbs_ref[...],
                                                       preferred_element_type=F32)
    y2 = jnp.dot(y.astype(BF16), wo_ref[...], preferred_element_type=F32)

    g1 = mod_ref[:, 2 * d:3 * d]
    x1 = _ln_plain(alpha * x_ref[...] + g1 * y2) * l1g_ref[...] + l1b_ref[...]
    x1_ref[...] = x1
    h2 = _ln_plain(x1) * (1.0 + mod_ref[:, 4 * d:5 * d]) + mod_ref[:, 3 * d:4 * d]
    h2b = h2.astype(BF16)
    h2_ref[...] = _pack_rows(h2b)

    logits = lax.dot_general(wr_ref[...], h2b, (((1,), (1,)), ((), ())),
                             preferred_element_type=F32) + br_ref[...]
    n_exp = logits.shape[0]
    expert = lax.broadcasted_iota(jnp.int32, logits.shape, 0)
    work = logits
    vals, idxs, hots = [], [], []
    for _ in range(TOP_K):
        mval = jnp.max(work, axis=0, keepdims=True)
        idx = jnp.min(jnp.where(work == mval, expert, n_exp), axis=0, keepdims=True)
        hot = expert == idx
        vals.append(mval)
        idxs.append(idx)
        hots.append(hot)
        work = jnp.where(hot, -jnp.inf, work)
    exps = [jnp.exp(v - vals[0]) for v in vals]
    denom = exps[0] + exps[1] + exps[2] + exps[3]
    sel = (hots[0] | hots[1] | hots[2] | hots[3]).astype(F32)
    r_i = lax.broadcasted_iota(jnp.int32, (tm, tm), 0)
    c_i = lax.broadcasted_iota(jnp.int32, (tm, tm), 1)
    earlier = (r_i < c_i).astype(BF16)
    rank = jnp.dot(sel.astype(BF16), earlier, preferred_element_type=F32) + cnt_sc[:, 0:1]
    rows = ([i_.astype(F32) for i_ in idxs] + [e / denom for e in exps]
            + [jnp.sum(jnp.where(h, rank, 0.0), axis=0, keepdims=True) for h in hots]
            + [jnp.zeros((route_ref.shape[0] - 3 * TOP_K, tm), F32)])
    route_ref[...] = jnp.concatenate(rows, axis=0)
    new_cnt = cnt_sc[:, 0:1] + jnp.sum(sel, axis=1, keepdims=True)
    cnt_sc[...] = jnp.broadcast_to(new_cnt, cnt_sc.shape)
    cnt_out_ref[...] = jnp.broadcast_to(new_cnt, cnt_out_ref.shape)


def _mix(x, zp, zuv, zmg, zo, hf, hb, mods, consts, *, n_rows, n_lat, seq, alpha):
    t, d = x.shape
    tm = ROW_TILE
    n_batch = mods.shape[0] - 1
    row = lambda i: (i, 0)
    const2 = lambda i: (0, 0)
    const3 = lambda i: (0, 0, 0)
    kind = lambda i: jnp.where(i * tm >= n_lat, 1, 0)

    def mod_idx(i):
        return (jnp.minimum((i * tm) // seq, n_batch), 0, 0)

    (pm, cnt, pw, ps, sg, sb, sw, sbm, ng, wbp, wbm, wbs, wo, l1g, l1b, wr, br) = consts
    in_specs = [
        pl.BlockSpec((tm, d), row),
        pl.BlockSpec((tm, zp.shape[1]), row),
        pl.BlockSpec((tm, zuv.shape[1]), row),
        pl.BlockSpec((tm, zmg.shape[1]), row),
        pl.BlockSpec((tm, zo.shape[1]), row),
        pl.BlockSpec((tm, hf.shape[1]), row),
        pl.BlockSpec((tm, hb.shape[1]), row),
        pl.BlockSpec((None, 1, mods.shape[2]), mod_idx),
        pl.BlockSpec((None,) + pm.shape[1:], lambda i: (kind(i), 0, 0, 0)),
        pl.BlockSpec((None,) + cnt.shape[1:], lambda i: (kind(i), 0, 0)),
        pl.BlockSpec(pw.shape, const2),
        pl.BlockSpec(ps.shape, const2),
        pl.BlockSpec(sg.shape, const2),
        pl.BlockSpec(sb.shape, const2),
        pl.BlockSpec(sw.shape, const3),
        pl.BlockSpec(sbm.shape, const2),
        pl.BlockSpec(ng.shape, const2),
        pl.BlockSpec(wbp.shape, const2),
        pl.BlockSpec(wbm.shape, const2),
        pl.BlockSpec(wbs.shape, const2),
        pl.BlockSpec(wo.shape, const2),
        pl.BlockSpec(l1g.shape, const2),
        pl.BlockSpec(l1b.shape, const2),
        pl.BlockSpec(wr.shape, const2),
        pl.BlockSpec(br.shape, const2),
    ]
    return pl.pallas_call(
        functools.partial(_mix_kernel, alpha=alpha),
        grid=(n_rows // tm,),
        in_specs=in_specs,
        out_specs=[
            pl.BlockSpec((tm, d), row),
            pl.BlockSpec((tm, d // 2), row),
            pl.BlockSpec((ROUTE_ROWS, tm), lambda i: (0, i)),
            pl.BlockSpec((wr.shape[0], LANES), const2),
        ],
        out_shape=[
            jax.ShapeDtypeStruct((n_rows, d), F32),
            jax.ShapeDtypeStruct((n_rows, d // 2), jnp.uint32),
            jax.ShapeDtypeStruct((ROUTE_ROWS, n_rows), F32),
            jax.ShapeDtypeStruct((wr.shape[0], LANES), F32),
        ],
        scratch_shapes=[pltpu.VMEM((wr.shape[0], LANES), F32)],
        compiler_params=_cparams("arbitrary"),
    )(x, zp, zuv, zmg, zo, hf, hb, mods, pm, cnt, pw, ps, sg, sb, sw, sbm, ng,
      wbp, wbm, wbs, wo, l1g, l1b, wr, br)


SCATTER_WINDOW = 128
SCATTER_PARTS = 2


def _scatter_rows(x, idx, n_out, part):
    n = x.shape[0]
    w = x.shape[1] // SCATTER_PARTS
    n_idx = idx.shape[0]
    mesh = plsc.VectorSubcoreMesh(core_axis_name="core", subcore_axis_name="subcore")

    @pl.kernel(out_type=jax.ShapeDtypeStruct((n_out, w), x.dtype), mesh=mesh, scratch_types=[])
    def scatter_kernel(x_hbm, i_hbm, o_hbm):
        def body(x_vmem, i_vmem):
            for k in range(n_idx):
                pltpu.sync_copy(x_vmem, o_hbm.at[i_vmem.at[k]])

        pltpu.emit_pipeline(
            body,
            grid=(n // SCATTER_WINDOW,),
            in_specs=[pl.BlockSpec((SCATTER_WINDOW, w), lambda i: (i, part)),
                      pl.BlockSpec((n_idx, SCATTER_WINDOW), lambda i: (0, i))],
            out_specs=[],
            core_axis_name=("core", "subcore"),
            dimension_semantics=(pltpu.PARALLEL,),
        )(x_hbm, i_hbm)

    return scatter_kernel(x, idx)


def _gather_rows(x, idx):
    w = x.shape[1]
    n_idx = idx.shape[1]
    mesh = plsc.VectorSubcoreMesh(core_axis_name="core", subcore_axis_name="subcore")

    @pl.kernel(out_type=jax.ShapeDtypeStruct((n_idx, w), x.dtype), mesh=mesh, scratch_types=[])
    def gather_kernel(x_hbm, i_hbm, o_hbm):
        def body(i_vmem, o_vmem):
            pltpu.sync_copy(x_hbm.at[i_vmem.at[0]], o_vmem)

        pltpu.emit_pipeline(
            body,
            grid=(n_idx // SCATTER_WINDOW,),
            in_specs=[pl.BlockSpec((1, SCATTER_WINDOW), lambda i: (0, i))],
            out_specs=[pl.BlockSpec((SCATTER_WINDOW, w), lambda i: (i, 0))],
            core_axis_name=("core", "subcore"),
            dimension_semantics=(pltpu.PARALLEL,),
        )(i_hbm, o_hbm)

    return gather_kernel(x, idx)


def _expert_kernel(te_ref, nt_ref, xa_ref, xb_ref, wgu_ref, bgu_ref, wd_ref, bd_ref, ya_ref, yb_ref,
                   wgu_sc, wd_sc):
    i = pl.program_id(0)
    f = wd_ref.shape[0]
    used = i < nt_ref[0]

    @pl.when(jnp.logical_and(used, jnp.logical_or(i == 0, te_ref[i] != te_ref[jnp.maximum(i - 1, 0)])))
    def _():
        for c0 in range(0, 2 * f, 512):
            wgu_sc[:, c0:c0 + 512] = wgu_ref[:, c0:c0 + 512].astype(BF16)
        for c0 in range(0, f, 256):
            wd_sc[c0:c0 + 256, :] = wd_ref[c0:c0 + 256, :].astype(BF16)

    @pl.when(used)
    def _():
        x = _unpack_rows([xa_ref[...], xb_ref[...]]).astype(BF16)
        acc = jnp.zeros((x.shape[0], wd_ref.shape[1]), F32)
        for c0 in range(0, f, 512):
            gate = jnp.dot(x, wgu_sc[:, c0:c0 + 512], preferred_element_type=F32) + bgu_ref[:, c0:c0 + 512]
            up = (jnp.dot(x, wgu_sc[:, f + c0:f + c0 + 512], preferred_element_type=F32)
                  + bgu_ref[:, f + c0:f + c0 + 512])
            gate = jnp.minimum(gate, SWIGLU_LIMIT)
            up = jnp.clip(up, -SWIGLU_LIMIT, SWIGLU_LIMIT)
            glu = gate * _sigmoid(SWIGLU_ALPHA * gate)
            act = ((up + 1.0) * glu).astype(BF16)
            acc += jnp.dot(act, wd_sc[c0:c0 + 512, :], preferred_element_type=F32)
        words = _pack_rows(acc + bd_ref[...])
        ya_ref[...] = words[:, :ya_ref.shape[1]]
        yb_ref[...] = words[:, ya_ref.shape[1]:]

    @pl.when(jnp.logical_not(used))
    def _():
        ya_ref[...] = jnp.zeros_like(ya_ref)
        yb_ref[...] = jnp.zeros_like(yb_ref)


def _experts(tile_expert, n_tiles_used, xa, xb, wgu, bgu, wd, bd, *, layer):
    p = xa.shape[0]
    d = 2 * SCATTER_PARTS * xa.shape[1]
    tm = EXPERT_TILE
    f2 = wgu.shape[3]
    grid_spec = pltpu.PrefetchScalarGridSpec(
        num_scalar_prefetch=2,
        grid=(p // tm,),
        in_specs=[
            pl.BlockSpec((tm, xa.shape[1]), lambda i, te, nt: (i, 0)),
            pl.BlockSpec((tm, xb.shape[1]), lambda i, te, nt: (i, 0)),
            pl.BlockSpec((None, None, d, f2), lambda i, te, nt: (layer, te[i], 0, 0)),
            pl.BlockSpec((None, None, 1, f2), lambda i, te, nt: (layer, te[i], 0, 0)),
            pl.BlockSpec((None, None, f2 // 2, d), lambda i, te, nt: (layer, te[i], 0, 0)),
            pl.BlockSpec((None, None, 1, d), lambda i, te, nt: (layer, te[i], 0, 0)),
        ],
        out_specs=[pl.BlockSpec((tm, xa.shape[1]), lambda i, te, nt: (i, 0)),
                   pl.BlockSpec((tm, xb.shape[1]), lambda i, te, nt: (i, 0))],
        scratch_shapes=[pltpu.VMEM((d, f2), BF16), pltpu.VMEM((f2 // 2, d), BF16)],
    )
    return pl.pallas_call(
        _expert_kernel,
        grid_spec=grid_spec,
        out_shape=[jax.ShapeDtypeStruct(xa.shape, xa.dtype), jax.ShapeDtypeStruct(xb.shape, xb.dtype)],
        compiler_params=_cparams("arbitrary"),
    )(tile_expert, n_tiles_used, xa, xb, wgu, bgu, wd, bd)


def _combine_kernel(x1_ref, ya_ref, yb_ref, route_ref, mod_ref, g_ref, b_ref, *rest, alpha):
    x2_ref = rest[-1]
    d = x1_ref.shape[1]
    tm = x1_ref.shape[0]
    route = jnp.concatenate([route_ref[...], jnp.zeros((LANES - ROUTE_ROWS, tm), F32)], axis=0).T
    f = jnp.zeros(x1_ref.shape, F32)
    for kk in range(TOP_K):
        f += route[:, TOP_K + kk:TOP_K + kk + 1] * _unpack_rows([ya_ref[kk], yb_ref[kk]])
    g2 = mod_ref[:, 5 * d:6 * d]
    x2_ref[...] = _ln_plain(alpha * x1_ref[...] + g2 * f) * g_ref[...] + b_ref[...]


def _combine(x1, yg_a, yg_b, route, mods, g, b, *, seq, alpha, dest=None, dest_rows=None, tile_offset=0):
    t, d = x1.shape
    tm = COMBINE_TILE
    assert t % tm == 0 and seq % tm == 0
    n_batch = mods.shape[0] - 1
    row = lambda i: (i, 0)

    def mod_idx(i):
        return (jnp.minimum((i * tm) // seq, n_batch), 0, 0)

    in_specs = [
        pl.BlockSpec((tm, d), row),
        pl.BlockSpec((TOP_K, tm, yg_a.shape[2]), lambda i: (0, i, 0)),
        pl.BlockSpec((TOP_K, tm, yg_b.shape[2]), lambda i: (0, i, 0)),
        pl.BlockSpec((ROUTE_ROWS, tm), lambda i: (0, i)),
        pl.BlockSpec((None, 1, mods.shape[2]), mod_idx),
        pl.BlockSpec(g.shape, lambda i: (0, 0)),
        pl.BlockSpec(b.shape, lambda i: (0, 0)),
    ]
    args = [x1, yg_a, yg_b, route, mods, g, b]
    aliases = {}
    if dest is not None:
        in_specs.append(pl.BlockSpec(memory_space=pl.ANY))
        args.append(dest)
        aliases = {len(args) - 1: 0}
    return pl.pallas_call(
        functools.partial(_combine_kernel, alpha=alpha),
        grid=(t // tm,),
        in_specs=in_specs,
        out_specs=pl.BlockSpec((tm, d), lambda i: (i + tile_offset, 0)),
        out_shape=jax.ShapeDtypeStruct((t if dest_rows is None else dest_rows, d), F32),
        input_output_aliases=aliases,
        compiler_params=_cparams("parallel"),
    )(*args)


def kernel(x, c, ctx, c_ctx, w_ada, b_ada, w_in, b_in, pool_w, pool_scale, qk_conv_w, qk_conv_b,
           mlstm_norm_g, sgu_ln_g, sgu_ln_b, sgu_w, sgu_b, w_br_pool, w_br_mlstm, w_br_sgu, w_out,
           ln1_g, ln1_b, w_router, b_router, w_gate_up, b_gate_up, w_down, b_down, ln2_g, ln2_b):
    n_batch, seq, d = x.shape
    ctx_len = ctx.shape[1]
    depth = w_in.shape[0]
    n_exp = w_router.shape[2]
    assert ctx_len == ROW_TILE and seq % ROW_TILE == 0 and ROW_TILE % GRID_W == 0
    assert n_batch % N_STREAMS == 0
    alpha = float((2 * depth) ** 0.25)

    rows = ((n_batch + 1 + 7) // 8) * 8
    cc = jnp.zeros((rows, d), F32).at[:n_batch].set(c).at[n_batch].set(c_ctx)
    mods_all = _ada_table(cc, w_ada, b_ada)[:, :n_batch + 1].reshape(depth, n_batch + 1, 1, 6 * d)

    pool_w_ = len(POOL_WINDOWS) * POOL_GROUP_DIM
    mlw = N_HEADS * HEAD_DIM
    col = lambda a, lo, hi: a[..., lo:hi]
    g_lo = pool_w_ + 4 * mlw
    uv_hi = g_lo + _N_GATE_COLS + 2 * pool_w_
    parts = ((pool_w_, pool_w_ + 2 * mlw), (pool_w_ + 2 * mlw, pool_w_ + 3 * mlw), (pool_w_ + 3 * mlw, g_lo),
             (g_lo + _N_GATE_COLS, uv_hi), (0, pool_w_), (uv_hi, w_in.shape[2]))
    part_scale = (1.0, 1.0, 0.5, 1.0, 1.0, 0.5)
    w_main = jnp.concatenate([col(w_in, lo, hi) * sc for (lo, hi), sc in zip(parts, part_scale)],
                             axis=-1).astype(BF16)
    b_main = jnp.concatenate([col(b_in, lo, hi) * sc for (lo, hi), sc in zip(parts, part_scale)],
                             axis=-1)[:, None, :]
    gw, gb = col(w_in, g_lo, g_lo + _N_GATE_COLS), col(b_in, g_lo, g_lo + _N_GATE_COLS)
    nh = N_HEADS
    pick = lambda a, k: jnp.concatenate([a[..., k * nh:(k + 1) * nh], a[..., (k + 2) * nh:(k + 3) * nh]], axis=-1)
    w_gate_t = jnp.swapaxes(jnp.concatenate([pick(gw, 0), pick(gw, 1)], axis=-1), 1, 2).astype(BF16)
    b_gate_t = jnp.concatenate([pick(gb, 0), pick(gb, 1)], axis=-1)[:, :, None]

    pm, cnt = _pool_constants(ctx_len)
    eye = jnp.eye(len(POOL_WINDOWS), dtype=F32)
    pool_bd = jnp.einsum('lgcd,gh->lgchd', pool_w, eye).reshape(depth, pool_w_, pool_w_).astype(BF16)
    sgu_bias_map = jnp.repeat(jnp.swapaxes(sgu_b, 1, 2), sgu_ln_g.shape[1] // sgu_b.shape[1], axis=2)
    w_router_t = jnp.swapaxes(w_router, 1, 2).astype(BF16)
    b_router_t = b_router[:, :, None]
    r1 = lambda a: a[:, None, :]
    b_gu4 = b_gate_up[:, :, None, :]
    b_d4 = b_down[:, :, None, :]

    nb = n_batch // N_STREAMS
    n_lat = nb * seq
    t_all = n_lat + nb * ctx_len
    streams, stream_mods = [], []
    for h in range(N_STREAMS):
        sl = slice(h * nb, (h + 1) * nb)
        streams.append(jnp.concatenate([x[sl].reshape(n_lat, d), ctx[sl].reshape(nb * ctx_len, d)], axis=0))
        stream_mods.append(jnp.concatenate([mods_all[:, sl], mods_all[:, n_batch:n_batch + 1]], axis=1))
    out = None
    for i in range(depth):
        last = i == depth - 1
        for h in range(N_STREAMS):
            stream = streams[h]
            mods = stream_mods[h][i]
            zqk, zv, zo, zuv, zp, zmg, g, gt = _inproj(
                stream, mods, w_main[i], b_main[i], w_gate_t[i], b_gate_t[i],
                qk_conv_w[i], r1(qk_conv_b)[i], n_lat=n_lat, seq=seq, ctx_len=ctx_len)
            hf, hb = _mlstm(zqk, zv, g, gt, n_batch=nb, seq=seq, ctx_len=ctx_len)
            n_rows = n_lat if last else t_all
            consts = (pm, cnt, pool_bd[i], r1(pool_scale)[i], r1(sgu_ln_g)[i], r1(sgu_ln_b)[i],
                      sgu_w[i].astype(BF16), sgu_bias_map[i], r1(0.5 * mlstm_norm_g)[i],
                      w_br_pool[i].astype(BF16), w_br_mlstm[i].astype(BF16), w_br_sgu[i].astype(BF16),
                      (0.5 * w_out[i]).astype(BF16), r1(ln1_g)[i], r1(ln1_b)[i], w_router_t[i], b_router_t[i])
            x1, h2, route, counts = _mix(stream, zp, zuv, zmg, zo, hf, hb, mods, consts,
                                         n_rows=n_rows, n_lat=n_lat, seq=seq, alpha=alpha)

            cnt_e = counts[:, 0].astype(jnp.int32)
            padded = ((cnt_e + EXPERT_TILE - 1) // EXPERT_TILE) * EXPERT_TILE
            ends = jnp.cumsum(padded)
            starts = ends - padded
            e_idx = route[0:TOP_K].astype(jnp.int32)
            rank = route[2 * TOP_K:3 * TOP_K].astype(jnp.int32)
            start_of = jnp.sum(jnp.where(e_idx[..., None] == jnp.arange(n_exp, dtype=jnp.int32), starts, 0),
                               axis=-1)
            pos_t = start_of + rank
            p_rows = n_rows * TOP_K + n_exp * EXPERT_TILE
            n_tiles = p_rows // EXPERT_TILE
            tile_start = jnp.arange(n_tiles, dtype=jnp.int32) * EXPERT_TILE
            tile_expert = jnp.minimum(
                jnp.sum((ends[None, :] <= tile_start[:, None]).astype(jnp.int32), axis=1), n_exp - 1)
            n_used = (ends[-1] // EXPERT_TILE).astype(jnp.int32).reshape(1)
            xs_a = _scatter_rows(h2, pos_t, p_rows, 0)
            xs_b = _scatter_rows(h2, pos_t, p_rows, 1)
            ys_a, ys_b = _experts(tile_expert, n_used, xs_a, xs_b, w_gate_up, b_gu4, w_down, b_d4, layer=i)
            pos_flat = pos_t.reshape(1, TOP_K * n_rows)
            yg_a = _gather_rows(ys_a, pos_flat).reshape(TOP_K, n_rows, -1)
            yg_b = _gather_rows(ys_b, pos_flat).reshape(TOP_K, n_rows, -1)
            if last:
                out = _combine(x1, yg_a, yg_b, route, mods, r1(ln2_g)[i], r1(ln2_b)[i], seq=seq, alpha=alpha,
                               dest=out, dest_rows=n_batch * seq, tile_offset=h * n_lat // COMBINE_TILE)
            else:
                streams[h] = _combine(x1, yg_a, yg_b, route, mods, r1(ln2_g)[i], r1(ln2_b)[i],
                                      seq=seq, alpha=alpha)
    return out.reshape(n_batch, seq, d)
```
